```python
import jax, jax.numpy as jnp
from jax import lax
import numpy as np

D_MODEL = 2048
BATCH = 4
SEQ = 4096
DEPTH = 4

HEAD_DIM = 64
A_Q_HEADS = D_MODEL // (2 * HEAD_DIM)
A_KV_HEADS = 2
A_WINDOW = 128
B_HEADS = D_MODEL // (2 * HEAD_DIM)
C_HEADS = D_MODEL // HEAD_DIM
C_PAIRS = ((128, 1), (512, 4), (2048, 16))
BLOCK = 128
D_FF = ((8 * D_MODEL // 3 + 255) // 256) * 256
N_EVEN = (DEPTH + 1) // 2
N_ODD = DEPTH // 2
EVEN_IN = (A_Q_HEADS + 2 * A_KV_HEADS + 3 * B_HEADS) * HEAD_DIM
EVEN_OUT = (A_Q_HEADS + B_HEADS) * HEAD_DIM
ODD_IN = 3 * C_HEADS * HEAD_DIM
ODD_OUT = C_HEADS * HEAD_DIM
RMS_EPS = 1e-6

kernel_name = 'hybrid_swa_sink_stickbreak_dilated'


def rmsnorm(x, gain):
    xf = x.astype(jnp.float32)
    y = xf * lax.rsqrt(jnp.mean(xf * xf, axis=-1, keepdims=True) + RMS_EPS)
    return (y * gain.astype(jnp.float32)).astype(x.dtype)


def alibi_slopes(n):
    return jnp.exp2(-8.0 * jnp.arange(1, n + 1, dtype=jnp.float32) / n)


def band_blocks(t):
    b, g, l, h, dh = t.shape
    nb = l // BLOCK
    tp = jnp.pad(t, ((0, 0), (0, 0), (BLOCK, 0), (0, 0), (0, 0)))
    prev = tp[:, :, :l].reshape(b, g, nb, BLOCK, h, dh)
    cur = t.reshape(b, g, nb, BLOCK, h, dh)
    return jnp.concatenate([prev, cur], axis=3)


def band_geometry(nb):
    i = jnp.arange(BLOCK)[:, None]
    j = jnp.arange(2 * BLOCK)[None, :]
    dist = i + BLOCK - j
    key_pos = jnp.arange(nb)[:, None] * BLOCK - BLOCK + jnp.arange(2 * BLOCK)[None, :]
    return dist, key_pos >= 0


def swa_sink_attention(q, k, v, sinks, slopes):
    b, s, hq, dh = q.shape
    hkv = k.shape[2]
    grp = hq // hkv
    nb = s // BLOCK
    qb = q.reshape(b, nb, BLOCK, hkv, grp, dh)
    kb = band_blocks(k[:, None])[:, 0]
    vb = band_blocks(v[:, None])[:, 0]
    sc = jnp.einsum('bnqhgd,bnkhd->bnhgqk', qb, kb).astype(jnp.float32) * dh ** -0.5
    dist, kvalid = band_geometry(nb)
    valid = (dist >= 0) & (dist < A_WINDOW) & kvalid[:, None, :]
    sc = sc - slopes.reshape(hkv, grp, 1, 1) * dist.astype(jnp.float32)
    sc = jnp.where(valid[:, None, None], sc, -jnp.inf)
    sink = sinks.astype(jnp.float32).reshape(hkv, grp)[..., None]
    m = jnp.maximum(sc.max(-1), sink)
    p = jnp.exp(sc - m[..., None])
    den = p.sum(-1) + jnp.exp(sink - m)
    o = jnp.einsum('bnhgqk,bnkhd->bnqhgd', p, vb) / jnp.moveaxis(den, -1, 2)[..., None]
    return o.reshape(b, s, hq * dh).astype(q.dtype)


def stick_breaking_attention(q, k, v):
    b, s, h, dh = q.shape
    nb = s // BLOCK
    qt = q.transpose(0, 2, 1, 3)
    kt = k.transpose(0, 2, 1, 3)
    vt = v.transpose(0, 2, 1, 3)
    key_pos = jnp.arange(s)

    def one_block(n):
        qb = lax.dynamic_slice_in_dim(qt, n * BLOCK, BLOCK, axis=2)
        z = jnp.einsum('bhqd,bhkd->bhqk', qb, kt).astype(jnp.float32) * dh ** -0.5
        q_pos = n * BLOCK + jnp.arange(BLOCK)
        causal = key_pos[None, :] < q_pos[:, None]
        log_keep = jnp.where(causal, jax.nn.log_sigmoid(-z), 0.0)
        between = lax.cumsum(log_keep, axis=3, reverse=True) - log_keep
        w = jnp.where(causal, jnp.exp(jax.nn.log_sigmoid(z) + between), 0.0)
        return jnp.einsum('bhqk,bhkd->bhqd', w, vt)

    o = lax.map(one_block, jnp.arange(nb))
    return o.transpose(1, 0, 3, 2, 4).reshape(b, s, h * dh).astype(q.dtype)


def dilated_branch(q, k, v, slopes, window, dilation):
    b, s, h, dh = q.shape
    l = s // dilation
    lp = -(-l // BLOCK) * BLOCK
    nb = lp // BLOCK

    def by_residue(t):
        t = t.reshape(b, l, dilation, h, dh).transpose(0, 2, 1, 3, 4)
        return jnp.pad(t, ((0, 0), (0, 0), (0, lp - l), (0, 0), (0, 0)))

    qb = by_residue(q).reshape(b, dilation, nb, BLOCK, h, dh)
    kb = band_blocks(by_residue(k))
    vb = band_blocks(by_residue(v))
    sc = jnp.einsum('bgnqhd,bgnkhd->bgnhqk', qb, kb).astype(jnp.float32) * dh ** -0.5
    dist, kvalid = band_geometry(nb)
    valid = (dist >= 0) & (dist <= window // dilation) & kvalid[:, None, :]
    sc = sc - slopes[:, None, None] * (dist * dilation).astype(jnp.float32)
    sc = jnp.where(valid[:, None], sc, -jnp.inf)
    m = sc.max(-1)
    p = jnp.exp(sc - m[..., None])
    den = p.sum(-1)
    num = jnp.einsum('bgnhqk,bgnkhd->bgnqhd', p, vb)

    def back(t):
        t = t.reshape((b, dilation, lp) + t.shape[4:])[:, :, :l]
        t = jnp.moveaxis(t, 1, 2)
        return t.reshape((b, s) + t.shape[3:])

    return back(num), back(jnp.moveaxis(m, -1, 3)), back(jnp.moveaxis(den, -1, 3))


def dilated_mixture(q, k, v, slopes):
    b, s, h, dh = q.shape
    branches = [dilated_branch(q, k, v, slopes, w, d) for (w, d) in C_PAIRS]
    m_all = jnp.stack([br[1] for br in branches])
    wts = jnp.exp(m_all - m_all.max(0))
    num = wts[0][..., None] * branches[0][0]
    den = wts[0] * branches[0][2]
    for idx in range(1, len(branches)):
        num = num + wts[idx][..., None] * branches[idx][0]
        den = den + wts[idx] * branches[idx][2]
    out = num / den[..., None]
    return out.reshape(b, s, h * dh).astype(q.dtype)


def swiglu(h, w_gate, w_up, w_down):
    return (jax.nn.silu(h @ w_gate) * (h @ w_up)) @ w_down


def setup_inputs(seed: int = 0) -> dict:
    key = jax.random.key(seed)
    ks = jax.random.split(key, 15)

    def normal(k, shape, scale):
        return jax.random.normal(k, shape, jnp.float32) * scale

    out_scale = (2.0 * DEPTH) ** -0.5
    return {
        'x': normal(ks[0], (BATCH, SEQ, D_MODEL), 1.0),
        'attn_norm': 1.0 + normal(ks[1], (DEPTH, D_MODEL), 0.1),
        'ffn_norm': 1.0 + normal(ks[2], (DEPTH, D_MODEL), 0.1),
        'even_w_in': normal(ks[3], (N_EVEN, D_MODEL, EVEN_IN), D_MODEL ** -0.5),
        'even_q_norm': 1.0 + normal(ks[4], (N_EVEN, HEAD_DIM), 0.1),
        'even_k_norm': 1.0 + normal(ks[5], (N_EVEN, HEAD_DIM), 0.1),
        'even_sinks': normal(ks[6], (N_EVEN, A_Q_HEADS), 0.5),
        'even_w_out': normal(ks[7], (N_EVEN, EVEN_OUT, D_MODEL), EVEN_OUT ** -0.5 * out_scale),
        'odd_w_in': normal(ks[8], (N_ODD, D_MODEL, ODD_IN), D_MODEL ** -0.5),
        'odd_q_norm': 1.0 + normal(ks[9], (N_ODD, HEAD_DIM), 0.1),
        'odd_k_norm': 1.0 + normal(ks[10], (N_ODD, HEAD_DIM), 0.1),
        'odd_w_out': normal(ks[11], (N_ODD, ODD_OUT, D_MODEL), ODD_OUT ** -0.5 * out_scale),
        'ffn_w_gate': normal(ks[12], (DEPTH, D_MODEL, D_FF), D_MODEL ** -0.5),
        'ffn_w_up': normal(ks[13], (DEPTH, D_MODEL, D_FF), D_MODEL ** -0.5),
        'ffn_w_down': normal(ks[14], (DEPTH, D_FF, D_MODEL), D_FF ** -0.5 * out_scale),
    }


def reference(x, attn_norm, ffn_norm, even_w_in, even_q_norm, even_k_norm, even_sinks,
              even_w_out, odd_w_in, odd_q_norm, odd_k_norm, odd_w_out,
              ffn_w_gate, ffn_w_up, ffn_w_down):
    b, s, _ = x.shape
    slopes_a = alibi_slopes(A_Q_HEADS)
    slopes_c = alibi_slopes(C_HEADS)
    split_even = (A_Q_HEADS * HEAD_DIM,
                  (A_Q_HEADS + A_KV_HEADS) * HEAD_DIM,
                  (A_Q_HEADS + 2 * A_KV_HEADS) * HEAD_DIM,
                  (A_Q_HEADS + 2 * A_KV_HEADS + B_HEADS) * HEAD_DIM,
                  (A_Q_HEADS + 2 * A_KV_HEADS + 2 * B_HEADS) * HEAD_DIM)
    for i in range(DEPTH):
        j = i // 2
        h = rmsnorm(x, attn_norm[i])
        if i % 2 == 0:
            proj = h @ even_w_in[j]
            qa, ka, va, qb, kb, vb = jnp.split(proj, split_even, axis=-1)
            qa = rmsnorm(qa.reshape(b, s, A_Q_HEADS, HEAD_DIM), even_q_norm[j])
            ka = rmsnorm(ka.reshape(b, s, A_KV_HEADS, HEAD_DIM), even_k_norm[j])
            va = va.reshape(b, s, A_KV_HEADS, HEAD_DIM)
            oa = swa_sink_attention(qa, ka, va, even_sinks[j], slopes_a)
            ob = stick_breaking_attention(qb.reshape(b, s, B_HEADS, HEAD_DIM),
                                          kb.reshape(b, s, B_HEADS, HEAD_DIM),
                                          vb.reshape(b, s, B_HEADS, HEAD_DIM))
            mixed = jnp.concatenate([oa, ob], axis=-1) @ even_w_out[j]
        else:
            proj = h @ odd_w_in[j]
            qc, kc, vc = jnp.split(proj, 3, axis=-1)
            qc = rmsnorm(qc.reshape(b, s, C_HEADS, HEAD_DIM), odd_q_norm[j])
            kc = rmsnorm(kc.reshape(b, s, C_HEADS, HEAD_DIM), odd_k_norm[j])
            vc = vc.reshape(b, s, C_HEADS, HEAD_DIM)
            mixed = dilated_mixture(qc, kc, vc, slopes_c) @ odd_w_out[j]
        x = x + mixed
        h = rmsnorm(x, ffn_norm[i])
        x = x + swiglu(h, ffn_w_gate[i], ffn_w_up[i], ffn_w_down[i])
    return x
```

```python
import functools

import jax
import jax.numpy as jnp
from jax import lax
from jax.experimental import pallas as pl
from jax.experimental.pallas import tpu as pltpu

HEAD_DIM = 64
LANES = 128
BLOCK = 128
A_Q_HEADS = 16
A_KV_HEADS = 2
B_HEADS = 16
C_HEADS = 32
C_PAIRS = ((128, 1), (512, 4), (2048, 16))
RMS_EPS = 1e-6
MASKED = -1e30
SB_UNDERFLOW = -90.0
VMEM_LIMIT = 56 * 1024 * 1024

F32 = jnp.float32
BF16 = jnp.bfloat16


def _dot(a, b):
    return jnp.dot(a, b, preferred_element_type=F32)


def _dot_t(a, b):
    return lax.dot_general(a, b, (((1,), (1,)), ((), ())), preferred_element_type=F32)


def _params(*sem):
    return pltpu.CompilerParams(dimension_semantics=sem, vmem_limit_bytes=VMEM_LIMIT)


def _proj_body(x_ref, g_ref, w_ref, cg_ref, cf_ref, s_ref, o_ref, h_scr, *, n_norm_tiles, tn):
    j = pl.program_id(1)

    @pl.when(j == 0)
    def _():
        x = x_ref[...]
        inv = lax.rsqrt(jnp.mean(x * x, axis=-1, keepdims=True) + RMS_EPS)
        h_scr[...] = ((x * inv) * g_ref[...]).astype(BF16)

    y = _dot(h_scr[...], w_ref[...])

    @pl.when(j < n_norm_tiles)
    def _():
        for c in range(tn // 256):
            sl = slice(c * 256, (c + 1) * 256)
            yc = y[:, sl]
            ss = _dot((yc * yc).astype(BF16), s_ref[...])
            inv = lax.rsqrt(ss * (1.0 / HEAD_DIM) + RMS_EPS)
            cg = cg_ref[:, sl]
            scale = jnp.where(cf_ref[:, sl] > 0.0, inv * cg, cg)
            o_ref[:, sl] = (yc * scale).astype(BF16)

    @pl.when(j >= n_norm_tiles)
    def _():
        o_ref[...] = (y * cg_ref[...]).astype(BF16)


def _norm_proj(x2, gain, w, colgain, colflag, n_norm_cols, tm, tn):
    n, d = x2.shape
    n_out = w.shape[1]
    assert n % tm == 0 and n_out % tn == 0 and tn % 256 == 0
    n_norm_tiles = -(-n_norm_cols // tn)
    idx = jnp.arange(256) // HEAD_DIM
    seg = (idx[:, None] == idx[None, :]).astype(BF16)
    return pl.pallas_call(
        functools.partial(_proj_body, n_norm_tiles=n_norm_tiles, tn=tn),
        grid=(n // tm, n_out // tn),
        in_specs=[
            pl.BlockSpec((tm, d), lambda i, j: (i, 0)),
            pl.BlockSpec((1, d), lambda i, j: (0, 0)),
            pl.BlockSpec((d, tn), lambda i, j: (0, j)),
            pl.BlockSpec((1, tn), lambda i, j: (0, j)),
            pl.BlockSpec((1, tn), lambda i, j: (0, j)),
            pl.BlockSpec((256, 256), lambda i, j: (0, 0)),
        ],
        out_specs=pl.BlockSpec((tm, tn), lambda i, j: (i, j)),
        out_shape=jax.ShapeDtypeStruct((n, n_out), BF16),
        scratch_shapes=[pltpu.VMEM((tm, d), BF16)],
        compiler_params=_params("parallel", "arbitrary"),
        name="norm_proj",
    )(x2, gain.reshape(1, d), w, colgain.reshape(1, n_out), colflag.reshape(1, n_out), seg)


def _out_body(*refs, n_pairs):
    x_ref = refs[0]
    o_ref = refs[1 + 2 * n_pairs]
    acc = x_ref[...]
    for p in range(n_pairs):
        acc = acc + _dot(refs[1 + 2 * p][...], refs[2 + 2 * p][...])
    o_ref[...] = acc


def _out_proj(x2, pairs, tm, tn):
    n, d = x2.shape
    in_specs = [pl.BlockSpec((tm, tn), lambda i, j: (i, j))]
    args = [x2]
    for a, w in pairs:
        kk = a.shape[1]
        in_specs.append(pl.BlockSpec((tm, kk), lambda i, j: (i, 0)))
        in_specs.append(pl.BlockSpec((kk, tn), lambda i, j: (0, j)))
        args += [a, w]
    return pl.pallas_call(
        functools.partial(_out_body, n_pairs=len(pairs)),
        grid=(n // tm, d // tn),
        in_specs=in_specs,
        out_specs=pl.BlockSpec((tm, tn), lambda i, j: (i, j)),
        out_shape=jax.ShapeDtypeStruct((n, d), F32),
        compiler_params=_params("parallel", "arbitrary"),
        name="out_proj",
    )(*args)


def _ffn_body(x_ref, g_ref, wg_ref, wu_ref, wd_ref, o_ref, h_scr):
    j = pl.program_id(1)

    @pl.when(j == 0)
    def _():
        x = x_ref[...]
        inv = lax.rsqrt(jnp.mean(x * x, axis=-1, keepdims=True) + RMS_EPS)
        h_scr[...] = ((x * inv) * g_ref[...]).astype(BF16)
        o_ref[...] = x

    h = h_scr[...]
    gate = _dot(h, wg_ref[...])
    up = _dot(h, wu_ref[...])
    act = (gate * (1.0 / (1.0 + jnp.exp(-gate)))) * up
    o_ref[...] += _dot(act.astype(BF16), wd_ref[...])


def _ffn(x2, gain, wg, wu, wd, tm, tf):
    n, d = x2.shape
    dff = wg.shape[1]
    assert n % tm == 0 and dff % tf == 0
    return pl.pallas_call(
        _ffn_body,
        grid=(n // tm, dff // tf),
        in_specs=[
            pl.BlockSpec((tm, d), lambda i, j: (i, 0)),
            pl.BlockSpec((1, d), lambda i, j: (0, 0)),
            pl.BlockSpec((d, tf), lambda i, j: (0, j)),
            pl.BlockSpec((d, tf), lambda i, j: (0, j)),
            pl.BlockSpec((tf, d), lambda i, j: (j, 0)),
        ],
        out_specs=pl.BlockSpec((tm, d), lambda i, j: (i, 0)),
        out_shape=jax.ShapeDtypeStruct((n, d), F32),
        scratch_shapes=[pltpu.VMEM((tm, d), BF16)],
        compiler_params=_params("parallel", "arbitrary"),
        name="ffn",
    )(x2, gain.reshape(1, d), wg, wu, wd)


def _stack_heads(q):
    lo = lax.broadcasted_iota(jnp.int32, q.shape, 1) < HEAD_DIM
    zero = jnp.zeros_like(q)
    return jnp.concatenate([jnp.where(lo, q, zero), jnp.where(lo, zero, q)], axis=0)


def _unstack_heads(t):
    lo = lax.broadcasted_iota(jnp.int32, (BLOCK, LANES), 1) < HEAD_DIM
    return jnp.where(lo, t[:BLOCK], t[BLOCK:])


def _band_softmax(lhs, kp, kc, vp, vc, bias, first_block, sink=None):
    pen = jnp.where(first_block, MASKED, 0.0)
    sp = _dot_t(lhs, kp) + bias[:, :BLOCK] + pen
    sc = _dot_t(lhs, kc) + bias[:, BLOCK:]
    m = jnp.maximum(jnp.max(sp, axis=1, keepdims=True), jnp.max(sc, axis=1, keepdims=True))
    if sink is not None:
        m = jnp.maximum(m, sink)
    pp = jnp.exp(sp - m)
    pc = jnp.exp(sc - m)
    den = jnp.sum(pp, axis=1, keepdims=True) + jnp.sum(pc, axis=1, keepdims=True)
    if sink is not None:
        den = den + jnp.exp(sink - m)
    num = _dot(pp.astype(BF16), vp) + _dot(pc.astype(BF16), vc)
    return num, m, den


def _band_bias(slopes, window_max, dist_scale, strict):
    i = jnp.arange(BLOCK)[:, None]
    j = jnp.arange(2 * BLOCK)[None, :]
    dist = i + BLOCK - j
    valid = (dist >= 0) & ((dist < window_max) if strict else (dist <= window_max))
    alibi = slopes[:, None, None] * (dist * dist_scale).astype(F32)[None]
    return jnp.where(valid[None], -alibi, MASKED)


def _pair_rows(t):
    h = t.shape[0]
    return t.reshape((h // 2, 2 * t.shape[1]) + t.shape[2:])


def _swa_body(q_ref, k_ref, v_ref, bias_ref, sink_ref, o_ref, kd_scr, vd_scr):
    s = q_ref.shape[1]
    c = pl.program_id(1)
    kv_head = c // (A_Q_HEADS // A_KV_HEADS // 2)
    sel = (lax.broadcasted_iota(jnp.int32, (s, LANES), 1) // HEAD_DIM) == kv_head
    kf = jnp.where(sel, k_ref[0].astype(F32), 0.0)
    kd_scr[...] = (kf + pltpu.roll(kf, HEAD_DIM, axis=1)).astype(BF16)
    vf = jnp.where(sel, v_ref[0].astype(F32), 0.0)
    vd_scr[...] = (vf + pltpu.roll(vf, HEAD_DIM, axis=1)).astype(BF16)

    def block(n, carry):
        r0 = pl.multiple_of(n * BLOCK, BLOCK)
        p0 = pl.multiple_of(jnp.maximum(n - 1, 0) * BLOCK, BLOCK)
        lhs = _stack_heads(q_ref[0, pl.ds(r0, BLOCK), :])
        num, _, den = _band_softmax(
            lhs, kd_scr[pl.ds(p0, BLOCK), :], kd_scr[pl.ds(r0, BLOCK), :],
            vd_scr[pl.ds(p0, BLOCK), :], vd_scr[pl.ds(r0, BLOCK), :],
            bias_ref[0], n == 0, sink=sink_ref[0][:, :1])
        o_ref[0, pl.ds(r0, BLOCK), :] = _unstack_heads(num / den).astype(BF16)
        return carry

    lax.fori_loop(0, s // BLOCK, block, 0)


def _swa_attention(proj, sinks, slopes):
    b, s, _ = proj.shape
    n_blocks = A_Q_HEADS // 2
    k_col = A_Q_HEADS * HEAD_DIM // LANES
    bias = _pair_rows(_band_bias(slopes, BLOCK, 1, strict=True))
    sink = _pair_rows(jnp.broadcast_to(sinks.astype(F32)[:, None, None],
                                       (A_Q_HEADS, BLOCK, LANES)))
    return pl.pallas_call(
        _swa_body,
        grid=(b, n_blocks),
        in_specs=[
            pl.BlockSpec((1, s, LANES), lambda i, c: (i, 0, c)),
            pl.BlockSpec((1, s, LANES), lambda i, c: (i, 0, k_col)),
            pl.BlockSpec((1, s, LANES), lambda i, c: (i, 0, k_col + 1)),
            pl.BlockSpec((1, 2 * BLOCK, 2 * BLOCK), lambda i, c: (c, 0, 0)),
            pl.BlockSpec((1, 2 * BLOCK, LANES), lambda i, c: (c, 0, 0)),
        ],
        out_specs=pl.BlockSpec((1, s, LANES), lambda i, c: (i, 0, c)),
        out_shape=jax.ShapeDtypeStruct((b, s, n_blocks * LANES), BF16),
        scratch_shapes=[pltpu.VMEM((s, LANES), BF16), pltpu.VMEM((s, LANES), BF16)],
        compiler_params=_params("parallel", "arbitrary"),
        name="swa_attention",
    )(proj, proj, proj, bias, sink)


def _sb_body(q_ref, k_ref, v_ref, u_ref, o_ref, acc_scr, run_scr):
    s = q_ref.shape[1]
    row = lax.broadcasted_iota(jnp.int32, (2 * BLOCK, BLOCK), 0) % BLOCK
    col = lax.broadcasted_iota(jnp.int32, (2 * BLOCK, BLOCK), 1)

    def q_block(n, carry):
        r0 = pl.multiple_of(n * BLOCK, BLOCK)
        lhs = _stack_heads(q_ref[0, pl.ds(r0, BLOCK), :])
        acc_scr[...] = jnp.zeros_like(acc_scr)
        run_scr[...] = jnp.zeros_like(run_scr)

        def more(st):
            j, alive = st
            return jnp.logical_and(j >= 0, alive > 0)

        def key_block(st):
            j, _ = st
            k0 = pl.multiple_of(j * BLOCK, BLOCK)
            z = _dot_t(lhs, k_ref[0, pl.ds(k0, BLOCK), :])
            valid = col < row + jnp.where(j < n, BLOCK, 0)
            log_keep = -(jnp.maximum(z, 0.0) + jnp.log(1.0 + jnp.exp(-jnp.abs(z))))
            log_keep = jnp.where(valid, log_keep, 0.0)
            hi = log_keep.astype(BF16)
            lo = (log_keep - hi.astype(F32)).astype(BF16)
            cs = _dot(hi, u_ref[...]) + _dot(lo, u_ref[...])
            run = run_scr[...]
            w = jnp.where(valid, jnp.exp(z + cs[:, :BLOCK] + run), 0.0)
            acc_scr[...] += _dot(w.astype(BF16), v_ref[0, pl.ds(k0, BLOCK), :])
            run = run + cs[:, BLOCK:]
            run_scr[...] = run
            return j - 1, (jnp.max(run) > SB_UNDERFLOW).astype(jnp.int32)

        lax.while_loop(more, key_block, (n, jnp.int32(1)))
        o_ref[0, pl.ds(r0, BLOCK), :] = _unstack_heads(acc_scr[...]).astype(BF16)
        return carry

    lax.fori_loop(0, s // BLOCK, q_block, 0)


def _sb_attention(proj, q_col, k_col, v_col):
    b, s, _ = proj.shape
    n_blocks = B_HEADS // 2
    kk = jnp.arange(BLOCK)
    suffix = (kk[:, None] >= kk[None, :]).astype(BF16)
    u = jnp.concatenate([suffix, jnp.ones((BLOCK, BLOCK), BF16)], axis=1)
    return pl.pallas_call(
        _sb_body,
        grid=(b, n_blocks),
        in_specs=[
            pl.BlockSpec((1, s, LANES), lambda i, c: (i, 0, q_col + c)),
            pl.BlockSpec((1, s, LANES), lambda i, c: (i, 0, k_col + c)),
            pl.BlockSpec((1, s, LANES), lambda i, c: (i, 0, v_col + c)),
            pl.BlockSpec((BLOCK, 2 * BLOCK), lambda i, c: (0, 0)),
        ],
        out_specs=pl.BlockSpec((1, s, LANES), lambda i, c: (i, 0, c)),
        out_shape=jax.ShapeDtypeStruct((b, s, n_blocks * LANES), BF16),
        scratch_shapes=[pltpu.VMEM((2 * BLOCK, LANES), F32), pltpu.VMEM((2 * BLOCK, LANES), F32)],
        compiler_params=_params("parallel", "arbitrary"),
        name="stick_breaking",
    )(proj, proj, proj, u)


def _rows(start, dilation):
    if dilation == 1:
        return pl.ds(start, BLOCK)
    return pl.ds(start, BLOCK, stride=dilation)


def _dil_body(q_ref, k_ref, v_ref, bias_ref, o_ref, qf, kf, vf, m_scr, den_scr, num_scr):
    s = q_ref.shape[1]
    qf[...] = q_ref[0].astype(F32)
    kf[...] = k_ref[0].astype(F32)
    vf[...] = v_ref[0].astype(F32)

    for branch, (_, dilation) in enumerate(C_PAIRS):
        def tile(t, carry, branch=branch, dilation=dilation):
            n = t // dilation
            r = t % dilation
            cur = n * (BLOCK * dilation) + r
            prev = jnp.maximum(n - 1, 0) * (BLOCK * dilation) + r
            rc = _rows(cur, dilation)
            rp = _rows(prev, dilation)
            lhs = _stack_heads(qf[rc, :].astype(BF16))
            num, m, den = _band_softmax(
                lhs, kf[rp, :].astype(BF16), kf[rc, :].astype(BF16),
                vf[rp, :].astype(BF16), vf[rc, :].astype(BF16),
                bias_ref[branch, 0], n == 0)
            num = _unstack_heads(num)
            m = _unstack_heads(jnp.broadcast_to(m, (2 * BLOCK, LANES)))
            den = _unstack_heads(jnp.broadcast_to(den, (2 * BLOCK, LANES)))
            if branch > 0:
                m_old = m_scr[rc, :]
                m_new = jnp.maximum(m_old, m)
                a_old = jnp.exp(m_old - m_new)
                a_cur = jnp.exp(m - m_new)
                num = a_old * num_scr[rc, :] + a_cur * num
                den = a_old * den_scr[rc, :] + a_cur * den
                m = m_new
            m_scr[rc, :] = m
            den_scr[rc, :] = den
            num_scr[rc, :] = num
            return carry

        lax.fori_loop(0, s // BLOCK, tile, 0)

    o_ref[0] = (num_scr[...] / den_scr[...]).astype(BF16)


def _dilated_attention(proj, slopes):
    b, s, _ = proj.shape
    n_blocks = C_HEADS // 2
    bias = jnp.stack([_pair_rows(_band_bias(slopes, w // d, d, strict=False))
                      for (w, d) in C_PAIRS])
    scr = [pltpu.VMEM((s, LANES), F32) for _ in range(6)]
    return pl.pallas_call(
        _dil_body,
        grid=(b, n_blocks),
        in_specs=[
            pl.BlockSpec((1, s, LANES), lambda i, c: (i, 0, c)),
            pl.BlockSpec((1, s, LANES), lambda i, c: (i, 0, n_blocks + c)),
            pl.BlockSpec((1, s, LANES), lambda i, c: (i, 0, 2 * n_blocks + c)),
            pl.BlockSpec((len(C_PAIRS), 1, 2 * BLOCK, 2 * BLOCK), lambda i, c: (0, c, 0, 0)),
        ],
        out_specs=pl.BlockSpec((1, s, LANES), lambda i, c: (i, 0, c)),
        out_shape=jax.ShapeDtypeStruct((b, s, n_blocks * LANES), BF16),
        scratch_shapes=scr,
        compiler_params=_params("parallel", "arbitrary"),
        name="dilated_mixture",
    )(proj, proj, proj, bias)


def _alibi_slopes(n):
    return jnp.exp2(-8.0 * jnp.arange(1, n + 1, dtype=F32) / n)


def kernel(x, attn_norm, ffn_norm, even_w_in, even_q_norm, even_k_norm, even_sinks, even_w_out,
           odd_w_in, odd_q_norm, odd_k_norm, odd_w_out, ffn_w_gate, ffn_w_up, ffn_w_down):
    b, s, d = x.shape
    depth = attn_norm.shape[0]
    scale = HEAD_DIM ** -0.5
    slopes_a = _alibi_slopes(A_Q_HEADS)
    slopes_c = _alibi_slopes(C_HEADS)
    qa, kva, hb = A_Q_HEADS * HEAD_DIM, A_KV_HEADS * HEAD_DIM, B_HEADS * HEAD_DIM
    hc = C_HEADS * HEAD_DIM
    ones = lambda n: jnp.ones((n,), F32)
    zeros = lambda n: jnp.zeros((n,), F32)

    x2 = x.reshape(b * s, d)
    for i in range(depth):
        j = i // 2
        if i % 2 == 0:
            colgain = jnp.concatenate([
                jnp.tile(even_q_norm[j].astype(F32), A_Q_HEADS) * scale,
                jnp.tile(even_k_norm[j].astype(F32), A_KV_HEADS),
                ones(kva), ones(hb) * scale, ones(hb), ones(hb)])
            colflag = jnp.concatenate([ones(qa + kva), zeros(kva + 3 * hb)])
            proj = _norm_proj(x2, attn_norm[i], even_w_in[j].astype(BF16), colgain, colflag,
                              qa + kva, tm=1024, tn=256).reshape(b, s, -1)
            oa = _swa_attention(proj, even_sinks[j], slopes_a)
            qb_col = (qa + 2 * kva) // LANES
            ob = _sb_attention(proj, qb_col, qb_col + hb // LANES, qb_col + 2 * hb // LANES)
            w_out = even_w_out[j].astype(BF16)
            pairs = [(oa.reshape(b * s, qa), w_out[:qa]), (ob.reshape(b * s, hb), w_out[qa:])]
        else:
            colgain = jnp.concatenate([
                jnp.tile(odd_q_norm[j].astype(F32), C_HEADS) * scale,
                jnp.tile(odd_k_norm[j].astype(F32), C_HEADS), ones(hc)])
            colflag = jnp.concatenate([ones(2 * hc), zeros(hc)])
            proj = _norm_proj(x2, attn_norm[i], odd_w_in[j].astype(BF16), colgain, colflag,
                              2 * hc, tm=1024, tn=512).reshape(b, s, -1)
            oc = _dilated_attention(proj, slopes_c)
            pairs = [(oc.reshape(b * s, hc), odd_w_out[j].astype(BF16))]
        x2 = _out_proj(x2, pairs, tm=512, tn=1024)
        x2 = _ffn(x2, ffn_norm[i], ffn_w_gate[i].astype(BF16), ffn_w_up[i].astype(BF16),
                  ffn_w_down[i].astype(BF16), tm=512, tf=512)
    return x2.reshape(b, s, d)
```

```python
import functools

import jax
import jax.numpy as jnp
from jax import lax
from jax.experimental import pallas as pl
from jax.experimental.pallas import tpu as pltpu

HEAD_DIM = 64
LANES = 128
BLOCK = 128
A_Q_HEADS = 16
A_KV_HEADS = 2
B_HEADS = 16
C_HEADS = 32
C_PAIRS = ((128, 1), (512, 4), (2048, 16))
RMS_EPS = 1e-6
MASKED = -1e30
SB_UNDERFLOW = -88.0
VMEM_LIMIT = 56 * 1024 * 1024
TILE_UNROLL = 4

F32 = jnp.float32
BF16 = jnp.bfloat16


def _dot(a, b):
    return jnp.dot(a, b, preferred_element_type=F32)


def _dot_t(a, b):
    return lax.dot_general(a, b, (((1,), (1,)), ((), ())), preferred_element_type=F32)


def _params(*sem):
    return pltpu.CompilerParams(dimension_semantics=sem, vmem_limit_bytes=VMEM_LIMIT)


def _proj_body(x_ref, g_ref, w_ref, cg_ref, cf_ref, s_ref, o_ref, h_scr, *, n_norm_tiles, tn):
    j = pl.program_id(1)

    @pl.when(j == 0)
    def _():
        x = x_ref[...]
        inv = lax.rsqrt(jnp.mean(x * x, axis=-1, keepdims=True) + RMS_EPS)
        h_scr[...] = ((x * inv) * g_ref[...]).astype(BF16)

    y = _dot(h_scr[...], w_ref[...])

    @pl.when(j < n_norm_tiles)
    def _():
        for c in range(tn // 256):
            sl = slice(c * 256, (c + 1) * 256)
            yc = y[:, sl]
            ss = _dot((yc * yc).astype(BF16), s_ref[...])
            inv = lax.rsqrt(ss * (1.0 / HEAD_DIM) + RMS_EPS)
            cg = cg_ref[:, sl]
            scale = jnp.where(cf_ref[:, sl] > 0.0, inv * cg, cg)
            o_ref[:, sl] = (yc * scale).astype(BF16)

    @pl.when(j >= n_norm_tiles)
    def _():
        o_ref[...] = (y * cg_ref[...]).astype(BF16)


def _norm_proj(x2, gain, w, colgain, colflag, n_norm_cols, tm, tn):
    n, d = x2.shape
    n_out = w.shape[1]
    assert n % tm == 0 and n_out % tn == 0 and tn % 256 == 0
    n_norm_tiles = -(-n_norm_cols // tn)
    idx = jnp.arange(256) // HEAD_DIM
    seg = (idx[:, None] == idx[None, :]).astype(BF16)
    return pl.pallas_call(
        functools.partial(_proj_body, n_norm_tiles=n_norm_tiles, tn=tn),
        grid=(n // tm, n_out // tn),
        in_specs=[
            pl.BlockSpec((tm, d), lambda i, j: (i, 0)),
            pl.BlockSpec((1, d), lambda i, j: (0, 0)),
            pl.BlockSpec((d, tn), lambda i, j: (0, j)),
            pl.BlockSpec((1, tn), lambda i, j: (0, j)),
            pl.BlockSpec((1, tn), lambda i, j: (0, j)),
            pl.BlockSpec((256, 256), lambda i, j: (0, 0)),
        ],
        out_specs=pl.BlockSpec((tm, tn), lambda i, j: (i, j)),
        out_shape=jax.ShapeDtypeStruct((n, n_out), BF16),
        scratch_shapes=[pltpu.VMEM((tm, d), BF16)],
        compiler_params=_params("parallel", "arbitrary"),
        name="norm_proj",
    )(x2, gain.reshape(1, d), w, colgain.reshape(1, n_out), colflag.reshape(1, n_out), seg)


def _out_body(*refs, n_pairs):
    x_ref = refs[0]
    o_ref = refs[1 + 2 * n_pairs]
    acc = x_ref[...]
    for p in range(n_pairs):
        acc = acc + _dot(refs[1 + 2 * p][...], refs[2 + 2 * p][...])
    o_ref[...] = acc


def _out_proj(x2, pairs, tm, tn):
    n, d = x2.shape
    in_specs = [pl.BlockSpec((tm, tn), lambda i, j: (i, j))]
    args = [x2]
    for a, w in pairs:
        kk = a.shape[1]
        in_specs.append(pl.BlockSpec((tm, kk), lambda i, j: (i, 0)))
        in_specs.append(pl.BlockSpec((kk, tn), lambda i, j: (0, j)))
        args += [a, w]
    return pl.pallas_call(
        functools.partial(_out_body, n_pairs=len(pairs)),
        grid=(n // tm, d // tn),
        in_specs=in_specs,
        out_specs=pl.BlockSpec((tm, tn), lambda i, j: (i, j)),
        out_shape=jax.ShapeDtypeStruct((n, d), F32),
        compiler_params=_params("parallel", "arbitrary"),
        name="out_proj",
    )(*args)


def _ffn_body(x_ref, g_ref, wg_ref, wu_ref, wd_ref, o_ref, h_scr):
    j = pl.program_id(1)

    @pl.when(j == 0)
    def _():
        x = x_ref[...]
        inv = lax.rsqrt(jnp.mean(x * x, axis=-1, keepdims=True) + RMS_EPS)
        h_scr[...] = ((x * inv) * g_ref[...]).astype(BF16)
        o_ref[...] = x

    h = h_scr[...]
    gate = _dot(h, wg_ref[...])
    up = _dot(h, wu_ref[...])
    act = (gate * (1.0 / (1.0 + jnp.exp(-gate)))) * up
    o_ref[...] += _dot(act.astype(BF16), wd_ref[...])


def _ffn(x2, gain, wg, wu, wd, tm, tf):
    n, d = x2.shape
    dff = wg.shape[1]
    assert n % tm == 0 and dff % tf == 0
    return pl.pallas_call(
        _ffn_body,
        grid=(n // tm, dff // tf),
        in_specs=[
            pl.BlockSpec((tm, d), lambda i, j: (i, 0)),
            pl.BlockSpec((1, d), lambda i, j: (0, 0)),
            pl.BlockSpec((d, tf), lambda i, j: (0, j)),
            pl.BlockSpec((d, tf), lambda i, j: (0, j)),
            pl.BlockSpec((tf, d), lambda i, j: (j, 0)),
        ],
        out_specs=pl.BlockSpec((tm, d), lambda i, j: (i, 0)),
        out_shape=jax.ShapeDtypeStruct((n, d), F32),
        scratch_shapes=[pltpu.VMEM((tm, d), BF16)],
        compiler_params=_params("parallel", "arbitrary"),
        name="ffn",
    )(x2, gain.reshape(1, d), wg, wu, wd)


def _stack_heads(q):
    lo = lax.broadcasted_iota(jnp.int32, q.shape, 1) < HEAD_DIM
    zero = jnp.zeros_like(q)
    return jnp.concatenate([jnp.where(lo, q, zero), jnp.where(lo, zero, q)], axis=0)


def _unstack_heads(t):
    lo = lax.broadcasted_iota(jnp.int32, (BLOCK, LANES), 1) < HEAD_DIM
    return jnp.where(lo, t[:BLOCK], t[BLOCK:])


def _band_softmax(lhs, k_band, v_band, bias, sink=None):
    s = _dot_t(lhs, k_band) + bias
    m = jnp.broadcast_to(jnp.max(s, axis=1, keepdims=True), (2 * BLOCK, LANES))
    if sink is not None:
        m = jnp.maximum(m, sink)
    p = jnp.exp(s - jnp.concatenate([m, m], axis=1)).astype(BF16)
    ones = jnp.ones((2 * BLOCK, LANES), BF16)
    ext = _dot(p, jnp.concatenate([v_band, ones], axis=1))
    num, den = ext[:, :LANES], ext[:, LANES:]
    if sink is not None:
        den = den + jnp.exp(sink - m)
    return num, m, den


def _band_bias(slopes, window_max, dist_scale, strict):
    i = jnp.arange(BLOCK)[:, None]
    j = jnp.arange(2 * BLOCK)[None, :]
    dist = i + BLOCK - j
    valid = (dist >= 0) & ((dist < window_max) if strict else (dist <= window_max))
    alibi = slopes[:, None, None] * (dist * dist_scale).astype(F32)[None]
    h = slopes.shape[0]
    table = jnp.stack([jnp.where(valid[None], -alibi, MASKED),
                       jnp.where((valid & (j >= BLOCK))[None], -alibi, MASKED)], axis=1)
    table = table.reshape(h // 2, 2, 2, BLOCK, 2 * BLOCK).transpose(0, 2, 1, 3, 4)
    return table.reshape(h // 2, 2, 2 * BLOCK, 2 * BLOCK)


def _pair_rows(t):
    h = t.shape[0]
    return t.reshape((h // 2, 2 * t.shape[1]) + t.shape[2:])


def _swa_body(q_ref, k_ref, v_ref, bias_ref, sink_ref, o_ref, kd_scr, vd_scr):
    s = q_ref.shape[1]
    c = pl.program_id(1)
    kv_head = c // (A_Q_HEADS // A_KV_HEADS // 2)
    sel = (lax.broadcasted_iota(jnp.int32, (s, LANES), 1) // HEAD_DIM) == kv_head
    kf = jnp.where(sel, k_ref[0].astype(F32), 0.0)
    kd_scr[...] = (kf + pltpu.roll(kf, HEAD_DIM, axis=1)).astype(BF16)
    vf = jnp.where(sel, v_ref[0].astype(F32), 0.0)
    vd_scr[...] = (vf + pltpu.roll(vf, HEAD_DIM, axis=1)).astype(BF16)

    def block(n, carry):
        r0 = pl.multiple_of(n * BLOCK, BLOCK)
        p0 = pl.multiple_of(jnp.maximum(n - 1, 0) * BLOCK, BLOCK)
        lhs = _stack_heads(q_ref[0, pl.ds(r0, BLOCK), :])
        k_band = jnp.concatenate([kd_scr[pl.ds(p0, BLOCK), :], kd_scr[pl.ds(r0, BLOCK), :]], axis=0)
        v_band = jnp.concatenate([vd_scr[pl.ds(p0, BLOCK), :], vd_scr[pl.ds(r0, BLOCK), :]], axis=0)
        first = (n == 0).astype(jnp.int32)
        num, _, den = _band_softmax(lhs, k_band, v_band, bias_ref[0, first], sink=sink_ref[0])
        out = _unstack_heads(num) / _unstack_heads(den)
        o_ref[0, pl.ds(r0, BLOCK), :] = out.astype(BF16)
        return carry

    lax.fori_loop(0, s // BLOCK, block, 0, unroll=TILE_UNROLL)


def _swa_attention(proj, sinks, slopes):
    b, s, _ = proj.shape
    n_blocks = A_Q_HEADS // 2
    k_col = A_Q_HEADS * HEAD_DIM // LANES
    bias = _band_bias(slopes, BLOCK, 1, strict=True)
    sink = _pair_rows(jnp.broadcast_to(sinks.astype(F32)[:, None, None],
                                       (A_Q_HEADS, BLOCK, LANES)))
    return pl.pallas_call(
        _swa_body,
        grid=(b, n_blocks),
        in_specs=[
            pl.BlockSpec((1, s, LANES), lambda i, c: (i, 0, c)),
            pl.BlockSpec((1, s, LANES), lambda i, c: (i, 0, k_col)),
            pl.BlockSpec((1, s, LANES), lambda i, c: (i, 0, k_col + 1)),
            pl.BlockSpec((1, 2, 2 * BLOCK, 2 * BLOCK), lambda i, c: (c, 0, 0, 0)),
            pl.BlockSpec((1, 2 * BLOCK, LANES), lambda i, c: (c, 0, 0)),
        ],
        out_specs=pl.BlockSpec((1, s, LANES), lambda i, c: (i, 0, c)),
        out_shape=jax.ShapeDtypeStruct((b, s, n_blocks * LANES), BF16),
        scratch_shapes=[pltpu.VMEM((s, LANES), BF16), pltpu.VMEM((s, LANES), BF16)],
        compiler_params=_params("parallel", "arbitrary"),
        name="swa_attention",
    )(proj, proj, proj, bias, sink)


def _sb_tile(lhs, k, v, u, valid, run):
    z = _dot_t(lhs, k)
    log_keep = -(jnp.maximum(z, 0.0) + jnp.log(1.0 + jnp.exp(-jnp.abs(z))))
    if valid is not None:
        log_keep = jnp.where(valid, log_keep, 0.0)
    hi = log_keep.astype(BF16)
    lo = (log_keep - hi.astype(F32)).astype(BF16)
    cs = _dot(hi, u) + _dot(lo, u)
    log_w = z + cs[:, :BLOCK]
    if run is not None:
        log_w = log_w + run
    w = jnp.exp(log_w)
    if valid is not None:
        w = jnp.where(valid, w, 0.0)
    total = cs[:, BLOCK:]
    return _dot(w.astype(BF16), v), (total if run is None else run + total)


def _sb_body(q_ref, k_ref, v_ref, u_ref, o_ref, acc_scr, run_scr, *, pairs):
    s = q_ref.shape[1]
    row = lax.broadcasted_iota(jnp.int32, (2 * BLOCK, BLOCK), 0) % BLOCK
    col = lax.broadcasted_iota(jnp.int32, (2 * BLOCK, BLOCK), 1)

    def q_block(n, carry):
        r0 = pl.multiple_of(n * BLOCK, BLOCK)
        p0 = pl.multiple_of(jnp.maximum(n - 1, 0) * BLOCK, BLOCK)
        u = u_ref[...]
        valid_diag = col < row
        valid_prev = col < row + jnp.where(n > 0, BLOCK, -BLOCK)
        alive = jnp.float32(-jnp.inf)
        for g in range(pairs):
            ls = slice(g * LANES, (g + 1) * LANES)
            lhs = _stack_heads(q_ref[0, pl.ds(r0, BLOCK), ls])
            o_diag, run = _sb_tile(lhs, k_ref[0, pl.ds(r0, BLOCK), ls],
                                   v_ref[0, pl.ds(r0, BLOCK), ls], u, valid_diag, None)
            o_prev, run = _sb_tile(lhs, k_ref[0, pl.ds(p0, BLOCK), ls],
                                   v_ref[0, pl.ds(p0, BLOCK), ls], u, valid_prev, run)
            acc_scr[g] = o_diag + o_prev
            run_scr[g] = run
            alive = jnp.maximum(alive, jnp.max(run))

        def more(st):
            j, alive = st
            return jnp.logical_and(j >= 0, alive > SB_UNDERFLOW)

        def key_block(st):
            j, _ = st
            k0 = pl.multiple_of(j * BLOCK, BLOCK)
            alive = jnp.float32(-jnp.inf)
            for g in range(pairs):
                ls = slice(g * LANES, (g + 1) * LANES)
                lhs = _stack_heads(q_ref[0, pl.ds(r0, BLOCK), ls])
                o_blk, run = _sb_tile(lhs, k_ref[0, pl.ds(k0, BLOCK), ls],
                                      v_ref[0, pl.ds(k0, BLOCK), ls], u, None, run_scr[g])
                acc_scr[g] += o_blk
                run_scr[g] = run
                alive = jnp.maximum(alive, jnp.max(run))
            return j - 1, alive

        lax.while_loop(more, key_block, (n - 2, alive))
        for g in range(pairs):
            ls = slice(g * LANES, (g + 1) * LANES)
            o_ref[0, pl.ds(r0, BLOCK), ls] = _unstack_heads(acc_scr[g]).astype(BF16)
        return carry

    lax.fori_loop(0, s // BLOCK, q_block, 0)


def _sb_attention(proj, q_col, k_col, v_col, pairs):
    b, s, _ = proj.shape
    width = pairs * LANES
    n_steps = B_HEADS // 2 // pairs
    assert q_col % pairs == 0 and k_col % pairs == 0 and v_col % pairs == 0
    kk = jnp.arange(BLOCK)
    suffix = (kk[:, None] >= kk[None, :]).astype(BF16)
    u = jnp.concatenate([suffix, jnp.ones((BLOCK, BLOCK), BF16)], axis=1)
    scr = pltpu.VMEM((pairs, 2 * BLOCK, LANES), F32)
    return pl.pallas_call(
        functools.partial(_sb_body, pairs=pairs),
        grid=(b, n_steps),
        in_specs=[
            pl.BlockSpec((1, s, width), lambda i, c: (i, 0, q_col // pairs + c)),
            pl.BlockSpec((1, s, width), lambda i, c: (i, 0, k_col // pairs + c)),
            pl.BlockSpec((1, s, width), lambda i, c: (i, 0, v_col // pairs + c)),
            pl.BlockSpec((BLOCK, 2 * BLOCK), lambda i, c: (0, 0)),
        ],
        out_specs=pl.BlockSpec((1, s, width), lambda i, c: (i, 0, c)),
        out_shape=jax.ShapeDtypeStruct((b, s, B_HEADS * HEAD_DIM), BF16),
        scratch_shapes=[scr, scr],
        compiler_params=_params("parallel", "arbitrary"),
        name="stick_breaking",
    )(proj, proj, proj, u)


RESIDUE_STEP = 4


def _to_residue_major(src, dst, seg):
    part = seg // RESIDUE_STEP
    for base in range(0, src.shape[0], seg):
        for r in range(RESIDUE_STEP):
            dst[base + r * part:base + (r + 1) * part, :] = (
                src[pl.ds(base + r, part, stride=RESIDUE_STEP), :])


def _from_residue_major(src, dst, seg):
    part = seg // RESIDUE_STEP
    for base in range(0, src.shape[0], seg):
        for r in range(RESIDUE_STEP):
            dst[pl.ds(base + r, part, stride=RESIDUE_STEP), :] = (
                src[base + r * part:base + (r + 1) * part, :])


def _dil_body(q_ref, k_ref, v_ref, bias_ref, o_ref, tmp_a, tmp_b,
              q4, k4, v4, q16, k16, v16, st_a, st_b):
    s = q_ref.shape[1]
    for src, d4, d16 in ((q_ref, q4, q16), (k_ref, k4, k16), (v_ref, v4, v16)):
        tmp_a[...] = src[0].astype(F32)
        _to_residue_major(tmp_a, tmp_b, s)
        d4[...] = tmp_b[...].astype(BF16)
        _to_residue_major(tmp_b, tmp_a, s // RESIDUE_STEP)
        d16[...] = tmp_a[...].astype(BF16)

    layouts = ((q_ref.at[0], k_ref.at[0], v_ref.at[0]), (q4, k4, v4), (q16, k16, v16))
    state, spare = st_a, st_b
    for branch, (_, dilation) in enumerate(C_PAIRS):
        qb, kb, vb = layouts[branch]
        class_blocks = s // dilation // BLOCK
        if branch > 0:
            for a in range(3):
                _to_residue_major(state.at[a], spare.at[a],
                                  s if branch == 1 else s // RESIDUE_STEP)
            state, spare = spare, state

        def tile(t, carry, branch=branch, qb=qb, kb=kb, vb=vb, state=state,
                 class_blocks=class_blocks):
            r0 = pl.multiple_of(t * BLOCK, BLOCK)
            p0 = pl.multiple_of(jnp.maximum(t - 1, 0) * BLOCK, BLOCK)
            rows = pl.ds(r0, BLOCK)
            lhs = _stack_heads(qb[rows, :])
            k_band = jnp.concatenate([kb[pl.ds(p0, BLOCK), :], kb[rows, :]], axis=0)
            v_band = jnp.concatenate([vb[pl.ds(p0, BLOCK), :], vb[rows, :]], axis=0)
            first = (t % class_blocks == 0).astype(jnp.int32)
            num, m, den = _band_softmax(lhs, k_band, v_band, bias_ref[branch, 0, first])
            num, m, den = _unstack_heads(num), _unstack_heads(m), _unstack_heads(den)
            if branch > 0:
                m_old = state[0, rows, :]
                m_new = jnp.maximum(m_old, m)
                a_old = jnp.exp(m_old - m_new)
                a_cur = jnp.exp(m - m_new)
                num = a_old * state[2, rows, :] + a_cur * num
                den = a_old * state[1, rows, :] + a_cur * den
                m = m_new
            state[0, rows, :] = m
            state[1, rows, :] = den
            state[2, rows, :] = num
            return carry

        lax.fori_loop(0, s // BLOCK, tile, 0, unroll=TILE_UNROLL)

    state[2] = state[2] / state[1]
    _from_residue_major(state.at[2], spare.at[2], s // RESIDUE_STEP)
    _from_residue_major(spare.at[2], state.at[2], s)
    o_ref[0] = state[2].astype(BF16)


def _dilated_attention(proj, slopes):
    b, s, _ = proj.shape
    n_blocks = C_HEADS // 2
    assert C_PAIRS[0][1] == 1 and C_PAIRS[1][1] == RESIDUE_STEP
    assert C_PAIRS[2][1] == RESIDUE_STEP ** 2
    bias = jnp.stack([_band_bias(slopes, w // d, d, strict=False)
                      for (w, d) in C_PAIRS])
    scr = ([pltpu.VMEM((s, LANES), F32)] * 2 + [pltpu.VMEM((s, LANES), BF16)] * 6
           + [pltpu.VMEM((3, s, LANES), F32)] * 2)
    return pl.pallas_call(
        _dil_body,
        grid=(b, n_blocks),
        in_specs=[
            pl.BlockSpec((1, s, LANES), lambda i, c: (i, 0, c)),
            pl.BlockSpec((1, s, LANES), lambda i, c: (i, 0, n_blocks + c)),
            pl.BlockSpec((1, s, LANES), lambda i, c: (i, 0, 2 * n_blocks + c)),
            pl.BlockSpec((len(C_PAIRS), 1, 2, 2 * BLOCK, 2 * BLOCK),
                         lambda i, c: (0, c, 0, 0, 0)),
        ],
        out_specs=pl.BlockSpec((1, s, LANES), lambda i, c: (i, 0, c)),
        out_shape=jax.ShapeDtypeStruct((b, s, n_blocks * LANES), BF16),
        scratch_shapes=scr,
        compiler_params=_params("parallel", "arbitrary"),
        name="dilated_mixture",
    )(proj, proj, proj, bias)


def _alibi_slopes(n):
    return jnp.exp2(-8.0 * jnp.arange(1, n + 1, dtype=F32) / n)


def kernel(x, attn_norm, ffn_norm, even_w_in, even_q_norm, even_k_norm, even_sinks, even_w_out,
           odd_w_in, odd_q_norm, odd_k_norm, odd_w_out, ffn_w_gate, ffn_w_up, ffn_w_down):
    b, s, d = x.shape
    depth = attn_norm.shape[0]
    scale = HEAD_DIM ** -0.5
    slopes_a = _alibi_slopes(A_Q_HEADS)
    slopes_c = _alibi_slopes(C_HEADS)
    qa, kva, hb = A_Q_HEADS * HEAD_DIM, A_KV_HEADS * HEAD_DIM, B_HEADS * HEAD_DIM
    hc = C_HEADS * HEAD_DIM
    ones = lambda n: jnp.ones((n,), F32)
    zeros = lambda n: jnp.zeros((n,), F32)

    x2 = x.reshape(b * s, d)
    for i in range(depth):
        j = i // 2
        if i % 2 == 0:
            colgain = jnp.concatenate([
                jnp.tile(even_q_norm[j].astype(F32), A_Q_HEADS) * scale,
                jnp.tile(even_k_norm[j].astype(F32), A_KV_HEADS),
                ones(kva), ones(hb) * scale, ones(hb), ones(hb)])
            colflag = jnp.concatenate([ones(qa + kva), zeros(kva + 3 * hb)])
            proj = _norm_proj(x2, attn_norm[i], even_w_in[j].astype(BF16), colgain, colflag,
                              qa + kva, tm=1024, tn=256).reshape(b, s, -1)
            oa = _swa_attention(proj, even_sinks[j], slopes_a)
            qb_col = (qa + 2 * kva) // LANES
            ob = _sb_attention(proj, qb_col, qb_col + hb // LANES, qb_col + 2 * hb // LANES,
                               pairs=2)
            w_out = even_w_out[j].astype(BF16)
            pairs = [(oa.reshape(b * s, qa), w_out[:qa]), (ob.reshape(b * s, hb), w_out[qa:])]
        else:
            colgain = jnp.concatenate([
                jnp.tile(odd_q_norm[j].astype(F32), C_HEADS) * scale,
                jnp.tile(odd_k_norm[j].astype(F32), C_HEADS), ones(hc)])
            colflag = jnp.concatenate([ones(2 * hc), zeros(hc)])
            proj = _norm_proj(x2, attn_norm[i], odd_w_in[j].astype(BF16), colgain, colflag,
                              2 * hc, tm=1024, tn=512).reshape(b, s, -1)
            oc = _dilated_attention(proj, slopes_c)
            pairs = [(oc.reshape(b * s, hc), odd_w_out[j].astype(BF16))]
        x2 = _out_proj(x2, pairs, tm=512, tn=1024)
        x2 = _ffn(x2, ffn_norm[i], ffn_w_gate[i].astype(BF16), ffn_w_up[i].astype(BF16),
                  ffn_w_down[i].astype(BF16), tm=512, tf=512)
    return x2.reshape(b, s, d)
```

```python
import functools

import jax
import jax.numpy as jnp
from jax import lax
from jax.experimental import pallas as pl
from jax.experimental.pallas import tpu as pltpu

HEAD_DIM = 64
LANES = 128
BLOCK = 128
A_Q_HEADS = 16
A_KV_HEADS = 2
B_HEADS = 16
C_HEADS = 32
C_PAIRS = ((128, 1), (512, 4), (2048, 16))
RMS_EPS = 1e-6
MASKED = -1e30
LOG2E = 1.4426950408889634
SB_UNDERFLOW = -127.0
VMEM_LIMIT = 56 * 1024 * 1024
TILE_UNROLL = 16

F32 = jnp.float32
BF16 = jnp.bfloat16


def _dot(a, b):
    return jnp.dot(a, b, preferred_element_type=F32)


def _dot_t(a, b):
    return lax.dot_general(a, b, (((1,), (1,)), ((), ())), preferred_element_type=F32)


def _params(*sem):
    return pltpu.CompilerParams(dimension_semantics=sem, vmem_limit_bytes=VMEM_LIMIT)


def _proj_body(x_ref, g_ref, w_ref, cg_ref, cf_ref, s_ref, o_ref, h_scr, *, n_norm_tiles, tn):
    j = pl.program_id(1)

    @pl.when(j == 0)
    def _():
        x = x_ref[...]
        inv = lax.rsqrt(jnp.mean(x * x, axis=-1, keepdims=True) + RMS_EPS)
        h_scr[...] = ((x * inv) * g_ref[...]).astype(BF16)

    y = _dot(h_scr[...], w_ref[...])

    @pl.when(j < n_norm_tiles)
    def _():
        for c in range(tn // 256):
            sl = slice(c * 256, (c + 1) * 256)
            yc = y[:, sl]
            ss = _dot((yc * yc).astype(BF16), s_ref[...])
            inv = lax.rsqrt(ss * (1.0 / HEAD_DIM) + RMS_EPS)
            cg = cg_ref[:, sl]
            scale = jnp.where(cf_ref[:, sl] > 0.0, inv * cg, cg)
            o_ref[:, sl] = (yc * scale).astype(BF16)

    @pl.when(j >= n_norm_tiles)
    def _():
        o_ref[...] = (y * cg_ref[...]).astype(BF16)


def _norm_proj(x2, gain, w, colgain, colflag, n_norm_cols, tm, tn):
    n, d = x2.shape
    n_out = w.shape[1]
    assert n % tm == 0 and n_out % tn == 0 and tn % 256 == 0
    n_norm_tiles = -(-n_norm_cols // tn)
    idx = jnp.arange(256) // HEAD_DIM
    seg = (idx[:, None] == idx[None, :]).astype(BF16)
    return pl.pallas_call(
        functools.partial(_proj_body, n_norm_tiles=n_norm_tiles, tn=tn),
        grid=(n // tm, n_out // tn),
        in_specs=[
            pl.BlockSpec((tm, d), lambda i, j: (i, 0)),
            pl.BlockSpec((1, d), lambda i, j: (0, 0)),
            pl.BlockSpec((d, tn), lambda i, j: (0, j)),
            pl.BlockSpec((1, tn), lambda i, j: (0, j)),
            pl.BlockSpec((1, tn), lambda i, j: (0, j)),
            pl.BlockSpec((256, 256), lambda i, j: (0, 0)),
        ],
        out_specs=pl.BlockSpec((tm, tn), lambda i, j: (i, j)),
        out_shape=jax.ShapeDtypeStruct((n, n_out), BF16),
        scratch_shapes=[pltpu.VMEM((tm, d), BF16)],
        compiler_params=_params("parallel", "arbitrary"),
        name="norm_proj",
    )(x2, gain.reshape(1, d), w, colgain.reshape(1, n_out), colflag.reshape(1, n_out), seg)


def _out_body(*refs, n_pairs):
    x_ref = refs[0]
    o_ref = refs[1 + 2 * n_pairs]
    acc = x_ref[...]
    for p in range(n_pairs):
        acc = acc + _dot(refs[1 + 2 * p][...], refs[2 + 2 * p][...])
    o_ref[...] = acc


def _out_proj(x2, pairs, tm, tn):
    n, d = x2.shape
    in_specs = [pl.BlockSpec((tm, tn), lambda i, j: (i, j))]
    args = [x2]
    for a, w in pairs:
        kk = a.shape[1]
        in_specs.append(pl.BlockSpec((tm, kk), lambda i, j: (i, 0)))
        in_specs.append(pl.BlockSpec((kk, tn), lambda i, j: (0, j)))
        args += [a, w]
    return pl.pallas_call(
        functools.partial(_out_body, n_pairs=len(pairs)),
        grid=(n // tm, d // tn),
        in_specs=in_specs,
        out_specs=pl.BlockSpec((tm, tn), lambda i, j: (i, j)),
        out_shape=jax.ShapeDtypeStruct((n, d), F32),
        compiler_params=_params("parallel", "arbitrary"),
        name="out_proj",
    )(*args)


def _ffn_body(x_ref, g_ref, wg_ref, wu_ref, wd_ref, o_ref, h_scr):
    j = pl.program_id(1)

    @pl.when(j == 0)
    def _():
        x = x_ref[...]
        inv = lax.rsqrt(jnp.mean(x * x, axis=-1, keepdims=True) + RMS_EPS)
        h_scr[...] = ((x * inv) * g_ref[...]).astype(BF16)
        o_ref[...] = x

    h = h_scr[...]
    gate = _dot(h, wg_ref[...])
    up = _dot(h, wu_ref[...])
    act = (gate * (1.0 / (1.0 + jnp.exp(-gate)))) * up
    o_ref[...] += _dot(act.astype(BF16), wd_ref[...])


def _ffn(x2, gain, wg, wu, wd, tm, tf):
    n, d = x2.shape
    dff = wg.shape[1]
    assert n % tm == 0 and dff % tf == 0
    return pl.pallas_call(
        _ffn_body,
        grid=(n // tm, dff // tf),
        in_specs=[
            pl.BlockSpec((tm, d), lambda i, j: (i, 0)),
            pl.BlockSpec((1, d), lambda i, j: (0, 0)),
            pl.BlockSpec((d, tf), lambda i, j: (0, j)),
            pl.BlockSpec((d, tf), lambda i, j: (0, j)),
            pl.BlockSpec((tf, d), lambda i, j: (j, 0)),
        ],
        out_specs=pl.BlockSpec((tm, d), lambda i, j: (i, 0)),
        out_shape=jax.ShapeDtypeStruct((n, d), F32),
        scratch_shapes=[pltpu.VMEM((tm, d), BF16)],
        compiler_params=_params("parallel", "arbitrary"),
        name="ffn",
    )(x2, gain.reshape(1, d), wg, wu, wd)


def _stack_heads(q):
    lo = lax.broadcasted_iota(jnp.int32, q.shape, 1) < HEAD_DIM
    zero = jnp.zeros_like(q)
    return jnp.concatenate([jnp.where(lo, q, zero), jnp.where(lo, zero, q)], axis=0)


def _unstack_heads(t):
    lo = lax.broadcasted_iota(jnp.int32, (BLOCK, LANES), 1) < HEAD_DIM
    return jnp.where(lo, t[:BLOCK], t[BLOCK:])


def _split_heads(v):
    lo = lax.broadcasted_iota(jnp.int32, v.shape, 1) < HEAD_DIM
    zero = jnp.zeros_like(v)
    return jnp.where(lo, v, zero), jnp.where(lo, zero, v)


def _band_softmax(lhs, k_band, va_band, vb_band, bias, sink=None):
    s = _dot_t(lhs, k_band) + bias
    m = jnp.broadcast_to(jnp.max(s, axis=1, keepdims=True), (2 * BLOCK, LANES))
    if sink is not None:
        m = jnp.maximum(m, sink)
    p = jnp.exp2(s - jnp.concatenate([m, m], axis=1)).astype(BF16)
    p_cat = jnp.concatenate([p[:BLOCK], p[BLOCK:]], axis=1)
    lo = lax.broadcasted_iota(jnp.int32, (2 * BLOCK, LANES), 1) < HEAD_DIM
    ones_a = jnp.where(lo, 1.0, 0.0).astype(BF16)
    ones_b = jnp.where(lo, 0.0, 1.0).astype(BF16)
    rhs = jnp.concatenate([jnp.concatenate([va_band, vb_band], axis=0),
                           jnp.concatenate([ones_a, ones_b], axis=0)], axis=1)
    ext = _dot(p_cat, rhs)
    return ext[:, :LANES], _unstack_heads(m), ext[:, LANES:]


def _band_bias(slopes, window_max, dist_scale, strict):
    i = jnp.arange(BLOCK)[:, None]
    j = jnp.arange(2 * BLOCK)[None, :]
    dist = i + BLOCK - j
    valid = (dist >= 0) & ((dist < window_max) if strict else (dist <= window_max))
    alibi = (slopes[:, None, None] * (dist * dist_scale).astype(F32)[None]) * LOG2E
    h = slopes.shape[0]
    table = jnp.stack([jnp.where(valid[None], -alibi, MASKED),
                       jnp.where((valid & (j >= BLOCK))[None], -alibi, MASKED)], axis=1)
    table = table.reshape(h // 2, 2, 2, BLOCK, 2 * BLOCK).transpose(0, 2, 1, 3, 4)
    return table.reshape(h // 2, 2, 2 * BLOCK, 2 * BLOCK)


def _pair_rows(t):
    h = t.shape[0]
    return t.reshape((h // 2, 2 * t.shape[1]) + t.shape[2:])


def _band_rows(ref, n, lanes=slice(None)):
    r0 = pl.multiple_of(n * BLOCK, BLOCK)
    p0 = pl.multiple_of(jnp.maximum(n - 1, 0) * BLOCK, BLOCK)
    return jnp.concatenate([ref[pl.ds(p0, BLOCK), lanes], ref[pl.ds(r0, BLOCK), lanes]], axis=0)


def _swa_body(q_ref, k_ref, v_ref, bias_ref, sink_ref, o_ref, kd_scr, va_scr, vb_scr):
    s = q_ref.shape[1]
    c = pl.program_id(1)
    kv_head = c // (A_Q_HEADS // A_KV_HEADS // 2)
    lane_head = lax.broadcasted_iota(jnp.int32, (s, LANES), 1) // HEAD_DIM
    sel = lane_head == kv_head
    kf = jnp.where(sel, k_ref[0].astype(F32), 0.0)
    kd_scr[...] = (kf + pltpu.roll(kf, HEAD_DIM, axis=1)).astype(BF16)
    vf = jnp.where(sel, v_ref[0].astype(F32), 0.0)
    vr = pltpu.roll(vf, HEAD_DIM, axis=1)
    va_scr[...] = jnp.where(kv_head == 0, vf, vr).astype(BF16)
    vb_scr[...] = jnp.where(kv_head == 0, vr, vf).astype(BF16)

    def block(n, carry):
        rows = pl.ds(pl.multiple_of(n * BLOCK, BLOCK), BLOCK)
        lhs = _stack_heads(q_ref[0, rows, :])
        first = (n == 0).astype(jnp.int32)
        num, m, den = _band_softmax(lhs, _band_rows(kd_scr, n), _band_rows(va_scr, n),
                                    _band_rows(vb_scr, n), bias_ref[0, first],
                                    sink=sink_ref[0, :2 * BLOCK])
        den = den + jnp.exp2(sink_ref[0, 2 * BLOCK:] - m)
        o_ref[0, rows, :] = (num / den).astype(BF16)
        return carry

    lax.fori_loop(0, s // BLOCK, block, 0, unroll=TILE_UNROLL)


def _swa_attention(proj, sinks, slopes):
    b, s, _ = proj.shape
    n_blocks = A_Q_HEADS // 2
    k_col = A_Q_HEADS * HEAD_DIM // LANES
    bias = _band_bias(slopes, BLOCK, 1, strict=True)
    sink2 = sinks.astype(F32) * LOG2E
    stacked = _pair_rows(jnp.broadcast_to(sink2[:, None, None], (A_Q_HEADS, BLOCK, LANES)))
    by_lane = jnp.broadcast_to(jnp.repeat(sink2, HEAD_DIM).reshape(n_blocks, 1, LANES),
                               (n_blocks, BLOCK, LANES))
    sink = jnp.concatenate([stacked, by_lane], axis=1)
    return pl.pallas_call(
        _swa_body,
        grid=(b, n_blocks),
        in_specs=[
            pl.BlockSpec((1, s, LANES), lambda i, c: (i, 0, c)),
            pl.BlockSpec((1, s, LANES), lambda i, c: (i, 0, k_col)),
            pl.BlockSpec((1, s, LANES), lambda i, c: (i, 0, k_col + 1)),
            pl.BlockSpec((1, 2, 2 * BLOCK, 2 * BLOCK), lambda i, c: (c, 0, 0, 0)),
            pl.BlockSpec((1, 3 * BLOCK, LANES), lambda i, c: (c, 0, 0)),
        ],
        out_specs=pl.BlockSpec((1, s, LANES), lambda i, c: (i, 0, c)),
        out_shape=jax.ShapeDtypeStruct((b, s, n_blocks * LANES), BF16),
        scratch_shapes=[pltpu.VMEM((s, LANES), BF16)] * 3,
        compiler_params=_params("parallel", "arbitrary"),
        name="swa_attention",
    )(proj, proj, proj, bias, sink)


def _sb_log_keep(z):
    return -(jnp.maximum(z, 0.0) + jnp.log2(1.0 + jnp.exp2(-jnp.abs(z))))


def _hi_lo(x):
    hi = x.astype(BF16)
    lo = (x - hi.astype(F32)).astype(BF16)
    return jnp.concatenate([hi, lo], axis=1)


def _sb_weighted_values(w, va, vb):
    w = w.astype(BF16)
    return _dot(jnp.concatenate([w[:BLOCK], w[BLOCK:]], axis=1),
                jnp.concatenate([va, vb], axis=0))


def _sb_band(lhs, k_band, va_band, vb_band, u_band, valid):
    z = _dot_t(lhs, k_band)
    log_keep = jnp.where(valid, _sb_log_keep(z), 0.0)
    cs = _dot(_hi_lo(log_keep), u_band)
    w = jnp.where(valid, jnp.exp2(z + cs), 0.0)
    total = jnp.broadcast_to(cs[:, :1], (2 * BLOCK, LANES))
    return _sb_weighted_values(w, va_band, vb_band), total


def _sb_block(lhs, k, va, vb, u_block, run):
    z = _dot_t(lhs, k)
    cs = _dot(_hi_lo(_sb_log_keep(z)), u_block)
    w = jnp.exp2(z + cs[:, :BLOCK] + run)
    return _sb_weighted_values(w, va, vb), run + cs[:, BLOCK:]


def _sb_body(q_ref, k_ref, v_ref, ub_ref, u_ref, o_ref, va_scr, vb_scr, acc_scr, run_scr,
             *, pairs, q_unroll):
    s = q_ref.shape[1]
    lo = lax.broadcasted_iota(jnp.int32, (s, pairs * LANES), 1) % LANES < HEAD_DIM
    v_all = v_ref[0]
    va_scr[...] = jnp.where(lo, v_all, jnp.zeros_like(v_all))
    vb_scr[...] = jnp.where(lo, jnp.zeros_like(v_all), v_all)
    row = lax.broadcasted_iota(jnp.int32, (2 * BLOCK, 2 * BLOCK), 0) % BLOCK + BLOCK
    col = lax.broadcasted_iota(jnp.int32, (2 * BLOCK, 2 * BLOCK), 1)
    k2 = k_ref.at[0]

    def q_group(i, carry):
        blocks = [i * q_unroll + qi for qi in range(q_unroll)]
        alive = []
        for qi, n in enumerate(blocks):
            rows = pl.ds(pl.multiple_of(n * BLOCK, BLOCK), BLOCK)
            first_key = jnp.where(n > 0, 0, BLOCK)
            valid = jnp.logical_and(col < row, col >= first_key)
            alive_n = jnp.float32(-jnp.inf)
            for g in range(pairs):
                ls = slice(g * LANES, (g + 1) * LANES)
                lhs = _stack_heads(q_ref[0, rows, ls])
                out, total = _sb_band(lhs, _band_rows(k2, n, ls), _band_rows(va_scr, n, ls),
                                      _band_rows(vb_scr, n, ls), ub_ref[...], valid)
                acc_scr[qi, g] = out
                run_scr[qi, g] = total
                alive_n = jnp.maximum(alive_n, jnp.max(total))
            alive.append(alive_n)

        for qi, n in enumerate(blocks):
            rows = pl.ds(pl.multiple_of(n * BLOCK, BLOCK), BLOCK)

            def more(st):
                j, alive_n = st
                return jnp.logical_and(j >= 0, alive_n > SB_UNDERFLOW)

            def key_block(st, qi=qi, rows=rows):
                j, _ = st
                keys = pl.ds(pl.multiple_of(j * BLOCK, BLOCK), BLOCK)
                alive_n = jnp.float32(-jnp.inf)
                for g in range(pairs):
                    ls = slice(g * LANES, (g + 1) * LANES)
                    lhs = _stack_heads(q_ref[0, rows, ls])
                    out, run = _sb_block(lhs, k_ref[0, keys, ls], va_scr[keys, ls],
                                         vb_scr[keys, ls], u_ref[...], run_scr[qi, g])
                    acc_scr[qi, g] += out
                    run_scr[qi, g] = run
                    alive_n = jnp.maximum(alive_n, jnp.max(run))
                return j - 1, alive_n

            lax.while_loop(more, key_block, (n - 2, alive[qi]))
            for g in range(pairs):
                ls = slice(g * LANES, (g + 1) * LANES)
                o_ref[0, rows, ls] = acc_scr[qi, g].astype(BF16)
        return carry

    lax.fori_loop(0, s // BLOCK // q_unroll, q_group, 0)


def _sb_attention(proj, q_col, k_col, v_col, pairs, q_unroll):
    b, s, _ = proj.shape
    width = pairs * LANES
    n_steps = B_HEADS // 2 // pairs
    assert q_col % pairs == 0 and k_col % pairs == 0 and v_col % pairs == 0
    assert (s // BLOCK) % q_unroll == 0

    def suffix(n):
        kk = jnp.arange(n)
        return (kk[:, None] >= kk[None, :]).astype(BF16)

    u_band = jnp.tile(suffix(2 * BLOCK), (2, 1))
    u_block = jnp.tile(jnp.concatenate([suffix(BLOCK), jnp.ones((BLOCK, BLOCK), BF16)], axis=1),
                       (2, 1))
    return pl.pallas_call(
        functools.partial(_sb_body, pairs=pairs, q_unroll=q_unroll),
        grid=(b, n_steps),
        in_specs=[
            pl.BlockSpec((1, s, width), lambda i, c: (i, 0, q_col // pairs + c)),
            pl.BlockSpec((1, s, width), lambda i, c: (i, 0, k_col // pairs + c)),
            pl.BlockSpec((1, s, width), lambda i, c: (i, 0, v_col // pairs + c)),
            pl.BlockSpec((4 * BLOCK, 2 * BLOCK), lambda i, c: (0, 0)),
            pl.BlockSpec((2 * BLOCK, 2 * BLOCK), lambda i, c: (0, 0)),
        ],
        out_specs=pl.BlockSpec((1, s, width), lambda i, c: (i, 0, c)),
        out_shape=jax.ShapeDtypeStruct((b, s, B_HEADS * HEAD_DIM), BF16),
        scratch_shapes=[pltpu.VMEM((s, width), BF16), pltpu.VMEM((s, width), BF16),
                        pltpu.VMEM((q_unroll, pairs, BLOCK, LANES), F32),
                        pltpu.VMEM((q_unroll, pairs, 2 * BLOCK, LANES), F32)],
        compiler_params=_params("parallel", "arbitrary"),
        name="stick_breaking",
    )(proj, proj, proj, u_band, u_block)


RESIDUE_STEP = 4


def _to_residue_major(src, dst, seg):
    part = seg // RESIDUE_STEP
    for base in range(0, src.shape[0], seg):
        for r in range(RESIDUE_STEP):
            dst[base + r * part:base + (r + 1) * part, :] = (
                src[pl.ds(base + r, part, stride=RESIDUE_STEP), :])


def _from_residue_major(src, dst, seg):
    part = seg // RESIDUE_STEP
    for base in range(0, src.shape[0], seg):
        for r in range(RESIDUE_STEP):
            dst[pl.ds(base + r, part, stride=RESIDUE_STEP), :] = (
                src[base + r * part:base + (r + 1) * part, :])


def _dil_body(q_ref, k_ref, v_ref, bias_ref, o_ref, tmp_a, tmp_b,
              q4, k4, q16, k16, va1, vb1, va4, vb4, va16, vb16, st_a, st_b):
    s = q_ref.shape[1]
    va1[...], vb1[...] = _split_heads(v_ref[0])
    for src, d4, d16 in ((q_ref, (q4,), (q16,)), (k_ref, (k4,), (k16,)),
                         (v_ref, (va4, vb4), (va16, vb16))):
        tmp_a[...] = src[0].astype(F32)
        _to_residue_major(tmp_a, tmp_b, s)
        _to_residue_major(tmp_b, tmp_a, s // RESIDUE_STEP)
        for dsts, tmp in ((d4, tmp_b), (d16, tmp_a)):
            vals = tmp[...].astype(BF16)
            if len(dsts) == 1:
                dsts[0][...] = vals
            else:
                dsts[0][...], dsts[1][...] = _split_heads(vals)

    layouts = ((q_ref.at[0], k_ref.at[0], va1, vb1), (q4, k4, va4, vb4), (q16, k16, va16, vb16))
    state, spare = st_a, st_b
    for step, branch in enumerate(reversed(range(len(C_PAIRS)))):
        qb, kb, va, vb = layouts[branch]
        class_blocks = s // C_PAIRS[branch][1] // BLOCK
        if step > 0:
            for a in range(3):
                _from_residue_major(state.at[a], spare.at[a],
                                    s // RESIDUE_STEP if step == 1 else s)
            state, spare = spare, state

        def tile(t, carry, step=step, branch=branch, qb=qb, kb=kb, va=va, vb=vb, state=state,
                 class_blocks=class_blocks):
            rows = pl.ds(pl.multiple_of(t * BLOCK, BLOCK), BLOCK)
            lhs = _stack_heads(qb[rows, :])
            first = (t % class_blocks == 0).astype(jnp.int32)
            num, m, den = _band_softmax(lhs, _band_rows(kb, t), _band_rows(va, t),
                                        _band_rows(vb, t), bias_ref[branch, 0, first])
            if step > 0:
                m_old = state[0, rows, :]
                m_new = jnp.maximum(m_old, m)
                a_old = jnp.exp2(m_old - m_new)
                a_cur = jnp.exp2(m - m_new)
                num = a_old * state[2, rows, :] + a_cur * num
                den = a_old * state[1, rows, :] + a_cur * den
                m = m_new
            if branch == 0:
                o_ref[0, rows, :] = (num / den).astype(BF16)
            else:
                state[0, rows, :] = m
                state[1, rows, :] = den
                state[2, rows, :] = num
            return carry

        lax.fori_loop(0, s // BLOCK, tile, 0, unroll=TILE_UNROLL)


def _dilated_attention(proj, slopes):
    b, s, _ = proj.shape
    n_blocks = C_HEADS // 2
    assert C_PAIRS[0][1] == 1 and C_PAIRS[1][1] == RESIDUE_STEP
    assert C_PAIRS[2][1] == RESIDUE_STEP ** 2
    bias = jnp.stack([_band_bias(slopes, w // d, d, strict=False)
                      for (w, d) in C_PAIRS])
    scr = ([pltpu.VMEM((s, LANES), F32)] * 2 + [pltpu.VMEM((s, LANES), BF16)] * 10
           + [pltpu.VMEM((3, s, LANES), F32)] * 2)
    return pl.pallas_call(
        _dil_body,
        grid=(b, n_blocks),
        in_specs=[
            pl.BlockSpec((1, s, LANES), lambda i, c: (i, 0, c)),
            pl.BlockSpec((1, s, LANES), lambda i, c: (i, 0, n_blocks + c)),
            pl.BlockSpec((1, s, LANES), lambda i, c: (i, 0, 2 * n_blocks + c)),
            pl.BlockSpec((len(C_PAIRS), 1, 2, 2 * BLOCK, 2 * BLOCK),
                         lambda i, c: (0, c, 0, 0, 0)),
        ],
        out_specs=pl.BlockSpec((1, s, LANES), lambda i, c: (i, 0, c)),
        out_shape=jax.ShapeDtypeStruct((b, s, n_blocks * LANES), BF16),
        scratch_shapes=scr,
        compiler_params=_params("parallel", "arbitrary"),
        name="dilated_mixture",
    )(proj, proj, proj, bias)


def _alibi_slopes(n):
    return jnp.exp2(-8.0 * jnp.arange(1, n + 1, dtype=F32) / n)


def kernel(x, attn_norm, ffn_norm, even_w_in, even_q_norm, even_k_norm, even_sinks, even_w_out,
           odd_w_in, odd_q_norm, odd_k_norm, odd_w_out, ffn_w_gate, ffn_w_up, ffn_w_down):
    b, s, d = x.shape
    depth = attn_norm.shape[0]
    scale = HEAD_DIM ** -0.5
    scale2 = scale * LOG2E
    slopes_a = _alibi_slopes(A_Q_HEADS)
    slopes_c = _alibi_slopes(C_HEADS)
    qa, kva, hb = A_Q_HEADS * HEAD_DIM, A_KV_HEADS * HEAD_DIM, B_HEADS * HEAD_DIM
    hc = C_HEADS * HEAD_DIM
    ones = lambda n: jnp.ones((n,), F32)
    zeros = lambda n: jnp.zeros((n,), F32)

    x2 = x.reshape(b * s, d)
    for i in range(depth):
        j = i // 2
        if i % 2 == 0:
            colgain = jnp.concatenate([
                jnp.tile(even_q_norm[j].astype(F32), A_Q_HEADS) * scale2,
                jnp.tile(even_k_norm[j].astype(F32), A_KV_HEADS),
                ones(kva), ones(hb) * scale2, ones(hb), ones(hb)])
            colflag = jnp.concatenate([ones(qa + kva), zeros(kva + 3 * hb)])
            proj = _norm_proj(x2, attn_norm[i], even_w_in[j].astype(BF16), colgain, colflag,
                              qa + kva, tm=1024, tn=256).reshape(b, s, -1)
            oa = _swa_attention(proj, even_sinks[j], slopes_a)
            qb_col = (qa + 2 * kva) // LANES
            ob = _sb_attention(proj, qb_col, qb_col + hb // LANES, qb_col + 2 * hb // LANES,
                               pairs=2, q_unroll=4)
            w_out = even_w_out[j].astype(BF16)
            pairs = [(oa.reshape(b * s, qa), w_out[:qa]), (ob.reshape(b * s, hb), w_out[qa:])]
        else:
            colgain = jnp.concatenate([
                jnp.tile(odd_q_norm[j].astype(F32), C_HEADS) * scale2,
                jnp.tile(odd_k_norm[j].astype(F32), C_HEADS), ones(hc)])
            colflag = jnp.concatenate([ones(2 * hc), zeros(hc)])
            proj = _norm_proj(x2, attn_norm[i], odd_w_in[j].astype(BF16), colgain, colflag,
                              2 * hc, tm=1024, tn=512).reshape(b, s, -1)
            oc = _dilated_attention(proj, slopes_c)
            pairs = [(oc.reshape(b * s, hc), odd_w_out[j].astype(BF16))]
        x2 = _out_proj(x2, pairs, tm=512, tn=1024)
        x2 = _ffn(x2, ffn_norm[i], ffn_w_gate[i].astype(BF16), ffn_w_up[i].astype(BF16),
                  ffn_w_down[i].astype(BF16), tm=512, tf=512)
    return x2.reshape(b, s, d)
```

```python
import functools

import jax
import jax.numpy as jnp
from jax import lax
from jax.experimental import pallas as pl
from jax.experimental.pallas import tpu as pltpu

HEAD_DIM = 64
LANES = 128
BLOCK = 128
A_Q_HEADS = 16
A_KV_HEADS = 2
B_HEADS = 16
C_HEADS = 32
C_PAIRS = ((128, 1), (512, 4), (2048, 16))
RMS_EPS = 1e-6
MASKED = -1e30
LOG2E = 1.4426950408889634
SB_UNDERFLOW = -127.0
VMEM_LIMIT = 56 * 1024 * 1024
TILE_UNROLL = 16

F32 = jnp.float32
BF16 = jnp.bfloat16


def _dot(a, b):
    return jnp.dot(a, b, preferred_element_type=F32)


def _dot_t(a, b):
    return lax.dot_general(a, b, (((1,), (1,)), ((), ())), preferred_element_type=F32)


def _params(*sem):
    return pltpu.CompilerParams(dimension_semantics=sem, vmem_limit_bytes=VMEM_LIMIT)


MXU_WIDTH = 256


def _proj_body(x_ref, g_ref, w_ref, cg_ref, cf_ref, s_ref, o_ref, *, n_norm_cols, chunk):
    x = x_ref[...]
    inv = lax.rsqrt(jnp.mean(x * x, axis=-1, keepdims=True) + RMS_EPS)
    h = ((x * inv) * g_ref[...]).astype(BF16)
    n_out = o_ref.shape[1]
    for c0 in range(0, n_out, chunk):
        width = min(chunk, n_out - c0)
        y = _dot(h, w_ref[:, c0:c0 + width])
        for c in range(c0, c0 + width, MXU_WIDTH):
            sl = slice(c, c + MXU_WIDTH)
            yc = y[:, c - c0:c - c0 + MXU_WIDTH]
            cg = cg_ref[:, sl]
            if c < n_norm_cols:
                ss = _dot((yc * yc).astype(BF16), s_ref[...])
                inv_h = lax.rsqrt(ss * (1.0 / HEAD_DIM) + RMS_EPS)
                scale = jnp.where(cf_ref[:, sl] > 0.0, inv_h * cg, cg)
            else:
                scale = cg
            o_ref[:, sl] = (yc * scale).astype(BF16)


def _norm_proj(x2, gain, w, colgain, colflag, n_norm_cols, tm, chunk):
    n, d = x2.shape
    n_out = w.shape[1]
    assert n % tm == 0 and n_out % MXU_WIDTH == 0 and chunk % MXU_WIDTH == 0
    idx = jnp.arange(MXU_WIDTH) // HEAD_DIM
    seg = (idx[:, None] == idx[None, :]).astype(BF16)
    const = lambda shape: pl.BlockSpec(shape, lambda i: (0, 0), pipeline_mode=pl.Buffered(1))
    return pl.pallas_call(
        functools.partial(_proj_body, n_norm_cols=n_norm_cols, chunk=chunk),
        grid=(n // tm,),
        in_specs=[
            pl.BlockSpec((tm, d), lambda i: (i, 0)),
            const((1, d)),
            const((d, n_out)),
            const((1, n_out)),
            const((1, n_out)),
            const((MXU_WIDTH, MXU_WIDTH)),
        ],
        out_specs=pl.BlockSpec((tm, n_out), lambda i: (i, 0)),
        out_shape=jax.ShapeDtypeStruct((n, n_out), BF16),
        compiler_params=_params("parallel"),
        name="norm_proj",
    )(x2, gain.reshape(1, d), w, colgain.reshape(1, n_out), colflag.reshape(1, n_out), seg)


def _out_body(*refs, n_pairs):
    x_ref = refs[0]
    o_ref = refs[1 + 2 * n_pairs]
    acc = x_ref[...]
    for p in range(n_pairs):
        acc = acc + _dot(refs[1 + 2 * p][...], refs[2 + 2 * p][...])
    o_ref[...] = acc


def _out_proj(x2, acts, w, tm, tn):
    n, d = x2.shape
    in_specs = [pl.BlockSpec((tm, tn), lambda i, j: (i, j))]
    args = [x2]
    offset = 0
    for a in acts:
        kk = a.shape[1]
        assert offset % kk == 0
        in_specs.append(pl.BlockSpec((tm, kk), lambda i, j: (i, 0)))
        in_specs.append(pl.BlockSpec((kk, tn), lambda i, j, r=offset // kk: (r, j)))
        args += [a, w]
        offset += kk
    assert offset == w.shape[0]
    return pl.pallas_call(
        functools.partial(_out_body, n_pairs=len(acts)),
        grid=(n // tm, d // tn),
        in_specs=in_specs,
        out_specs=pl.BlockSpec((tm, tn), lambda i, j: (i, j)),
        out_shape=jax.ShapeDtypeStruct((n, d), F32),
        compiler_params=_params("parallel", "arbitrary"),
        name="out_proj",
    )(*args)


def _ffn_body(x_ref, g_ref, wg_ref, wu_ref, wd_ref, o_ref, h_scr):
    j = pl.program_id(1)

    @pl.when(j == 0)
    def _():
        x = x_ref[...]
        inv = lax.rsqrt(jnp.mean(x * x, axis=-1, keepdims=True) + RMS_EPS)
        h_scr[...] = ((x * inv) * g_ref[...]).astype(BF16)
        o_ref[...] = x

    h = h_scr[...]
    gate = _dot(h, wg_ref[...])
    up = _dot(h, wu_ref[...])
    act = (gate * (1.0 / (1.0 + jnp.exp(-gate)))) * up
    o_ref[...] += _dot(act.astype(BF16), wd_ref[...])


def _ffn(x2, gain, wg, wu, wd, tm, tf):
    n, d = x2.shape
    dff = wg.shape[1]
    assert n % tm == 0 and dff % tf == 0
    return pl.pallas_call(
        _ffn_body,
        grid=(n // tm, dff // tf),
        in_specs=[
            pl.BlockSpec((tm, d), lambda i, j: (i, 0)),
            pl.BlockSpec((1, d), lambda i, j: (0, 0)),
            pl.BlockSpec((d, tf), lambda i, j: (0, j)),
            pl.BlockSpec((d, tf), lambda i, j: (0, j)),
            pl.BlockSpec((tf, d), lambda i, j: (j, 0)),
        ],
        out_specs=pl.BlockSpec((tm, d), lambda i, j: (i, 0)),
        out_shape=jax.ShapeDtypeStruct((n, d), F32),
        scratch_shapes=[pltpu.VMEM((tm, d), BF16)],
        compiler_params=_params("parallel", "arbitrary"),
        name="ffn",
    )(x2, gain.reshape(1, d), wg, wu, wd)


def _stack_heads(q):
    lo = lax.broadcasted_iota(jnp.int32, q.shape, 1) < HEAD_DIM
    zero = jnp.zeros_like(q)
    return jnp.concatenate([jnp.where(lo, q, zero), jnp.where(lo, zero, q)], axis=0)


def _unstack_heads(t):
    lo = lax.broadcasted_iota(jnp.int32, (BLOCK, LANES), 1) < HEAD_DIM
    return jnp.where(lo, t[:BLOCK], t[BLOCK:])


def _split_heads(v):
    lo = lax.broadcasted_iota(jnp.int32, v.shape, 1) < HEAD_DIM
    zero = jnp.zeros_like(v)
    return jnp.where(lo, v, zero), jnp.where(lo, zero, v)


def _band_softmax(lhs, k_band, va_band, vb_band, bias, sink=None):
    s = _dot_t(lhs, k_band) + bias
    m = jnp.broadcast_to(jnp.max(s, axis=1, keepdims=True), (2 * BLOCK, LANES))
    if sink is not None:
        m = jnp.maximum(m, sink)
    p = jnp.exp2(s - jnp.concatenate([m, m], axis=1)).astype(BF16)
    p_cat = jnp.concatenate([p[:BLOCK], p[BLOCK:]], axis=1)
    lo = lax.broadcasted_iota(jnp.int32, (2 * BLOCK, LANES), 1) < HEAD_DIM
    ones_a = jnp.where(lo, 1.0, 0.0).astype(BF16)
    ones_b = jnp.where(lo, 0.0, 1.0).astype(BF16)
    rhs = jnp.concatenate([jnp.concatenate([va_band, vb_band], axis=0),
                           jnp.concatenate([ones_a, ones_b], axis=0)], axis=1)
    ext = _dot(p_cat, rhs)
    return ext[:, :LANES], _unstack_heads(m), ext[:, LANES:]


def _band_bias(slopes, window_max, dist_scale, strict):
    i = jnp.arange(BLOCK)[:, None]
    j = jnp.arange(2 * BLOCK)[None, :]
    dist = i + BLOCK - j
    valid = (dist >= 0) & ((dist < window_max) if strict else (dist <= window_max))
    alibi = (slopes[:, None, None] * (dist * dist_scale).astype(F32)[None]) * LOG2E
    h = slopes.shape[0]
    table = jnp.stack([jnp.where(valid[None], -alibi, MASKED),
                       jnp.where((valid & (j >= BLOCK))[None], -alibi, MASKED)], axis=1)
    table = table.reshape(h // 2, 2, 2, BLOCK, 2 * BLOCK).transpose(0, 2, 1, 3, 4)
    return table.reshape(h // 2, 2, 2 * BLOCK, 2 * BLOCK)


def _pair_rows(t):
    h = t.shape[0]
    return t.reshape((h // 2, 2 * t.shape[1]) + t.shape[2:])


def _band_rows(ref, n, lanes=slice(None)):
    r0 = pl.multiple_of(n * BLOCK, BLOCK)
    p0 = pl.multiple_of(jnp.maximum(n - 1, 0) * BLOCK, BLOCK)
    return jnp.concatenate([ref[pl.ds(p0, BLOCK), lanes], ref[pl.ds(r0, BLOCK), lanes]], axis=0)


def _swa_body(q_ref, k_ref, v_ref, bias_ref, sink_ref, o_ref, kd_scr, va_scr, vb_scr):
    s = q_ref.shape[1]
    c = pl.program_id(1)
    kv_head = c // (A_Q_HEADS // A_KV_HEADS // 2)
    lane_head = lax.broadcasted_iota(jnp.int32, (s, LANES), 1) // HEAD_DIM
    sel = lane_head == kv_head
    kf = jnp.where(sel, k_ref[0].astype(F32), 0.0)
    kd_scr[...] = (kf + pltpu.roll(kf, HEAD_DIM, axis=1)).astype(BF16)
    vf = jnp.where(sel, v_ref[0].astype(F32), 0.0)
    vr = pltpu.roll(vf, HEAD_DIM, axis=1)
    va_scr[...] = jnp.where(kv_head == 0, vf, vr).astype(BF16)
    vb_scr[...] = jnp.where(kv_head == 0, vr, vf).astype(BF16)

    def block(n, carry):
        rows = pl.ds(pl.multiple_of(n * BLOCK, BLOCK), BLOCK)
        lhs = _stack_heads(q_ref[0, rows, :])
        first = (n == 0).astype(jnp.int32)
        num, m, den = _band_softmax(lhs, _band_rows(kd_scr, n), _band_rows(va_scr, n),
                                    _band_rows(vb_scr, n), bias_ref[0, first],
                                    sink=sink_ref[0, :2 * BLOCK])
        den = den + jnp.exp2(sink_ref[0, 2 * BLOCK:] - m)
        o_ref[0, rows, :] = (num / den).astype(BF16)
        return carry

    lax.fori_loop(0, s // BLOCK, block, 0, unroll=TILE_UNROLL)


def _swa_attention(proj, sinks, slopes):
    b, s, _ = proj.shape
    n_blocks = A_Q_HEADS // 2
    k_col = A_Q_HEADS * HEAD_DIM // LANES
    bias = _band_bias(slopes, BLOCK, 1, strict=True)
    sink2 = sinks.astype(F32) * LOG2E
    stacked = _pair_rows(jnp.broadcast_to(sink2[:, None, None], (A_Q_HEADS, BLOCK, LANES)))
    by_lane = jnp.broadcast_to(jnp.repeat(sink2, HEAD_DIM).reshape(n_blocks, 1, LANES),
                               (n_blocks, BLOCK, LANES))
    sink = jnp.concatenate([stacked, by_lane], axis=1)
    return pl.pallas_call(
        _swa_body,
        grid=(b, n_blocks),
        in_specs=[
            pl.BlockSpec((1, s, LANES), lambda i, c: (i, 0, c)),
            pl.BlockSpec((1, s, LANES), lambda i, c: (i, 0, k_col)),
            pl.BlockSpec((1, s, LANES), lambda i, c: (i, 0, k_col + 1)),
            pl.BlockSpec((1, 2, 2 * BLOCK, 2 * BLOCK), lambda i, c: (c, 0, 0, 0)),
            pl.BlockSpec((1, 3 * BLOCK, LANES), lambda i, c: (c, 0, 0)),
        ],
        out_specs=pl.BlockSpec((1, s, LANES), lambda i, c: (i, 0, c)),
        out_shape=jax.ShapeDtypeStruct((b, s, n_blocks * LANES), BF16),
        scratch_shapes=[pltpu.VMEM((s, LANES), BF16)] * 3,
        compiler_params=_params("parallel", "arbitrary"),
        name="swa_attention",
    )(proj, proj, proj, bias, sink)


def _sb_log_keep(z):
    return -(jnp.maximum(z, 0.0) + jnp.log2(1.0 + jnp.exp2(-jnp.abs(z))))


def _hi_lo(x):
    hi = x.astype(BF16)
    lo = (x - hi.astype(F32)).astype(BF16)
    return jnp.concatenate([hi, lo], axis=1)


def _sb_weighted_values(w, va, vb):
    w = w.astype(BF16)
    return _dot(jnp.concatenate([w[:BLOCK], w[BLOCK:]], axis=1),
                jnp.concatenate([va, vb], axis=0))


def _sb_band(lhs, k_band, va_band, vb_band, u_band, valid):
    z = _dot_t(lhs, k_band)
    log_keep = jnp.where(valid, _sb_log_keep(z), 0.0)
    cs = _dot(_hi_lo(log_keep), u_band)
    w = jnp.where(valid, jnp.exp2(z + cs), 0.0)
    total = jnp.broadcast_to(cs[:, :1], (2 * BLOCK, LANES))
    return _sb_weighted_values(w, va_band, vb_band), total


def _sb_block(lhs, k, va, vb, u_block, run):
    z = _dot_t(lhs, k)
    cs = _dot(_hi_lo(_sb_log_keep(z)), u_block)
    w = jnp.exp2(z + cs[:, :BLOCK] + run)
    return _sb_weighted_values(w, va, vb), run + cs[:, BLOCK:]


def _sb_body(q_ref, k_ref, v_ref, ub_ref, u_ref, o_ref, va_scr, vb_scr, acc_scr, run_scr,
             *, pairs, q_unroll):
    s = q_ref.shape[1]
    lo = lax.broadcasted_iota(jnp.int32, (s, pairs * LANES), 1) % LANES < HEAD_DIM
    v_all = v_ref[0]
    va_scr[...] = jnp.where(lo, v_all, jnp.zeros_like(v_all))
    vb_scr[...] = jnp.where(lo, jnp.zeros_like(v_all), v_all)
    row = lax.broadcasted_iota(jnp.int32, (2 * BLOCK, 2 * BLOCK), 0) % BLOCK + BLOCK
    col = lax.broadcasted_iota(jnp.int32, (2 * BLOCK, 2 * BLOCK), 1)
    k2 = k_ref.at[0]

    def q_group(i, carry):
        blocks = [i * q_unroll + qi for qi in range(q_unroll)]
        alive = []
        for qi, n in enumerate(blocks):
            rows = pl.ds(pl.multiple_of(n * BLOCK, BLOCK), BLOCK)
            first_key = jnp.where(n > 0, 0, BLOCK)
            valid = jnp.logical_and(col < row, col >= first_key)
            alive_n = jnp.float32(-jnp.inf)
            for g in range(pairs):
                ls = slice(g * LANES, (g + 1) * LANES)
                lhs = _stack_heads(q_ref[0, rows, ls])
                out, total = _sb_band(lhs, _band_rows(k2, n, ls), _band_rows(va_scr, n, ls),
                                      _band_rows(vb_scr, n, ls), ub_ref[...], valid)
                acc_scr[qi, g] = out
                run_scr[qi, g] = total
                alive_n = jnp.maximum(alive_n, jnp.max(total))
            alive.append(alive_n)

        for qi, n in enumerate(blocks):
            rows = pl.ds(pl.multiple_of(n * BLOCK, BLOCK), BLOCK)

            def more(st):
                j, alive_n = st
                return jnp.logical_and(j >= 0, alive_n > SB_UNDERFLOW)

            def key_block(st, qi=qi, rows=rows):
                j, _ = st
                keys = pl.ds(pl.multiple_of(j * BLOCK, BLOCK), BLOCK)
                alive_n = jnp.float32(-jnp.inf)
                for g in range(pairs):
                    ls = slice(g * LANES, (g + 1) * LANES)
                    lhs = _stack_heads(q_ref[0, rows, ls])
                    out, run = _sb_block(lhs, k_ref[0, keys, ls], va_scr[keys, ls],
                                         vb_scr[keys, ls], u_ref[...], run_scr[qi, g])
                    acc_scr[qi, g] += out
                    run_scr[qi, g] = run
                    alive_n = jnp.maximum(alive_n, jnp.max(run))
                return j - 1, alive_n

            lax.while_loop(more, key_block, (n - 2, alive[qi]))
            for g in range(pairs):
                ls = slice(g * LANES, (g + 1) * LANES)
                o_ref[0, rows, ls] = acc_scr[qi, g].astype(BF16)
        return carry

    lax.fori_loop(0, s // BLOCK // q_unroll, q_group, 0)


def _sb_attention(proj, q_col, k_col, v_col, pairs, q_unroll):
    b, s, _ = proj.shape
    width = pairs * LANES
    n_steps = B_HEADS // 2 // pairs
    assert q_col % pairs == 0 and k_col % pairs == 0 and v_col % pairs == 0
    assert (s // BLOCK) % q_unroll == 0

    def suffix(n):
        kk = jnp.arange(n)
        return (kk[:, None] >= kk[None, :]).astype(BF16)

    u_band = jnp.tile(suffix(2 * BLOCK), (2, 1))
    u_block = jnp.tile(jnp.concatenate([suffix(BLOCK), jnp.ones((BLOCK, BLOCK), BF16)], axis=1),
                       (2, 1))
    return pl.pallas_call(
        functools.partial(_sb_body, pairs=pairs, q_unroll=q_unroll),
        grid=(b, n_steps),
        in_specs=[
            pl.BlockSpec((1, s, width), lambda i, c: (i, 0, q_col // pairs + c)),
            pl.BlockSpec((1, s, width), lambda i, c: (i, 0, k_col // pairs + c)),
            pl.BlockSpec((1, s, width), lambda i, c: (i, 0, v_col // pairs + c)),
            pl.BlockSpec((4 * BLOCK, 2 * BLOCK), lambda i, c: (0, 0)),
            pl.BlockSpec((2 * BLOCK, 2 * BLOCK), lambda i, c: (0, 0)),
        ],
        out_specs=pl.BlockSpec((1, s, width), lambda i, c: (i, 0, c)),
        out_shape=jax.ShapeDtypeStruct((b, s, B_HEADS * HEAD_DIM), BF16),
        scratch_shapes=[pltpu.VMEM((s, width), BF16), pltpu.VMEM((s, width), BF16),
                        pltpu.VMEM((q_unroll, pairs, BLOCK, LANES), F32),
                        pltpu.VMEM((q_unroll, pairs, 2 * BLOCK, LANES), F32)],
        compiler_params=_params("parallel", "arbitrary"),
        name="stick_breaking",
    )(proj, proj, proj, u_band, u_block)


RESIDUE_STEP = 4


def _to_residue_major(src, dst, seg):
    part = seg // RESIDUE_STEP
    for base in range(0, src.shape[0], seg):
        for r in range(RESIDUE_STEP):
            dst[base + r * part:base + (r + 1) * part, :] = (
                src[pl.ds(base + r, part, stride=RESIDUE_STEP), :])


def _from_residue_major(src, dst, seg):
    part = seg // RESIDUE_STEP
    for base in range(0, src.shape[0], seg):
        for r in range(RESIDUE_STEP):
            dst[pl.ds(base + r, part, stride=RESIDUE_STEP), :] = (
                src[base + r * part:base + (r + 1) * part, :])


def _dil_body(q_ref, k_ref, v_ref, bias_ref, o_ref, tmp_a, tmp_b,
              q4, k4, q16, k16, va1, vb1, va4, vb4, va16, vb16, st_a, st_b):
    s = q_ref.shape[1]
    va1[...], vb1[...] = _split_heads(v_ref[0])
    for src, d4, d16 in ((q_ref, (q4,), (q16,)), (k_ref, (k4,), (k16,)),
                         (v_ref, (va4, vb4), (va16, vb16))):
        tmp_a[...] = src[0].astype(F32)
        _to_residue_major(tmp_a, tmp_b, s)
        _to_residue_major(tmp_b, tmp_a, s // RESIDUE_STEP)
        for dsts, tmp in ((d4, tmp_b), (d16, tmp_a)):
            vals = tmp[...].astype(BF16)
            if len(dsts) == 1:
                dsts[0][...] = vals
            else:
                dsts[0][...], dsts[1][...] = _split_heads(vals)

    layouts = ((q_ref.at[0], k_ref.at[0], va1, vb1), (q4, k4, va4, vb4), (q16, k16, va16, vb16))
    state, spare = st_a, st_b
    for step, branch in enumerate(reversed(range(len(C_PAIRS)))):
        qb, kb, va, vb = layouts[branch]
        class_blocks = s // C_PAIRS[branch][1] // BLOCK
        if step > 0:
            for a in range(3):
                _from_residue_major(state.at[a], spare.at[a],
                                    s // RESIDUE_STEP if step == 1 else s)
            state, spare = spare, state

        def tile(t, carry, step=step, branch=branch, qb=qb, kb=kb, va=va, vb=vb, state=state,
                 class_blocks=class_blocks):
            rows = pl.ds(pl.multiple_of(t * BLOCK, BLOCK), BLOCK)
            lhs = _stack_heads(qb[rows, :])
            first = (t % class_blocks == 0).astype(jnp.int32)
            num, m, den = _band_softmax(lhs, _band_rows(kb, t), _band_rows(va, t),
                                        _band_rows(vb, t), bias_ref[branch, 0, first])
            if step > 0:
                m_old = state[0, rows, :]
                m_new = jnp.maximum(m_old, m)
                a_old = jnp.exp2(m_old - m_new)
                a_cur = jnp.exp2(m - m_new)
                num = a_old * state[2, rows, :] + a_cur * num
                den = a_old * state[1, rows, :] + a_cur * den
                m = m_new
            if branch == 0:
                o_ref[0, rows, :] = (num / den).astype(BF16)
            else:
                state[0, rows, :] = m
                state[1, rows, :] = den
                state[2, rows, :] = num
            return carry

        lax.fori_loop(0, s // BLOCK, tile, 0, unroll=TILE_UNROLL)


def _dilated_attention(proj, slopes):
    b, s, _ = proj.shape
    n_blocks = C_HEADS // 2
    assert C_PAIRS[0][1] == 1 and C_PAIRS[1][1] == RESIDUE_STEP
    assert C_PAIRS[2][1] == RESIDUE_STEP ** 2
    bias = jnp.stack([_band_bias(slopes, w // d, d, strict=False)
                      for (w, d) in C_PAIRS])
    scr = ([pltpu.VMEM((s, LANES), F32)] * 2 + [pltpu.VMEM((s, LANES), BF16)] * 10
           + [pltpu.VMEM((3, s, LANES), F32)] * 2)
    return pl.pallas_call(
        _dil_body,
        grid=(b, n_blocks),
        in_specs=[
            pl.BlockSpec((1, s, LANES), lambda i, c: (i, 0, c)),
            pl.BlockSpec((1, s, LANES), lambda i, c: (i, 0, n_blocks + c)),
            pl.BlockSpec((1, s, LANES), lambda i, c: (i, 0, 2 * n_blocks + c)),
            pl.BlockSpec((len(C_PAIRS), 1, 2, 2 * BLOCK, 2 * BLOCK),
                         lambda i, c: (0, c, 0, 0, 0)),
        ],
        out_specs=pl.BlockSpec((1, s, LANES), lambda i, c: (i, 0, c)),
        out_shape=jax.ShapeDtypeStruct((b, s, n_blocks * LANES), BF16),
        scratch_shapes=scr,
        compiler_params=_params("parallel", "arbitrary"),
        name="dilated_mixture",
    )(proj, proj, proj, bias)


def _alibi_slopes(n):
    return jnp.exp2(-8.0 * jnp.arange(1, n + 1, dtype=F32) / n)


def kernel(x, attn_norm, ffn_norm, even_w_in, even_q_norm, even_k_norm, even_sinks, even_w_out,
           odd_w_in, odd_q_norm, odd_k_norm, odd_w_out, ffn_w_gate, ffn_w_up, ffn_w_down):
    b, s, d = x.shape
    depth = attn_norm.shape[0]
    scale = HEAD_DIM ** -0.5
    scale2 = scale * LOG2E
    slopes_a = _alibi_slopes(A_Q_HEADS)
    slopes_c = _alibi_slopes(C_HEADS)
    qa, kva, hb = A_Q_HEADS * HEAD_DIM, A_KV_HEADS * HEAD_DIM, B_HEADS * HEAD_DIM
    hc = C_HEADS * HEAD_DIM
    ones = lambda n: jnp.ones((n,), F32)
    zeros = lambda n: jnp.zeros((n,), F32)

    x2 = x.reshape(b * s, d)
    for i in range(depth):
        j = i // 2
        if i % 2 == 0:
            colgain = jnp.concatenate([
                jnp.tile(even_q_norm[j].astype(F32), A_Q_HEADS) * scale2,
                jnp.tile(even_k_norm[j].astype(F32), A_KV_HEADS),
                ones(kva), ones(hb) * scale2, ones(hb), ones(hb)])
            colflag = jnp.concatenate([ones(qa + kva), zeros(kva + 3 * hb)])
            proj = _norm_proj(x2, attn_norm[i], even_w_in[j].astype(BF16), colgain, colflag,
                              qa + kva, tm=512, chunk=512).reshape(b, s, -1)
            oa = _swa_attention(proj, even_sinks[j], slopes_a)
            qb_col = (qa + 2 * kva) // LANES
            ob = _sb_attention(proj, qb_col, qb_col + hb // LANES, qb_col + 2 * hb // LANES,
                               pairs=2, q_unroll=4)
            w_out = even_w_out[j].astype(BF16)
            acts = [oa.reshape(b * s, qa), ob.reshape(b * s, hb)]
        else:
            colgain = jnp.concatenate([
                jnp.tile(odd_q_norm[j].astype(F32), C_HEADS) * scale2,
                jnp.tile(odd_k_norm[j].astype(F32), C_HEADS), ones(hc)])
            colflag = jnp.concatenate([ones(2 * hc), zeros(hc)])
            proj = _norm_proj(x2, attn_norm[i], odd_w_in[j].astype(BF16), colgain, colflag,
                              2 * hc, tm=512, chunk=512).reshape(b, s, -1)
            oc = _dilated_attention(proj, slopes_c)
            w_out = odd_w_out[j].astype(BF16)
            acts = [oc.reshape(b * s, hc)]
        x2 = _out_proj(x2, acts, w_out, tm=512, tn=2048)
        x2 = _ffn(x2, ffn_norm[i], ffn_w_gate[i].astype(BF16), ffn_w_up[i].astype(BF16),
                  ffn_w_down[i].astype(BF16), tm=1024, tf=512)
    return x2.reshape(b, s, d)
```

```python
import functools

import jax
import jax.numpy as jnp
from jax import lax
from jax.experimental import pallas as pl
from jax.experimental.pallas import tpu as pltpu

HEAD_DIM = 64
LANES = 128
BLOCK = 128
A_Q_HEADS = 16
A_KV_HEADS = 2
B_HEADS = 16
C_HEADS = 32
C_PAIRS = ((128, 1), (512, 4), (2048, 16))
RMS_EPS = 1e-6
MASKED = -1e30
LOG2E = 1.4426950408889634
SB_DECAY_DONE = 127.0
VMEM_LIMIT = 56 * 1024 * 1024
TILE_UNROLL = 16

F32 = jnp.float32
BF16 = jnp.bfloat16


def _dot(a, b):
    return jnp.dot(a, b, preferred_element_type=F32)


def _dot_t(a, b):
    return lax.dot_general(a, b, (((1,), (1,)), ((), ())), preferred_element_type=F32)


def _params(*sem):
    return pltpu.CompilerParams(dimension_semantics=sem, vmem_limit_bytes=VMEM_LIMIT)


MXU_WIDTH = 256


def _proj_body(x_ref, g_ref, w_ref, cg_ref, cf_ref, s_ref, o_ref, *, n_norm_cols, chunk):
    x = x_ref[...]
    inv = lax.rsqrt(jnp.mean(x * x, axis=-1, keepdims=True) + RMS_EPS)
    h = ((x * inv) * g_ref[...]).astype(BF16)
    n_out = o_ref.shape[1]
    for c0 in range(0, n_out, chunk):
        width = min(chunk, n_out - c0)
        y = _dot(h, w_ref[:, c0:c0 + width])
        for c in range(c0, c0 + width, MXU_WIDTH):
            sl = slice(c, c + MXU_WIDTH)
            yc = y[:, c - c0:c - c0 + MXU_WIDTH]
            cg = cg_ref[:, sl]
            if c < n_norm_cols:
                ss = _dot((yc * yc).astype(BF16), s_ref[...])
                inv_h = lax.rsqrt(ss * (1.0 / HEAD_DIM) + RMS_EPS)
                scale = jnp.where(cf_ref[:, sl] > 0.0, inv_h * cg, cg)
            else:
                scale = cg
            o_ref[:, sl] = (yc * scale).astype(BF16)


def _norm_proj(x2, gain, w, colgain, colflag, n_norm_cols, tm, chunk):
    n, d = x2.shape
    n_out = w.shape[1]
    assert n % tm == 0 and n_out % MXU_WIDTH == 0 and chunk % MXU_WIDTH == 0
    idx = jnp.arange(MXU_WIDTH) // HEAD_DIM
    seg = (idx[:, None] == idx[None, :]).astype(BF16)
    const = lambda shape: pl.BlockSpec(shape, lambda i: (0, 0), pipeline_mode=pl.Buffered(1))
    return pl.pallas_call(
        functools.partial(_proj_body, n_norm_cols=n_norm_cols, chunk=chunk),
        grid=(n // tm,),
        in_specs=[
            pl.BlockSpec((tm, d), lambda i: (i, 0)),
            const((1, d)),
            const((d, n_out)),
            const((1, n_out)),
            const((1, n_out)),
            const((MXU_WIDTH, MXU_WIDTH)),
        ],
        out_specs=pl.BlockSpec((tm, n_out), lambda i: (i, 0)),
        out_shape=jax.ShapeDtypeStruct((n, n_out), BF16),
        compiler_params=_params("parallel"),
        name="norm_proj",
    )(x2, gain.reshape(1, d), w, colgain.reshape(1, n_out), colflag.reshape(1, n_out), seg)


def _out_body(*refs, n_pairs):
    x_ref = refs[0]
    o_ref = refs[1 + 2 * n_pairs]
    acc = x_ref[...]
    for p in range(n_pairs):
        acc = acc + _dot(refs[1 + 2 * p][...], refs[2 + 2 * p][...])
    o_ref[...] = acc


def _out_proj(x2, acts, w, tm, tn):
    n, d = x2.shape
    in_specs = [pl.BlockSpec((tm, tn), lambda i, j: (i, j))]
    args = [x2]
    offset = 0
    for a in acts:
        kk = a.shape[1]
        assert offset % kk == 0
        in_specs.append(pl.BlockSpec((tm, kk), lambda i, j: (i, 0)))
        in_specs.append(pl.BlockSpec((kk, tn), lambda i, j, r=offset // kk: (r, j)))
        args += [a, w]
        offset += kk
    assert offset == w.shape[0]
    return pl.pallas_call(
        functools.partial(_out_body, n_pairs=len(acts)),
        grid=(n // tm, d // tn),
        in_specs=in_specs,
        out_specs=pl.BlockSpec((tm, tn), lambda i, j: (i, j)),
        out_shape=jax.ShapeDtypeStruct((n, d), F32),
        compiler_params=_params("parallel", "arbitrary"),
        name="out_proj",
    )(*args)


def _ffn_body(x_ref, g_ref, wg_ref, wu_ref, wd_ref, o_ref, h_scr):
    j = pl.program_id(1)

    @pl.when(j == 0)
    def _():
        x = x_ref[...]
        inv = lax.rsqrt(jnp.mean(x * x, axis=-1, keepdims=True) + RMS_EPS)
        h_scr[...] = ((x * inv) * g_ref[...]).astype(BF16)
        o_ref[...] = x

    h = h_scr[...]
    gate = _dot(h, wg_ref[...])
    up = _dot(h, wu_ref[...])
    act = (gate * (1.0 / (1.0 + jnp.exp(-gate)))) * up
    o_ref[...] += _dot(act.astype(BF16), wd_ref[...])


def _ffn(x2, gain, wg, wu, wd, tm, tf):
    n, d = x2.shape
    dff = wg.shape[1]
    assert n % tm == 0 and dff % tf == 0
    return pl.pallas_call(
        _ffn_body,
        grid=(n // tm, dff // tf),
        in_specs=[
            pl.BlockSpec((tm, d), lambda i, j: (i, 0)),
            pl.BlockSpec((1, d), lambda i, j: (0, 0)),
            pl.BlockSpec((d, tf), lambda i, j: (0, j)),
            pl.BlockSpec((d, tf), lambda i, j: (0, j)),
            pl.BlockSpec((tf, d), lambda i, j: (j, 0)),
        ],
        out_specs=pl.BlockSpec((tm, d), lambda i, j: (i, 0)),
        out_shape=jax.ShapeDtypeStruct((n, d), F32),
        scratch_shapes=[pltpu.VMEM((tm, d), BF16)],
        compiler_params=_params("parallel", "arbitrary"),
        name="ffn",
    )(x2, gain.reshape(1, d), wg, wu, wd)


def _stack_heads(q):
    lo = lax.broadcasted_iota(jnp.int32, q.shape, 1) < HEAD_DIM
    zero = jnp.zeros_like(q)
    return jnp.concatenate([jnp.where(lo, q, zero), jnp.where(lo, zero, q)], axis=0)


def _unstack_heads(t):
    lo = lax.broadcasted_iota(jnp.int32, (BLOCK, LANES), 1) < HEAD_DIM
    return jnp.where(lo, t[:BLOCK], t[BLOCK:])


def _split_heads(v):
    lo = lax.broadcasted_iota(jnp.int32, v.shape, 1) < HEAD_DIM
    zero = jnp.zeros_like(v)
    return jnp.where(lo, v, zero), jnp.where(lo, zero, v)


def _band_rows(ref, n, lanes=slice(None)):
    r0 = pl.multiple_of(n * BLOCK, BLOCK)
    p0 = pl.multiple_of(jnp.maximum(n - 1, 0) * BLOCK, BLOCK)
    return jnp.concatenate([ref[pl.ds(p0, BLOCK), lanes], ref[pl.ds(r0, BLOCK), lanes]], axis=0)


def _band_softmax(lhs, k_band, va_band, vb_band, bias, sink=None):
    s = _dot_t(lhs, k_band) + bias
    m = jnp.broadcast_to(jnp.max(s, axis=1, keepdims=True), (2 * BLOCK, LANES))
    if sink is not None:
        m = jnp.maximum(m, sink)
    p = jnp.exp2(s - jnp.concatenate([m, m], axis=1)).astype(BF16)
    p_cat = jnp.concatenate([p[:BLOCK], p[BLOCK:]], axis=1)
    lo = lax.broadcasted_iota(jnp.int32, (2 * BLOCK, LANES), 1) < HEAD_DIM
    ones_a = jnp.where(lo, 1.0, 0.0).astype(BF16)
    ones_b = jnp.where(lo, 0.0, 1.0).astype(BF16)
    rhs = jnp.concatenate([jnp.concatenate([va_band, vb_band], axis=0),
                           jnp.concatenate([ones_a, ones_b], axis=0)], axis=1)
    ext = _dot(p_cat, rhs)
    return ext[:, :LANES], _unstack_heads(m), ext[:, LANES:]


def _band_bias(slopes, window_max, dist_scale, strict):
    i = jnp.arange(BLOCK)[:, None]
    j = jnp.arange(2 * BLOCK)[None, :]
    dist = i + BLOCK - j
    valid = (dist >= 0) & ((dist < window_max) if strict else (dist <= window_max))
    alibi = (slopes[:, None, None] * (dist * dist_scale).astype(F32)[None]) * LOG2E
    h = slopes.shape[0]
    table = jnp.stack([jnp.where(valid[None], -alibi, MASKED),
                       jnp.where((valid & (j >= BLOCK))[None], -alibi, MASKED)], axis=1)
    table = table.reshape(h // 2, 2, 2, BLOCK, 2 * BLOCK).transpose(0, 2, 1, 3, 4)
    return table.reshape(h // 2, 2, 2 * BLOCK, 2 * BLOCK)


def _pair_rows(t):
    h = t.shape[0]
    return t.reshape((h // 2, 2 * t.shape[1]) + t.shape[2:])


def _swa_body(q_ref, k_ref, v_ref, bias_ref, sink_ref, o_ref, kd_scr, va_scr, vb_scr):
    s = q_ref.shape[1]
    c = pl.program_id(1)
    kv_head = c // (A_Q_HEADS // A_KV_HEADS // 2)
    lane_head = lax.broadcasted_iota(jnp.int32, (s, LANES), 1) // HEAD_DIM
    sel = lane_head == kv_head
    kf = jnp.where(sel, k_ref[0].astype(F32), 0.0)
    kd_scr[...] = (kf + pltpu.roll(kf, HEAD_DIM, axis=1)).astype(BF16)
    vf = jnp.where(sel, v_ref[0].astype(F32), 0.0)
    vr = pltpu.roll(vf, HEAD_DIM, axis=1)
    va_scr[...] = jnp.where(kv_head == 0, vf, vr).astype(BF16)
    vb_scr[...] = jnp.where(kv_head == 0, vr, vf).astype(BF16)

    def block(n, carry):
        rows = pl.ds(pl.multiple_of(n * BLOCK, BLOCK), BLOCK)
        lhs = _stack_heads(q_ref[0, rows, :])
        first = (n == 0).astype(jnp.int32)
        num, m, den = _band_softmax(lhs, _band_rows(kd_scr, n), _band_rows(va_scr, n),
                                    _band_rows(vb_scr, n), bias_ref[0, first],
                                    sink=sink_ref[0, :2 * BLOCK])
        den = den + jnp.exp2(sink_ref[0, 2 * BLOCK:] - m)
        o_ref[0, rows, :] = (num / den).astype(BF16)
        return carry

    lax.fori_loop(0, s // BLOCK, block, 0, unroll=TILE_UNROLL)


def _swa_attention(proj, sinks, slopes):
    b, s, _ = proj.shape
    n_blocks = A_Q_HEADS // 2
    k_col = A_Q_HEADS * HEAD_DIM // LANES
    bias = _band_bias(slopes, BLOCK, 1, strict=True)
    sink2 = sinks.astype(F32) * LOG2E
    stacked = _pair_rows(jnp.broadcast_to(sink2[:, None, None], (A_Q_HEADS, BLOCK, LANES)))
    by_lane = jnp.broadcast_to(jnp.repeat(sink2, HEAD_DIM).reshape(n_blocks, 1, LANES),
                               (n_blocks, BLOCK, LANES))
    sink = jnp.concatenate([stacked, by_lane], axis=1)
    return pl.pallas_call(
        _swa_body,
        grid=(b, n_blocks),
        in_specs=[
            pl.BlockSpec((1, s, LANES), lambda i, c: (i, 0, c)),
            pl.BlockSpec((1, s, LANES), lambda i, c: (i, 0, k_col)),
            pl.BlockSpec((1, s, LANES), lambda i, c: (i, 0, k_col + 1)),
            pl.BlockSpec((1, 2, 2 * BLOCK, 2 * BLOCK), lambda i, c: (c, 0, 0, 0)),
            pl.BlockSpec((1, 3 * BLOCK, LANES), lambda i, c: (c, 0, 0)),
        ],
        out_specs=pl.BlockSpec((1, s, LANES), lambda i, c: (i, 0, c)),
        out_shape=jax.ShapeDtypeStruct((b, s, n_blocks * LANES), BF16),
        scratch_shapes=[pltpu.VMEM((s, LANES), BF16)] * 3,
        compiler_params=_params("parallel", "arbitrary"),
        name="swa_attention",
    )(proj, proj, proj, bias, sink)


def _sb_softplus(z):
    return jnp.maximum(z, 0.0) + jnp.log2(1.0 + jnp.exp2(-jnp.abs(z)))


def _hi_lo(x):
    hi = x.astype(BF16)
    lo = (x - hi.astype(F32)).astype(BF16)
    return jnp.concatenate([hi, lo], axis=1)


def _sb_weighted_values(w, va, vb):
    return _dot(jnp.concatenate([w[:BLOCK], w[BLOCK:]], axis=1),
                jnp.concatenate([va, vb], axis=0))


def _sb_band_scores(lhs, k_band, first_pen, causal):
    z = _dot_t(lhs, k_band)
    z_prev, z_diag = z[:, :BLOCK] + first_pen, z[:, BLOCK:]
    drop = jnp.concatenate([_sb_softplus(z_prev), jnp.where(causal, _sb_softplus(z_diag), 0.0)],
                           axis=1)
    return z_prev, z_diag, _hi_lo(drop)


def _sb_band_weights(z_prev, z_diag, drop_hi_lo, u_band, causal):
    cs = _dot(drop_hi_lo, u_band)
    w = jnp.concatenate([jnp.exp2(z_prev - cs[:, :BLOCK]),
                         jnp.where(causal, jnp.exp2(z_diag - cs[:, BLOCK:]), 0.0)], axis=1)
    return w.astype(BF16), jnp.broadcast_to(cs[:, :1], (2 * BLOCK, LANES))


def _sb_body(q_ref, k_ref, v_ref, ub_ref, u_ref, o_ref, va_scr, vb_scr, acc_scr, run_scr,
             *, pairs, q_unroll):
    s = q_ref.shape[1]
    lo = lax.broadcasted_iota(jnp.int32, (s, pairs * LANES), 1) % LANES < HEAD_DIM
    v_all = v_ref[0]
    va_scr[...] = jnp.where(lo, v_all, jnp.zeros_like(v_all))
    vb_scr[...] = jnp.where(lo, jnp.zeros_like(v_all), v_all)
    causal = (lax.broadcasted_iota(jnp.int32, (2 * BLOCK, BLOCK), 1)
              < lax.broadcasted_iota(jnp.int32, (2 * BLOCK, BLOCK), 0) % BLOCK)
    k2 = k_ref.at[0]
    lanes = [slice(g * LANES, (g + 1) * LANES) for g in range(pairs)]

    def q_group(i, carry):
        blocks = [i * q_unroll + qi for qi in range(q_unroll)]
        tiles = [(qi, n, g) for qi, n in enumerate(blocks) for g in range(pairs)]
        scores = {}
        for qi, n, g in tiles:
            rows = pl.ds(pl.multiple_of(n * BLOCK, BLOCK), BLOCK)
            lhs = _stack_heads(q_ref[0, rows, lanes[g]])
            scores[qi, g] = _sb_band_scores(lhs, _band_rows(k2, n, lanes[g]),
                                            jnp.where(n > 0, 0.0, MASKED), causal)
        weights = {}
        for qi, n, g in tiles:
            weights[qi, g] = _sb_band_weights(*scores[qi, g], ub_ref[...], causal)
        decay = [jnp.float32(jnp.inf)] * q_unroll
        for qi, n, g in tiles:
            w, total = weights[qi, g]
            acc_scr[qi, g] = _sb_weighted_values(w, _band_rows(va_scr, n, lanes[g]),
                                                 _band_rows(vb_scr, n, lanes[g]))
            run_scr[qi, g] = total
            decay[qi] = jnp.minimum(decay[qi], jnp.min(total))

        for qi, n in enumerate(blocks):
            rows = pl.ds(pl.multiple_of(n * BLOCK, BLOCK), BLOCK)

            def more(st):
                j, least = st
                return jnp.logical_and(j >= 0, least < SB_DECAY_DONE)

            def key_block(st, qi=qi, rows=rows):
                j, _ = st
                keys = pl.ds(pl.multiple_of(j * BLOCK, BLOCK), BLOCK)
                zs = [_dot_t(_stack_heads(q_ref[0, rows, lanes[g]]), k_ref[0, keys, lanes[g]])
                      for g in range(pairs)]
                cs = [_dot(_hi_lo(_sb_softplus(z)), u_ref[...]) for z in zs]
                least = jnp.float32(jnp.inf)
                for g in range(pairs):
                    run = run_scr[qi, g]
                    w = jnp.exp2(zs[g] - cs[g][:, :BLOCK] - run).astype(BF16)
                    acc_scr[qi, g] += _sb_weighted_values(w, va_scr[keys, lanes[g]],
                                                          vb_scr[keys, lanes[g]])
                    run = run + cs[g][:, BLOCK:]
                    run_scr[qi, g] = run
                    least = jnp.minimum(least, jnp.min(run))
                return j - 1, least

            lax.while_loop(more, key_block, (n - 2, decay[qi]))
            for g in range(pairs):
                o_ref[0, rows, lanes[g]] = acc_scr[qi, g].astype(BF16)
        return carry

    lax.fori_loop(0, s // BLOCK // q_unroll, q_group, 0)


def _sb_attention(proj, q_col, k_col, v_col, pairs, q_unroll):
    b, s, _ = proj.shape
    width = pairs * LANES
    n_steps = B_HEADS // 2 // pairs
    assert q_col % pairs == 0 and k_col % pairs == 0 and v_col % pairs == 0
    assert (s // BLOCK) % q_unroll == 0

    def suffix(n):
        kk = jnp.arange(n)
        return (kk[:, None] >= kk[None, :]).astype(BF16)

    u_band = jnp.tile(suffix(2 * BLOCK), (2, 1))
    u_block = jnp.tile(jnp.concatenate([suffix(BLOCK), jnp.ones((BLOCK, BLOCK), BF16)], axis=1),
                       (2, 1))
    return pl.pallas_call(
        functools.partial(_sb_body, pairs=pairs, q_unroll=q_unroll),
        grid=(b, n_steps),
        in_specs=[
            pl.BlockSpec((1, s, width), lambda i, c: (i, 0, q_col // pairs + c)),
            pl.BlockSpec((1, s, width), lambda i, c: (i, 0, k_col // pairs + c)),
            pl.BlockSpec((1, s, width), lambda i, c: (i, 0, v_col // pairs + c)),
            pl.BlockSpec((4 * BLOCK, 2 * BLOCK), lambda i, c: (0, 0)),
            pl.BlockSpec((2 * BLOCK, 2 * BLOCK), lambda i, c: (0, 0)),
        ],
        out_specs=pl.BlockSpec((1, s, width), lambda i, c: (i, 0, c)),
        out_shape=jax.ShapeDtypeStruct((b, s, B_HEADS * HEAD_DIM), BF16),
        scratch_shapes=[pltpu.VMEM((s, width), BF16), pltpu.VMEM((s, width), BF16),
                        pltpu.VMEM((q_unroll, pairs, BLOCK, LANES), F32),
                        pltpu.VMEM((q_unroll, pairs, 2 * BLOCK, LANES), F32)],
        compiler_params=_params("parallel", "arbitrary"),
        name="stick_breaking",
    )(proj, proj, proj, u_band, u_block)


RESIDUE_STEP = 4


def _to_residue_major(src, dst, seg):
    part = seg // RESIDUE_STEP
    for base in range(0, src.shape[0], seg):
        for r in range(RESIDUE_STEP):
            dst[base + r * part:base + (r + 1) * part, :] = (
                src[pl.ds(base + r, part, stride=RESIDUE_STEP), :])


def _from_residue_major(src, dst, seg):
    part = seg // RESIDUE_STEP
    for base in range(0, src.shape[0], seg):
        for r in range(RESIDUE_STEP):
            dst[pl.ds(base + r, part, stride=RESIDUE_STEP), :] = (
                src[base + r * part:base + (r + 1) * part, :])


def _dil_body(q_ref, k_ref, v_ref, bias_ref, o_ref, tmp_a, tmp_b,
              q4, k4, q16, k16, va1, vb1, va4, vb4, va16, vb16, st_a, st_b):
    s = q_ref.shape[1]
    va1[...], vb1[...] = _split_heads(v_ref[0])
    for src, d4, d16 in ((q_ref, (q4,), (q16,)), (k_ref, (k4,), (k16,)),
                         (v_ref, (va4, vb4), (va16, vb16))):
        tmp_a[...] = src[0].astype(F32)
        _to_residue_major(tmp_a, tmp_b, s)
        _to_residue_major(tmp_b, tmp_a, s // RESIDUE_STEP)
        for dsts, tmp in ((d4, tmp_b), (d16, tmp_a)):
            vals = tmp[...].astype(BF16)
            if len(dsts) == 1:
                dsts[0][...] = vals
            else:
                dsts[0][...], dsts[1][...] = _split_heads(vals)

    layouts = ((q_ref.at[0], k_ref.at[0], va1, vb1), (q4, k4, va4, vb4), (q16, k16, va16, vb16))
    state, spare = st_a, st_b
    for step, branch in enumerate(reversed(range(len(C_PAIRS)))):
        qb, kb, va, vb = layouts[branch]
        class_blocks = s // C_PAIRS[branch][1] // BLOCK
        if step > 0:
            for a in range(3):
                _from_residue_major(state.at[a], spare.at[a],
                                    s // RESIDUE_STEP if step == 1 else s)
            state, spare = spare, state

        def tile(t, carry, step=step, branch=branch, qb=qb, kb=kb, va=va, vb=vb, state=state,
                 class_blocks=class_blocks):
            rows = pl.ds(pl.multiple_of(t * BLOCK, BLOCK), BLOCK)
            lhs = _stack_heads(qb[rows, :])
            first = (t % class_blocks == 0).astype(jnp.int32)
            num, m, den = _band_softmax(lhs, _band_rows(kb, t), _band_rows(va, t),
                                        _band_rows(vb, t), bias_ref[branch, 0, first])
            if step > 0:
                m_old = state[0, rows, :]
                m_new = jnp.maximum(m_old, m)
                a_old = jnp.exp2(m_old - m_new)
                a_cur = jnp.exp2(m - m_new)
                num = a_old * state[2, rows, :] + a_cur * num
                den = a_old * state[1, rows, :] + a_cur * den
                m = m_new
            if branch == 0:
                o_ref[0, rows, :] = (num / den).astype(BF16)
            else:
                state[0, rows, :] = m
                state[1, rows, :] = den
                state[2, rows, :] = num
            return carry

        lax.fori_loop(0, s // BLOCK, tile, 0, unroll=TILE_UNROLL)


def _dilated_attention(proj, slopes):
    b, s, _ = proj.shape
    n_blocks = C_HEADS // 2
    assert C_PAIRS[0][1] == 1 and C_PAIRS[1][1] == RESIDUE_STEP
    assert C_PAIRS[2][1] == RESIDUE_STEP ** 2
    bias = jnp.stack([_band_bias(slopes, w // d, d, strict=False)
                      for (w, d) in C_PAIRS])
    scr = ([pltpu.VMEM((s, LANES), F32)] * 2 + [pltpu.VMEM((s, LANES), BF16)] * 10
           + [pltpu.VMEM((3, s, LANES), F32)] * 2)
    return pl.pallas_call(
        _dil_body,
        grid=(b, n_blocks),
        in_specs=[
            pl.BlockSpec((1, s, LANES), lambda i, c: (i, 0, c)),
            pl.BlockSpec((1, s, LANES), lambda i, c: (i, 0, n_blocks + c)),
            pl.BlockSpec((1, s, LANES), lambda i, c: (i, 0, 2 * n_blocks + c)),
            pl.BlockSpec((len(C_PAIRS), 1, 2, 2 * BLOCK, 2 * BLOCK),
                         lambda i, c: (0, c, 0, 0, 0)),
        ],
        out_specs=pl.BlockSpec((1, s, LANES), lambda i, c: (i, 0, c)),
        out_shape=jax.ShapeDtypeStruct((b, s, n_blocks * LANES), BF16),
        scratch_shapes=scr,
        compiler_params=_params("parallel", "arbitrary"),
        name="dilated_mixture",
    )(proj, proj, proj, bias)


def _alibi_slopes(n):
    return jnp.exp2(-8.0 * jnp.arange(1, n + 1, dtype=F32) / n)


def kernel(x, attn_norm, ffn_norm, even_w_in, even_q_norm, even_k_norm, even_sinks, even_w_out,
           odd_w_in, odd_q_norm, odd_k_norm, odd_w_out, ffn_w_gate, ffn_w_up, ffn_w_down):
    b, s, d = x.shape
    depth = attn_norm.shape[0]
    scale = HEAD_DIM ** -0.5
    scale2 = scale * LOG2E
    slopes_a = _alibi_slopes(A_Q_HEADS)
    slopes_c = _alibi_slopes(C_HEADS)
    qa, kva, hb = A_Q_HEADS * HEAD_DIM, A_KV_HEADS * HEAD_DIM, B_HEADS * HEAD_DIM
    hc = C_HEADS * HEAD_DIM
    ones = lambda n: jnp.ones((n,), F32)
    zeros = lambda n: jnp.zeros((n,), F32)

    x2 = x.reshape(b * s, d)
    for i in range(depth):
        j = i // 2
        if i % 2 == 0:
            colgain = jnp.concatenate([
                jnp.tile(even_q_norm[j].astype(F32), A_Q_HEADS) * scale2,
                jnp.tile(even_k_norm[j].astype(F32), A_KV_HEADS),
                ones(kva), ones(hb) * scale2, ones(hb), ones(hb)])
            colflag = jnp.concatenate([ones(qa + kva), zeros(kva + 3 * hb)])
            proj = _norm_proj(x2, attn_norm[i], even_w_in[j].astype(BF16), colgain, colflag,
                              qa + kva, tm=512, chunk=512).reshape(b, s, -1)
            oa = _swa_attention(proj, even_sinks[j], slopes_a)
            qb_col = (qa + 2 * kva) // LANES
            ob = _sb_attention(proj, qb_col, qb_col + hb // LANES, qb_col + 2 * hb // LANES,
                               pairs=2, q_unroll=4)
            w_out = even_w_out[j].astype(BF16)
            acts = [oa.reshape(b * s, qa), ob.reshape(b * s, hb)]
        else:
            colgain = jnp.concatenate([
                jnp.tile(odd_q_norm[j].astype(F32), C_HEADS) * scale2,
                jnp.tile(odd_k_norm[j].astype(F32), C_HEADS), ones(hc)])
            colflag = jnp.concatenate([ones(2 * hc), zeros(hc)])
            proj = _norm_proj(x2, attn_norm[i], odd_w_in[j].astype(BF16), colgain, colflag,
                              2 * hc, tm=512, chunk=512).reshape(b, s, -1)
            oc = _dilated_attention(proj, slopes_c)
            w_out = odd_w_out[j].astype(BF16)
            acts = [oc.reshape(b * s, hc)]
        x2 = _out_proj(x2, acts, w_out, tm=512, tn=2048)
        x2 = _ffn(x2, ffn_norm[i], ffn_w_gate[i].astype(BF16), ffn_w_up[i].astype(BF16),
                  ffn_w_down[i].astype(BF16), tm=1024, tf=512)
    return x2.reshape(b, s, d)
```

```python
import functools

import jax
import jax.numpy as jnp
from jax import lax
from jax.experimental import pallas as pl
from jax.experimental.pallas import tpu as pltpu

HEAD_DIM = 64
LANES = 128
BLOCK = 128
A_Q_HEADS = 16
A_KV_HEADS = 2
B_HEADS = 16
C_HEADS = 32
C_PAIRS = ((128, 1), (512, 4), (2048, 16))
RMS_EPS = 1e-6
MASKED = -1e30
LOG2E = 1.4426950408889634
SB_DECAY_DONE = 127.0
VMEM_LIMIT = 56 * 1024 * 1024
TILE_UNROLL = 16

F32 = jnp.float32
BF16 = jnp.bfloat16


def _dot(a, b):
    return jnp.dot(a, b, preferred_element_type=F32)


def _dot_t(a, b):
    return lax.dot_general(a, b, (((1,), (1,)), ((), ())), preferred_element_type=F32)


def _params(*sem):
    return pltpu.CompilerParams(dimension_semantics=sem, vmem_limit_bytes=VMEM_LIMIT)


def _cast_specs(passengers, n_steps, step_of):
    in_specs, out_specs, out_shapes = [], [], []
    for w, layer in passengers:
        _, rows, cols = w.shape
        slab = rows // n_steps
        assert rows % n_steps == 0 and slab % 16 == 0
        in_specs.append(pl.BlockSpec((1, slab, cols),
                                     lambda *g, layer=layer: (layer, step_of(*g), 0)))
        out_specs.append(pl.BlockSpec((slab, cols), lambda *g: (step_of(*g), 0)))
        out_shapes.append(jax.ShapeDtypeStruct((rows, cols), BF16))
    return in_specs, out_specs, out_shapes


def _cast_rows(src_refs, dst_refs):
    for src, dst in zip(src_refs, dst_refs):
        dst[...] = src[0].astype(BF16)


def _split_refs(refs, n_in, n_cast):
    a, b, c = n_in + n_cast, n_in + n_cast + 1, n_in + 2 * n_cast + 1
    return refs[:n_in], refs[n_in:a], refs[a], refs[b:c], refs[c:]


MXU_WIDTH = 256


def _proj_body(x_ref, g_ref, w_ref, cg_ref, cf_ref, s_ref, o_ref, *, n_norm_cols, chunk):
    x = x_ref[...]
    inv = lax.rsqrt(jnp.mean(x * x, axis=-1, keepdims=True) + RMS_EPS)
    h = ((x * inv) * g_ref[...]).astype(BF16)
    n_out = o_ref.shape[1]
    for c0 in range(0, n_out, chunk):
        width = min(chunk, n_out - c0)
        y = _dot(h, w_ref[:, c0:c0 + width])
        for c in range(c0, c0 + width, MXU_WIDTH):
            sl = slice(c, c + MXU_WIDTH)
            yc = y[:, c - c0:c - c0 + MXU_WIDTH]
            cg = cg_ref[:, sl]
            if c < n_norm_cols:
                ss = _dot((yc * yc).astype(BF16), s_ref[...])
                inv_h = lax.rsqrt(ss * (1.0 / HEAD_DIM) + RMS_EPS)
                scale = jnp.where(cf_ref[:, sl] > 0.0, inv_h * cg, cg)
            else:
                scale = cg
            o_ref[:, sl] = (yc * scale).astype(BF16)


def _norm_proj(x2, gain, w, colgain, colflag, n_norm_cols, tm, chunk):
    n, d = x2.shape
    n_out = w.shape[1]
    assert n % tm == 0 and n_out % MXU_WIDTH == 0 and chunk % MXU_WIDTH == 0
    idx = jnp.arange(MXU_WIDTH) // HEAD_DIM
    seg = (idx[:, None] == idx[None, :]).astype(BF16)
    const = lambda shape: pl.BlockSpec(shape, lambda i: (0, 0), pipeline_mode=pl.Buffered(1))
    return pl.pallas_call(
        functools.partial(_proj_body, n_norm_cols=n_norm_cols, chunk=chunk),
        grid=(n // tm,),
        in_specs=[
            pl.BlockSpec((tm, d), lambda i: (i, 0)),
            const((1, d)),
            const((d, n_out)),
            const((1, n_out)),
            const((1, n_out)),
            const((MXU_WIDTH, MXU_WIDTH)),
        ],
        out_specs=pl.BlockSpec((tm, n_out), lambda i: (i, 0)),
        out_shape=jax.ShapeDtypeStruct((n, n_out), BF16),
        compiler_params=_params("parallel"),
        name="norm_proj",
    )(x2, gain.reshape(1, d), w, colgain.reshape(1, n_out), colflag.reshape(1, n_out), seg)


def _out_body(*refs, n_pairs):
    x_ref = refs[0]
    o_ref = refs[1 + 2 * n_pairs]
    acc = x_ref[...]
    for p in range(n_pairs):
        acc = acc + _dot(refs[1 + 2 * p][...], refs[2 + 2 * p][...])
    o_ref[...] = acc


def _out_proj(x2, acts, w, tm, tn):
    n, d = x2.shape
    in_specs = [pl.BlockSpec((tm, tn), lambda i, j: (i, j))]
    args = [x2]
    offset = 0
    for a in acts:
        kk = a.shape[1]
        assert offset % kk == 0
        in_specs.append(pl.BlockSpec((tm, kk), lambda i, j: (i, 0)))
        in_specs.append(pl.BlockSpec((kk, tn), lambda i, j, r=offset // kk: (r, j)))
        args += [a, w]
        offset += kk
    assert offset == w.shape[0]
    return pl.pallas_call(
        functools.partial(_out_body, n_pairs=len(acts)),
        grid=(n // tm, d // tn),
        in_specs=in_specs,
        out_specs=pl.BlockSpec((tm, tn), lambda i, j: (i, j)),
        out_shape=jax.ShapeDtypeStruct((n, d), F32),
        compiler_params=_params("parallel", "arbitrary"),
        name="out_proj",
    )(*args)


def _ffn_body(x_ref, g_ref, wg_ref, wu_ref, wd_ref, o_ref, h_scr):
    j = pl.program_id(1)

    @pl.when(j == 0)
    def _():
        x = x_ref[...]
        inv = lax.rsqrt(jnp.mean(x * x, axis=-1, keepdims=True) + RMS_EPS)
        h_scr[...] = ((x * inv) * g_ref[...]).astype(BF16)
        o_ref[...] = x

    h = h_scr[...]
    gate = _dot(h, wg_ref[...])
    up = _dot(h, wu_ref[...])
    act = (gate * (1.0 / (1.0 + jnp.exp(-gate)))) * up
    o_ref[...] += _dot(act.astype(BF16), wd_ref[...])


def _ffn(x2, gain, wg, wu, wd, tm, tf):
    n, d = x2.shape
    dff = wg.shape[1]
    assert n % tm == 0 and dff % tf == 0
    return pl.pallas_call(
        _ffn_body,
        grid=(n // tm, dff // tf),
        in_specs=[
            pl.BlockSpec((tm, d), lambda i, j: (i, 0)),
            pl.BlockSpec((1, d), lambda i, j: (0, 0)),
            pl.BlockSpec((d, tf), lambda i, j: (0, j)),
            pl.BlockSpec((d, tf), lambda i, j: (0, j)),
            pl.BlockSpec((tf, d), lambda i, j: (j, 0)),
        ],
        out_specs=pl.BlockSpec((tm, d), lambda i, j: (i, 0)),
        out_shape=jax.ShapeDtypeStruct((n, d), F32),
        scratch_shapes=[pltpu.VMEM((tm, d), BF16)],
        compiler_params=_params("parallel", "arbitrary"),
        name="ffn",
    )(x2, gain.reshape(1, d), wg, wu, wd)


def _stack_heads(q):
    lo = lax.broadcasted_iota(jnp.int32, q.shape, 1) < HEAD_DIM
    zero = jnp.zeros_like(q)
    return jnp.concatenate([jnp.where(lo, q, zero), jnp.where(lo, zero, q)], axis=0)


def _unstack_heads(t):
    lo = lax.broadcasted_iota(jnp.int32, (BLOCK, LANES), 1) < HEAD_DIM
    return jnp.where(lo, t[:BLOCK], t[BLOCK:])


def _split_heads(v):
    lo = lax.broadcasted_iota(jnp.int32, v.shape, 1) < HEAD_DIM
    zero = jnp.zeros_like(v)
    return jnp.where(lo, v, zero), jnp.where(lo, zero, v)


def _band_rows(ref, n, lanes=slice(None)):
    r0 = pl.multiple_of(n * BLOCK, BLOCK)
    p0 = pl.multiple_of(jnp.maximum(n - 1, 0) * BLOCK, BLOCK)
    return jnp.concatenate([ref[pl.ds(p0, BLOCK), lanes], ref[pl.ds(r0, BLOCK), lanes]], axis=0)


def _band_softmax(lhs, k_band, va_band, vb_band, bias, sink=None):
    s = _dot_t(lhs, k_band) + bias
    m = jnp.broadcast_to(jnp.max(s, axis=1, keepdims=True), (2 * BLOCK, LANES))
    if sink is not None:
        m = jnp.maximum(m, sink)
    p = jnp.exp2(s - jnp.concatenate([m, m], axis=1)).astype(BF16)
    p_cat = jnp.concatenate([p[:BLOCK], p[BLOCK:]], axis=1)
    lo = lax.broadcasted_iota(jnp.int32, (2 * BLOCK, LANES), 1) < HEAD_DIM
    ones_a = jnp.where(lo, 1.0, 0.0).astype(BF16)
    ones_b = jnp.where(lo, 0.0, 1.0).astype(BF16)
    rhs = jnp.concatenate([jnp.concatenate([va_band, vb_band], axis=0),
                           jnp.concatenate([ones_a, ones_b], axis=0)], axis=1)
    ext = _dot(p_cat, rhs)
    return ext[:, :LANES], _unstack_heads(m), ext[:, LANES:]


def _band_bias(slopes, window_max, dist_scale, strict):
    i = jnp.arange(BLOCK)[:, None]
    j = jnp.arange(2 * BLOCK)[None, :]
    dist = i + BLOCK - j
    valid = (dist >= 0) & ((dist < window_max) if strict else (dist <= window_max))
    alibi = (slopes[:, None, None] * (dist * dist_scale).astype(F32)[None]) * LOG2E
    h = slopes.shape[0]
    table = jnp.stack([jnp.where(valid[None], -alibi, MASKED),
                       jnp.where((valid & (j >= BLOCK))[None], -alibi, MASKED)], axis=1)
    table = table.reshape(h // 2, 2, 2, BLOCK, 2 * BLOCK).transpose(0, 2, 1, 3, 4)
    return table.reshape(h // 2, 2, 2 * BLOCK, 2 * BLOCK)


def _pair_rows(t):
    h = t.shape[0]
    return t.reshape((h // 2, 2 * t.shape[1]) + t.shape[2:])


def _swa_body(*refs, n_cast):
    ins, cast_in, o_ref, cast_out, (kd_scr, va_scr, vb_scr) = _split_refs(refs, 5, n_cast)
    q_ref, k_ref, v_ref, bias_ref, sink_ref = ins
    _cast_rows(cast_in, cast_out)
    s = q_ref.shape[1]
    c = pl.program_id(1)
    kv_head = c // (A_Q_HEADS // A_KV_HEADS // 2)
    lane_head = lax.broadcasted_iota(jnp.int32, (s, LANES), 1) // HEAD_DIM
    sel = lane_head == kv_head
    kf = jnp.where(sel, k_ref[0].astype(F32), 0.0)
    kd_scr[...] = (kf + pltpu.roll(kf, HEAD_DIM, axis=1)).astype(BF16)
    vf = jnp.where(sel, v_ref[0].astype(F32), 0.0)
    vr = pltpu.roll(vf, HEAD_DIM, axis=1)
    va_scr[...] = jnp.where(kv_head == 0, vf, vr).astype(BF16)
    vb_scr[...] = jnp.where(kv_head == 0, vr, vf).astype(BF16)

    def block(n, carry):
        rows = pl.ds(pl.multiple_of(n * BLOCK, BLOCK), BLOCK)
        lhs = _stack_heads(q_ref[0, rows, :])
        first = jnp.where(n == 0, 1, 0)
        num, m, den = _band_softmax(lhs, _band_rows(kd_scr, n), _band_rows(va_scr, n),
                                    _band_rows(vb_scr, n), bias_ref[0, first],
                                    sink=sink_ref[0, :2 * BLOCK])
        den = den + jnp.exp2(sink_ref[0, 2 * BLOCK:] - m)
        o_ref[0, rows, :] = (num / den).astype(BF16)
        return carry

    lax.fori_loop(0, s // BLOCK, block, 0, unroll=TILE_UNROLL)


def _swa_attention(proj, sinks, slopes, cast=()):
    b, s, _ = proj.shape
    n_blocks = A_Q_HEADS // 2
    cast_in, cast_out, cast_shapes = _cast_specs(cast, b * n_blocks,
                                                 lambda i, c: i * n_blocks + c)
    k_col = A_Q_HEADS * HEAD_DIM // LANES
    bias = _band_bias(slopes, BLOCK, 1, strict=True)
    sink2 = sinks.astype(F32) * LOG2E
    stacked = _pair_rows(jnp.broadcast_to(sink2[:, None, None], (A_Q_HEADS, BLOCK, LANES)))
    by_lane = jnp.broadcast_to(jnp.repeat(sink2, HEAD_DIM).reshape(n_blocks, 1, LANES),
                               (n_blocks, BLOCK, LANES))
    sink = jnp.concatenate([stacked, by_lane], axis=1)
    return pl.pallas_call(
        functools.partial(_swa_body, n_cast=len(cast)),
        grid=(b, n_blocks),
        in_specs=[
            pl.BlockSpec((1, s, LANES), lambda i, c: (i, 0, c)),
            pl.BlockSpec((1, s, LANES), lambda i, c: (i, 0, k_col)),
            pl.BlockSpec((1, s, LANES), lambda i, c: (i, 0, k_col + 1)),
            pl.BlockSpec((1, 2, 2 * BLOCK, 2 * BLOCK), lambda i, c: (c, 0, 0, 0)),
            pl.BlockSpec((1, 3 * BLOCK, LANES), lambda i, c: (c, 0, 0)),
        ] + cast_in,
        out_specs=[pl.BlockSpec((1, s, LANES), lambda i, c: (i, 0, c))] + cast_out,
        out_shape=[jax.ShapeDtypeStruct((b, s, n_blocks * LANES), BF16)] + cast_shapes,
        scratch_shapes=[pltpu.VMEM((s, LANES), BF16)] * 3,
        compiler_params=_params("arbitrary", "arbitrary"),
        name="swa_attention",
    )(proj, proj, proj, bias, sink, *[w for w, _ in cast])


def _sb_softplus(z):
    return jnp.maximum(z, 0.0) + jnp.log2(1.0 + jnp.exp2(-jnp.abs(z)))


def _hi_lo(x):
    hi = x.astype(BF16)
    lo = (x - hi.astype(F32)).astype(BF16)
    return jnp.concatenate([hi, lo], axis=1)


def _sb_weighted_values(w, va, vb):
    return _dot(jnp.concatenate([w[:BLOCK], w[BLOCK:]], axis=1),
                jnp.concatenate([va, vb], axis=0))


def _sb_band_scores(lhs, k_band, first_pen, causal):
    z = _dot_t(lhs, k_band)
    z_prev, z_diag = z[:, :BLOCK] + first_pen, z[:, BLOCK:]
    drop = jnp.concatenate([_sb_softplus(z_prev), jnp.where(causal, _sb_softplus(z_diag), 0.0)],
                           axis=1)
    return z_prev, z_diag, _hi_lo(drop)


def _sb_band_weights(z_prev, z_diag, drop_hi_lo, u_band, causal):
    cs = _dot(drop_hi_lo, u_band)
    w = jnp.concatenate([jnp.exp2(z_prev - cs[:, :BLOCK]),
                         jnp.where(causal, jnp.exp2(z_diag - cs[:, BLOCK:]), 0.0)], axis=1)
    return w.astype(BF16), jnp.broadcast_to(cs[:, :1], (2 * BLOCK, LANES))


def _sb_body(*refs, pairs, q_unroll, n_cast):
    ins, cast_in, o_ref, cast_out, scratch = _split_refs(refs, 5, n_cast)
    q_ref, k_ref, v_ref, ub_ref, u_ref = ins
    va_scr, vb_scr, acc_scr, run_scr = scratch
    _cast_rows(cast_in, cast_out)
    s = q_ref.shape[1]
    lo = lax.broadcasted_iota(jnp.int32, (s, pairs * LANES), 1) % LANES < HEAD_DIM
    v_all = v_ref[0]
    va_scr[...] = jnp.where(lo, v_all, jnp.zeros_like(v_all))
    vb_scr[...] = jnp.where(lo, jnp.zeros_like(v_all), v_all)
    causal = (lax.broadcasted_iota(jnp.int32, (2 * BLOCK, BLOCK), 1)
              < lax.broadcasted_iota(jnp.int32, (2 * BLOCK, BLOCK), 0) % BLOCK)
    k2 = k_ref.at[0]
    lanes = [slice(g * LANES, (g + 1) * LANES) for g in range(pairs)]

    def q_group(i, carry):
        blocks = [i * q_unroll + qi for qi in range(q_unroll)]
        tiles = [(qi, n, g) for qi, n in enumerate(blocks) for g in range(pairs)]
        scores = {}
        for qi, n, g in tiles:
            rows = pl.ds(pl.multiple_of(n * BLOCK, BLOCK), BLOCK)
            lhs = _stack_heads(q_ref[0, rows, lanes[g]])
            scores[qi, g] = _sb_band_scores(lhs, _band_rows(k2, n, lanes[g]),
                                            jnp.where(n > 0, 0.0, MASKED), causal)
        weights = {}
        for qi, n, g in tiles:
            weights[qi, g] = _sb_band_weights(*scores[qi, g], ub_ref[...], causal)
        decay = [jnp.float32(jnp.inf)] * q_unroll
        for qi, n, g in tiles:
            w, total = weights[qi, g]
            acc_scr[qi, g] = _sb_weighted_values(w, _band_rows(va_scr, n, lanes[g]),
                                                 _band_rows(vb_scr, n, lanes[g]))
            run_scr[qi, g] = total
            decay[qi] = jnp.minimum(decay[qi], jnp.min(total))

        for qi, n in enumerate(blocks):
            rows = pl.ds(pl.multiple_of(n * BLOCK, BLOCK), BLOCK)

            def more(st):
                j, least = st
                return jnp.logical_and(j >= 0, least < SB_DECAY_DONE)

            def key_block(st, qi=qi, rows=rows):
                j, _ = st
                keys = pl.ds(pl.multiple_of(j * BLOCK, BLOCK), BLOCK)
                zs = [_dot_t(_stack_heads(q_ref[0, rows, lanes[g]]), k_ref[0, keys, lanes[g]])
                      for g in range(pairs)]
                cs = [_dot(_hi_lo(_sb_softplus(z)), u_ref[...]) for z in zs]
                least = jnp.float32(jnp.inf)
                for g in range(pairs):
                    run = run_scr[qi, g]
                    w = jnp.exp2(zs[g] - cs[g][:, :BLOCK] - run).astype(BF16)
                    acc_scr[qi, g] += _sb_weighted_values(w, va_scr[keys, lanes[g]],
                                                          vb_scr[keys, lanes[g]])
                    run = run + cs[g][:, BLOCK:]
                    run_scr[qi, g] = run
                    least = jnp.minimum(least, jnp.min(run))
                return j - 1, least

            lax.while_loop(more, key_block, (n - 2, decay[qi]))
            for g in range(pairs):
                o_ref[0, rows, lanes[g]] = acc_scr[qi, g].astype(BF16)
        return carry

    lax.fori_loop(0, s // BLOCK // q_unroll, q_group, 0)


def _sb_attention(proj, q_col, k_col, v_col, pairs, q_unroll, cast=()):
    b, s, _ = proj.shape
    width = pairs * LANES
    n_steps = B_HEADS // 2 // pairs
    cast_in, cast_out, cast_shapes = _cast_specs(cast, b * n_steps, lambda i, c: i * n_steps + c)
    assert q_col % pairs == 0 and k_col % pairs == 0 and v_col % pairs == 0
    assert (s // BLOCK) % q_unroll == 0

    def suffix(n):
        kk = jnp.arange(n)
        return (kk[:, None] >= kk[None, :]).astype(BF16)

    u_band = jnp.tile(suffix(2 * BLOCK), (2, 1))
    u_block = jnp.tile(jnp.concatenate([suffix(BLOCK), jnp.ones((BLOCK, BLOCK), BF16)], axis=1),
                       (2, 1))
    return pl.pallas_call(
        functools.partial(_sb_body, pairs=pairs, q_unroll=q_unroll, n_cast=len(cast)),
        grid=(b, n_steps),
        in_specs=[
            pl.BlockSpec((1, s, width), lambda i, c: (i, 0, q_col // pairs + c)),
            pl.BlockSpec((1, s, width), lambda i, c: (i, 0, k_col // pairs + c)),
            pl.BlockSpec((1, s, width), lambda i, c: (i, 0, v_col // pairs + c)),
            pl.BlockSpec((4 * BLOCK, 2 * BLOCK), lambda i, c: (0, 0)),
            pl.BlockSpec((2 * BLOCK, 2 * BLOCK), lambda i, c: (0, 0)),
        ] + cast_in,
        out_specs=[pl.BlockSpec((1, s, width), lambda i, c: (i, 0, c))] + cast_out,
        out_shape=[jax.ShapeDtypeStruct((b, s, B_HEADS * HEAD_DIM), BF16)] + cast_shapes,
        scratch_shapes=[pltpu.VMEM((s, width), BF16), pltpu.VMEM((s, width), BF16),
                        pltpu.VMEM((q_unroll, pairs, BLOCK, LANES), F32),
                        pltpu.VMEM((q_unroll, pairs, 2 * BLOCK, LANES), F32)],
        compiler_params=_params("arbitrary", "arbitrary"),
        name="stick_breaking",
    )(proj, proj, proj, u_band, u_block, *[w for w, _ in cast])


RESIDUE_STEP = 4


def _to_residue_major(src, dst, seg):
    part = seg // RESIDUE_STEP
    for base in range(0, src.shape[0], seg):
        for r in range(RESIDUE_STEP):
            dst[base + r * part:base + (r + 1) * part, :] = (
                src[pl.ds(base + r, part, stride=RESIDUE_STEP), :])


def _from_residue_major(src, dst, seg):
    part = seg // RESIDUE_STEP
    for base in range(0, src.shape[0], seg):
        for r in range(RESIDUE_STEP):
            dst[pl.ds(base + r, part, stride=RESIDUE_STEP), :] = (
                src[base + r * part:base + (r + 1) * part, :])


def _dil_body(*refs, n_cast):
    ins, cast_in, o_ref, cast_out, scratch = _split_refs(refs, 4, n_cast)
    q_ref, k_ref, v_ref, bias_ref = ins
    tmp_a, tmp_b, q4, k4, q16, k16, va1, vb1, va4, vb4, va16, vb16, st_a, st_b = scratch
    _cast_rows(cast_in, cast_out)
    s = q_ref.shape[1]
    va1[...], vb1[...] = _split_heads(v_ref[0])
    for src, d4, d16 in ((q_ref, (q4,), (q16,)), (k_ref, (k4,), (k16,)),
                         (v_ref, (va4, vb4), (va16, vb16))):
        tmp_a[...] = src[0].astype(F32)
        _to_residue_major(tmp_a, tmp_b, s)
        _to_residue_major(tmp_b, tmp_a, s // RESIDUE_STEP)
        for dsts, tmp in ((d4, tmp_b), (d16, tmp_a)):
            vals = tmp[...].astype(BF16)
            if len(dsts) == 1:
                dsts[0][...] = vals
            else:
                dsts[0][...], dsts[1][...] = _split_heads(vals)

    layouts = ((q_ref.at[0], k_ref.at[0], va1, vb1), (q4, k4, va4, vb4), (q16, k16, va16, vb16))
    state, spare = st_a, st_b
    for step, branch in enumerate(reversed(range(len(C_PAIRS)))):
        qb, kb, va, vb = layouts[branch]
        class_blocks = s // C_PAIRS[branch][1] // BLOCK
        if step > 0:
            for a in range(3):
                _from_residue_major(state.at[a], spare.at[a],
                                    s // RESIDUE_STEP if step == 1 else s)
            state, spare = spare, state

        def tile(t, carry, step=step, branch=branch, qb=qb, kb=kb, va=va, vb=vb, state=state,
                 class_blocks=class_blocks):
            rows = pl.ds(pl.multiple_of(t * BLOCK, BLOCK), BLOCK)
            lhs = _stack_heads(qb[rows, :])
            first = jnp.where(t % class_blocks == 0, 1, 0)
            num, m, den = _band_softmax(lhs, _band_rows(kb, t), _band_rows(va, t),
                                        _band_rows(vb, t), bias_ref[branch, 0, first])
            if step > 0:
                m_old = state[0, rows, :]
                m_new = jnp.maximum(m_old, m)
                a_old = jnp.exp2(m_old - m_new)
                a_cur = jnp.exp2(m - m_new)
                num = a_old * state[2, rows, :] + a_cur * num
                den = a_old * state[1, rows, :] + a_cur * den
                m = m_new
            if branch == 0:
                o_ref[0, rows, :] = (num / den).astype(BF16)
            else:
                state[0, rows, :] = m
                state[1, rows, :] = den
                state[2, rows, :] = num
            return carry

        lax.fori_loop(0, s // BLOCK, tile, 0, unroll=TILE_UNROLL)


def _dilated_attention(proj, slopes, cast=()):
    b, s, _ = proj.shape
    n_blocks = C_HEADS // 2
    cast_in, cast_out, cast_shapes = _cast_specs(cast, b * n_blocks,
                                                 lambda i, c: i * n_blocks + c)
    assert C_PAIRS[0][1] == 1 and C_PAIRS[1][1] == RESIDUE_STEP
    assert C_PAIRS[2][1] == RESIDUE_STEP ** 2
    bias = jnp.stack([_band_bias(slopes, w // d, d, strict=False)
                      for (w, d) in C_PAIRS])
    scr = ([pltpu.VMEM((s, LANES), F32)] * 2 + [pltpu.VMEM((s, LANES), BF16)] * 10
           + [pltpu.VMEM((3, s, LANES), F32)] * 2)
    return pl.pallas_call(
        functools.partial(_dil_body, n_cast=len(cast)),
        grid=(b, n_blocks),
        in_specs=[
            pl.BlockSpec((1, s, LANES), lambda i, c: (i, 0, c)),
            pl.BlockSpec((1, s, LANES), lambda i, c: (i, 0, n_blocks + c)),
            pl.BlockSpec((1, s, LANES), lambda i, c: (i, 0, 2 * n_blocks + c)),
            pl.BlockSpec((len(C_PAIRS), 1, 2, 2 * BLOCK, 2 * BLOCK),
                         lambda i, c: (0, c, 0, 0, 0)),
        ] + cast_in,
        out_specs=[pl.BlockSpec((1, s, LANES), lambda i, c: (i, 0, c))] + cast_out,
        out_shape=[jax.ShapeDtypeStruct((b, s, n_blocks * LANES), BF16)] + cast_shapes,
        scratch_shapes=scr,
        compiler_params=_params("arbitrary", "arbitrary"),
        name="dilated_mixture",
    )(proj, proj, proj, bias, *[w for w, _ in cast])


def _alibi_slopes(n):
    return jnp.exp2(-8.0 * jnp.arange(1, n + 1, dtype=F32) / n)


def kernel(x, attn_norm, ffn_norm, even_w_in, even_q_norm, even_k_norm, even_sinks, even_w_out,
           odd_w_in, odd_q_norm, odd_k_norm, odd_w_out, ffn_w_gate, ffn_w_up, ffn_w_down):
    b, s, d = x.shape
    depth = attn_norm.shape[0]
    scale = HEAD_DIM ** -0.5
    scale2 = scale * LOG2E
    slopes_a = _alibi_slopes(A_Q_HEADS)
    slopes_c = _alibi_slopes(C_HEADS)
    qa, kva, hb = A_Q_HEADS * HEAD_DIM, A_KV_HEADS * HEAD_DIM, B_HEADS * HEAD_DIM
    hc = C_HEADS * HEAD_DIM
    ones = lambda n: jnp.ones((n,), F32)
    zeros = lambda n: jnp.zeros((n,), F32)

    d_ff = ffn_w_gate.shape[2]
    w_down_rows = ffn_w_down.reshape(depth, d, d_ff)
    ffn_cast = lambda i: [(ffn_w_gate, i), (ffn_w_up, i), (w_down_rows, i)]
    w_in = even_w_in[0].astype(BF16)

    x2 = x.reshape(b * s, d)
    for i in range(depth):
        j = i // 2
        if i % 2 == 0:
            colgain = jnp.concatenate([
                jnp.tile(even_q_norm[j].astype(F32), A_Q_HEADS) * scale2,
                jnp.tile(even_k_norm[j].astype(F32), A_KV_HEADS),
                ones(kva), ones(hb) * scale2, ones(hb), ones(hb)])
            colflag = jnp.concatenate([ones(qa + kva), zeros(kva + 3 * hb)])
            proj = _norm_proj(x2, attn_norm[i], w_in, colgain, colflag,
                              qa + kva, tm=512, chunk=512).reshape(b, s, -1)
            oa, w_gate, w_up, w_down = _swa_attention(proj, even_sinks[j], slopes_a,
                                                      cast=ffn_cast(i))
            qb_col = (qa + 2 * kva) // LANES
            ob, w_out, w_in = _sb_attention(
                proj, qb_col, qb_col + hb // LANES, qb_col + 2 * hb // LANES, pairs=2, q_unroll=4,
                cast=[(even_w_out, j), (odd_w_in, j)])
            acts = [oa.reshape(b * s, qa), ob.reshape(b * s, hb)]
        else:
            colgain = jnp.concatenate([
                jnp.tile(odd_q_norm[j].astype(F32), C_HEADS) * scale2,
                jnp.tile(odd_k_norm[j].astype(F32), C_HEADS), ones(hc)])
            colflag = jnp.concatenate([ones(2 * hc), zeros(hc)])
            proj = _norm_proj(x2, attn_norm[i], w_in, colgain, colflag,
                              2 * hc, tm=512, chunk=512).reshape(b, s, -1)
            cast = ffn_cast(i) + [(odd_w_out, j)]
            if i + 1 < depth:
                cast.append((even_w_in, j + 1))
            oc, w_gate, w_up, w_down, w_out, *rest = _dilated_attention(proj, slopes_c, cast=cast)
            w_in = rest[0] if rest else None
            acts = [oc.reshape(b * s, hc)]
        x2 = _out_proj(x2, acts, w_out, tm=512, tn=2048)
        x2 = _ffn(x2, ffn_norm[i], w_gate, w_up, w_down.reshape(d_ff, d), tm=1024, tf=512)
    return x2.reshape(b, s, d)
```

```python
import functools

import jax
import jax.numpy as jnp
from jax import lax
from jax.experimental import pallas as pl
from jax.experimental.pallas import tpu as pltpu

HEAD_DIM = 64
LANES = 128
BLOCK = 128
A_Q_HEADS = 16
A_KV_HEADS = 2
B_HEADS = 16
C_HEADS = 32
C_PAIRS = ((128, 1), (512, 4), (2048, 16))
RMS_EPS = 1e-6
MASKED = -1e30
LOG2E = 1.4426950408889634
SB_DECAY_DONE = 127.0
VMEM_LIMIT = 56 * 1024 * 1024
TILE_UNROLL = 16

F32 = jnp.float32
BF16 = jnp.bfloat16


def _dot(a, b):
    return jnp.dot(a, b, preferred_element_type=F32)


def _dot_t(a, b):
    return lax.dot_general(a, b, (((1,), (1,)), ((), ())), preferred_element_type=F32)


def _params(*sem):
    return pltpu.CompilerParams(dimension_semantics=sem, vmem_limit_bytes=VMEM_LIMIT)


def _cast_specs(passengers, n_steps, step_of):
    in_specs, out_specs, out_shapes = [], [], []
    for w, layer in passengers:
        _, rows, cols = w.shape
        n_col = next(c for c in (1, 2, 4, 8) if n_steps % c == 0
                     and rows % (16 * (n_steps // c)) == 0 and cols % (LANES * c) == 0)
        slab = (rows // (n_steps // n_col), cols // n_col)
        in_specs.append(pl.BlockSpec(
            (1,) + slab,
            lambda *g, layer=layer, n_col=n_col: (layer, step_of(*g) // n_col, step_of(*g) % n_col)))
        out_specs.append(pl.BlockSpec(
            slab, lambda *g, n_col=n_col: (step_of(*g) // n_col, step_of(*g) % n_col)))
        out_shapes.append(jax.ShapeDtypeStruct((rows, cols), BF16))
    return in_specs, out_specs, out_shapes


def _cast_rows(src_refs, dst_refs):
    for src, dst in zip(src_refs, dst_refs):
        dst[...] = src[0].astype(BF16)


def _split_refs(refs, n_in, n_cast):
    a, b, c = n_in + n_cast, n_in + n_cast + 1, n_in + 2 * n_cast + 1
    return refs[:n_in], refs[n_in:a], refs[a], refs[b:c], refs[c:]


MXU_WIDTH = 256


def _proj_body(x_ref, g_ref, w_ref, cg_ref, cf_ref, s_ref, o_ref, *, n_norm_cols, chunk):
    x = x_ref[...]
    inv = lax.rsqrt(jnp.mean(x * x, axis=-1, keepdims=True) + RMS_EPS)
    h = ((x * inv) * g_ref[...]).astype(BF16)
    n_out = o_ref.shape[1]
    for c0 in range(0, n_out, chunk):
        width = min(chunk, n_out - c0)
        y = _dot(h, w_ref[:, c0:c0 + width])
        for c in range(c0, c0 + width, MXU_WIDTH):
            sl = slice(c, c + MXU_WIDTH)
            yc = y[:, c - c0:c - c0 + MXU_WIDTH]
            cg = cg_ref[:, sl]
            if c < n_norm_cols:
                ss = _dot((yc * yc).astype(BF16), s_ref[...])
                inv_h = lax.rsqrt(ss * (1.0 / HEAD_DIM) + RMS_EPS)
                scale = jnp.where(cf_ref[:, sl] > 0.0, inv_h * cg, cg)
            else:
                scale = cg
            o_ref[:, sl] = (yc * scale).astype(BF16)


def _norm_proj(x2, gain, w, colgain, colflag, n_norm_cols, tm, chunk):
    n, d = x2.shape
    n_out = w.shape[1]
    assert n % tm == 0 and n_out % MXU_WIDTH == 0 and chunk % MXU_WIDTH == 0
    idx = jnp.arange(MXU_WIDTH) // HEAD_DIM
    seg = (idx[:, None] == idx[None, :]).astype(BF16)
    const = lambda shape: pl.BlockSpec(shape, lambda i: (0, 0), pipeline_mode=pl.Buffered(1))
    return pl.pallas_call(
        functools.partial(_proj_body, n_norm_cols=n_norm_cols, chunk=chunk),
        grid=(n // tm,),
        in_specs=[
            pl.BlockSpec((tm, d), lambda i: (i, 0)),
            const((1, d)),
            const((d, n_out)),
            const((1, n_out)),
            const((1, n_out)),
            const((MXU_WIDTH, MXU_WIDTH)),
        ],
        out_specs=pl.BlockSpec((tm, n_out), lambda i: (i, 0)),
        out_shape=jax.ShapeDtypeStruct((n, n_out), BF16),
        compiler_params=_params("parallel"),
        name="norm_proj",
    )(x2, gain.reshape(1, d), w, colgain.reshape(1, n_out), colflag.reshape(1, n_out), seg)


def _out_body(*refs, n_pairs):
    x_ref = refs[0]
    o_ref = refs[1 + 2 * n_pairs]
    acc = x_ref[...]
    for p in range(n_pairs):
        acc = acc + _dot(refs[1 + 2 * p][...], refs[2 + 2 * p][...])
    o_ref[...] = acc


def _out_proj(x2, acts, w, tm, tn):
    n, d = x2.shape
    in_specs = [pl.BlockSpec((tm, tn), lambda i, j: (i, j))]
    args = [x2]
    offset = 0
    for a in acts:
        kk = a.shape[1]
        assert offset % kk == 0
        in_specs.append(pl.BlockSpec((tm, kk), lambda i, j: (i, 0)))
        in_specs.append(pl.BlockSpec((kk, tn), lambda i, j, r=offset // kk: (r, j)))
        args += [a, w]
        offset += kk
    assert offset == w.shape[0]
    return pl.pallas_call(
        functools.partial(_out_body, n_pairs=len(acts)),
        grid=(n // tm, d // tn),
        in_specs=in_specs,
        out_specs=pl.BlockSpec((tm, tn), lambda i, j: (i, j)),
        out_shape=jax.ShapeDtypeStruct((n, d), F32),
        compiler_params=_params("parallel", "arbitrary"),
        name="out_proj",
    )(*args)


def _ffn_body(x_ref, g_ref, wg_ref, wu_ref, wd_ref, o_ref, h_scr):
    j = pl.program_id(1)

    @pl.when(j == 0)
    def _():
        x = x_ref[...]
        inv = lax.rsqrt(jnp.mean(x * x, axis=-1, keepdims=True) + RMS_EPS)
        h_scr[...] = ((x * inv) * g_ref[...]).astype(BF16)
        o_ref[...] = x

    h = h_scr[...]
    gate = _dot(h, wg_ref[...])
    up = _dot(h, wu_ref[...])
    act = (gate * (1.0 / (1.0 + jnp.exp(-gate)))) * up
    o_ref[...] += _dot(act.astype(BF16), wd_ref[...])


def _ffn(x2, gain, wg, wu, wd, tm, tf):
    n, d = x2.shape
    dff = wg.shape[1]
    assert n % tm == 0 and dff % tf == 0
    return pl.pallas_call(
        _ffn_body,
        grid=(n // tm, dff // tf),
        in_specs=[
            pl.BlockSpec((tm, d), lambda i, j: (i, 0)),
            pl.BlockSpec((1, d), lambda i, j: (0, 0)),
            pl.BlockSpec((d, tf), lambda i, j: (0, j)),
            pl.BlockSpec((d, tf), lambda i, j: (0, j)),
            pl.BlockSpec((tf, d), lambda i, j: (j, 0)),
        ],
        out_specs=pl.BlockSpec((tm, d), lambda i, j: (i, 0)),
        out_shape=jax.ShapeDtypeStruct((n, d), F32),
        scratch_shapes=[pltpu.VMEM((tm, d), BF16)],
        compiler_params=_params("parallel", "arbitrary"),
        name="ffn",
    )(x2, gain.reshape(1, d), wg, wu, wd)


def _stack_heads(q):
    lo = lax.broadcasted_iota(jnp.int32, q.shape, 1) < HEAD_DIM
    zero = jnp.zeros_like(q)
    return jnp.concatenate([jnp.where(lo, q, zero), jnp.where(lo, zero, q)], axis=0)


def _unstack_heads(t):
    lo = lax.broadcasted_iota(jnp.int32, (BLOCK, LANES), 1) < HEAD_DIM
    return jnp.where(lo, t[:BLOCK], t[BLOCK:])


def _split_heads(v):
    lo = lax.broadcasted_iota(jnp.int32, v.shape, 1) < HEAD_DIM
    zero = jnp.zeros_like(v)
    return jnp.where(lo, v, zero), jnp.where(lo, zero, v)


def _band_rows(ref, n, lanes=slice(None)):
    r0 = pl.multiple_of(n * BLOCK, BLOCK)
    p0 = pl.multiple_of(jnp.maximum(n - 1, 0) * BLOCK, BLOCK)
    return jnp.concatenate([ref[pl.ds(p0, BLOCK), lanes], ref[pl.ds(r0, BLOCK), lanes]], axis=0)


def _band_softmax(lhs, k_band, va_band, vb_band, bias, sink=None):
    s = _dot_t(lhs, k_band) + bias
    m = jnp.broadcast_to(jnp.max(s, axis=1, keepdims=True), (2 * BLOCK, LANES))
    if sink is not None:
        m = jnp.maximum(m, sink)
    p = jnp.exp2(s - jnp.concatenate([m, m], axis=1)).astype(BF16)
    p_cat = jnp.concatenate([p[:BLOCK], p[BLOCK:]], axis=1)
    lo = lax.broadcasted_iota(jnp.int32, (2 * BLOCK, LANES), 1) < HEAD_DIM
    ones_a = jnp.where(lo, 1.0, 0.0).astype(BF16)
    ones_b = jnp.where(lo, 0.0, 1.0).astype(BF16)
    rhs = jnp.concatenate([jnp.concatenate([va_band, vb_band], axis=0),
                           jnp.concatenate([ones_a, ones_b], axis=0)], axis=1)
    ext = _dot(p_cat, rhs)
    return ext[:, :LANES], _unstack_heads(m), ext[:, LANES:]


def _band_bias(slopes, window_max, dist_scale, strict):
    i = jnp.arange(BLOCK)[:, None]
    j = jnp.arange(2 * BLOCK)[None, :]
    dist = i + BLOCK - j
    valid = (dist >= 0) & ((dist < window_max) if strict else (dist <= window_max))
    alibi = (slopes[:, None, None] * (dist * dist_scale).astype(F32)[None]) * LOG2E
    h = slopes.shape[0]
    table = jnp.stack([jnp.where(valid[None], -alibi, MASKED),
                       jnp.where((valid & (j >= BLOCK))[None], -alibi, MASKED)], axis=1)
    table = table.reshape(h // 2, 2, 2, BLOCK, 2 * BLOCK).transpose(0, 2, 1, 3, 4)
    return table.reshape(h // 2, 2, 2 * BLOCK, 2 * BLOCK)


def _pair_rows(t):
    h = t.shape[0]
    return t.reshape((h // 2, 2 * t.shape[1]) + t.shape[2:])


def _swa_body(*refs, n_cast):
    ins, cast_in, o_ref, cast_out, (kd_scr, va_scr, vb_scr) = _split_refs(refs, 5, n_cast)
    q_ref, k_ref, v_ref, bias_ref, sink_ref = ins
    _cast_rows(cast_in, cast_out)
    s = q_ref.shape[1]
    c = pl.program_id(1)
    kv_head = c // (A_Q_HEADS // A_KV_HEADS // 2)
    lane_head = lax.broadcasted_iota(jnp.int32, (s, LANES), 1) // HEAD_DIM
    sel = lane_head == kv_head
    kf = jnp.where(sel, k_ref[0].astype(F32), 0.0)
    kd_scr[...] = (kf + pltpu.roll(kf, HEAD_DIM, axis=1)).astype(BF16)
    vf = jnp.where(sel, v_ref[0].astype(F32), 0.0)
    vr = pltpu.roll(vf, HEAD_DIM, axis=1)
    va_scr[...] = jnp.where(kv_head == 0, vf, vr).astype(BF16)
    vb_scr[...] = jnp.where(kv_head == 0, vr, vf).astype(BF16)

    def block(n, carry):
        rows = pl.ds(pl.multiple_of(n * BLOCK, BLOCK), BLOCK)
        lhs = _stack_heads(q_ref[0, rows, :])
        first = jnp.where(n == 0, 1, 0)
        num, m, den = _band_softmax(lhs, _band_rows(kd_scr, n), _band_rows(va_scr, n),
                                    _band_rows(vb_scr, n), bias_ref[0, first],
                                    sink=sink_ref[0, :2 * BLOCK])
        den = den + jnp.exp2(sink_ref[0, 2 * BLOCK:] - m)
        o_ref[0, rows, :] = (num / den).astype(BF16)
        return carry

    lax.fori_loop(0, s // BLOCK, block, 0, unroll=TILE_UNROLL)


def _swa_attention(proj, sinks, slopes, cast=()):
    b, s, _ = proj.shape
    n_blocks = A_Q_HEADS // 2
    cast_in, cast_out, cast_shapes = _cast_specs(cast, b * n_blocks,
                                                 lambda i, c: i * n_blocks + c)
    k_col = A_Q_HEADS * HEAD_DIM // LANES
    bias = _band_bias(slopes, BLOCK, 1, strict=True)
    sink2 = sinks.astype(F32) * LOG2E
    stacked = _pair_rows(jnp.broadcast_to(sink2[:, None, None], (A_Q_HEADS, BLOCK, LANES)))
    by_lane = jnp.broadcast_to(jnp.repeat(sink2, HEAD_DIM).reshape(n_blocks, 1, LANES),
                               (n_blocks, BLOCK, LANES))
    sink = jnp.concatenate([stacked, by_lane], axis=1)
    return pl.pallas_call(
        functools.partial(_swa_body, n_cast=len(cast)),
        grid=(b, n_blocks),
        in_specs=[
            pl.BlockSpec((1, s, LANES), lambda i, c: (i, 0, c)),
            pl.BlockSpec((1, s, LANES), lambda i, c: (i, 0, k_col)),
            pl.BlockSpec((1, s, LANES), lambda i, c: (i, 0, k_col + 1)),
            pl.BlockSpec((1, 2, 2 * BLOCK, 2 * BLOCK), lambda i, c: (c, 0, 0, 0)),
            pl.BlockSpec((1, 3 * BLOCK, LANES), lambda i, c: (c, 0, 0)),
        ] + cast_in,
        out_specs=[pl.BlockSpec((1, s, LANES), lambda i, c: (i, 0, c))] + cast_out,
        out_shape=[jax.ShapeDtypeStruct((b, s, n_blocks * LANES), BF16)] + cast_shapes,
        scratch_shapes=[pltpu.VMEM((s, LANES), BF16)] * 3,
        compiler_params=_params("arbitrary", "arbitrary"),
        name="swa_attention",
    )(proj, proj, proj, bias, sink, *[w for w, _ in cast])


def _sb_softplus(z):
    return jnp.maximum(z, 0.0) + jnp.log2(1.0 + jnp.exp2(-jnp.abs(z)))


def _hi_lo(x):
    hi = x.astype(BF16)
    lo = (x - hi.astype(F32)).astype(BF16)
    return jnp.concatenate([hi, lo], axis=1)


def _sb_weighted_values(w, va, vb):
    return _dot(jnp.concatenate([w[:BLOCK], w[BLOCK:]], axis=1),
                jnp.concatenate([va, vb], axis=0))


def _sb_band_scores(lhs, k_band, first_pen, causal):
    z = _dot_t(lhs, k_band)
    z_prev, z_diag = z[:, :BLOCK] + first_pen, z[:, BLOCK:]
    drop = jnp.concatenate([_sb_softplus(z_prev), jnp.where(causal, _sb_softplus(z_diag), 0.0)],
                           axis=1)
    return z_prev, z_diag, _hi_lo(drop)


def _sb_band_weights(z_prev, z_diag, drop_hi_lo, u_band, causal):
    cs = _dot(drop_hi_lo, u_band)
    w = jnp.concatenate([jnp.exp2(z_prev - cs[:, :BLOCK]),
                         jnp.where(causal, jnp.exp2(z_diag - cs[:, BLOCK:]), 0.0)], axis=1)
    return w.astype(BF16), jnp.broadcast_to(cs[:, :1], (2 * BLOCK, LANES))


def _sb_body(*refs, pairs, q_unroll, n_cast):
    ins, cast_in, o_ref, cast_out, scratch = _split_refs(refs, 5, n_cast)
    q_ref, k_ref, v_ref, ub_ref, u_ref = ins
    va_scr, vb_scr, acc_scr, run_scr = scratch
    _cast_rows(cast_in, cast_out)
    s = q_ref.shape[1]
    lo = lax.broadcasted_iota(jnp.int32, (s, pairs * LANES), 1) % LANES < HEAD_DIM
    v_all = v_ref[0]
    va_scr[...] = jnp.where(lo, v_all, jnp.zeros_like(v_all))
    vb_scr[...] = jnp.where(lo, jnp.zeros_like(v_all), v_all)
    causal = (lax.broadcasted_iota(jnp.int32, (2 * BLOCK, BLOCK), 1)
              < lax.broadcasted_iota(jnp.int32, (2 * BLOCK, BLOCK), 0) % BLOCK)
    k2 = k_ref.at[0]
    lanes = [slice(g * LANES, (g + 1) * LANES) for g in range(pairs)]

    def q_group(i, carry):
        blocks = [i * q_unroll + qi for qi in range(q_unroll)]
        tiles = [(qi, n, g) for qi, n in enumerate(blocks) for g in range(pairs)]
        scores = {}
        for qi, n, g in tiles:
            rows = pl.ds(pl.multiple_of(n * BLOCK, BLOCK), BLOCK)
            lhs = _stack_heads(q_ref[0, rows, lanes[g]])
            scores[qi, g] = _sb_band_scores(lhs, _band_rows(k2, n, lanes[g]),
                                            jnp.where(n > 0, 0.0, MASKED), causal)
        weights = {}
        for qi, n, g in tiles:
            weights[qi, g] = _sb_band_weights(*scores[qi, g], ub_ref[...], causal)
        decay = [jnp.float32(jnp.inf)] * q_unroll
        for qi, n, g in tiles:
            w, total = weights[qi, g]
            acc_scr[qi, g] = _sb_weighted_values(w, _band_rows(va_scr, n, lanes[g]),
                                                 _band_rows(vb_scr, n, lanes[g]))
            run_scr[qi, g] = total
            decay[qi] = jnp.minimum(decay[qi], jnp.min(total))

        for qi, n in enumerate(blocks):
            rows = pl.ds(pl.multiple_of(n * BLOCK, BLOCK), BLOCK)

            def more(st):
                j, least = st
                return jnp.logical_and(j >= 0, least < SB_DECAY_DONE)

            def key_block(st, qi=qi, rows=rows):
                j, _ = st
                keys = pl.ds(pl.multiple_of(j * BLOCK, BLOCK), BLOCK)
                zs = [_dot_t(_stack_heads(q_ref[0, rows, lanes[g]]), k_ref[0, keys, lanes[g]])
                      for g in range(pairs)]
                cs = [_dot(_hi_lo(_sb_softplus(z)), u_ref[...]) for z in zs]
                least = jnp.float32(jnp.inf)
                for g in range(pairs):
                    run = run_scr[qi, g]
                    w = jnp.exp2(zs[g] - cs[g][:, :BLOCK] - run).astype(BF16)
                    acc_scr[qi, g] += _sb_weighted_values(w, va_scr[keys, lanes[g]],
                                                          vb_scr[keys, lanes[g]])
                    run = run + cs[g][:, BLOCK:]
                    run_scr[qi, g] = run
                    least = jnp.minimum(least, jnp.min(run))
                return j - 1, least

            lax.while_loop(more, key_block, (n - 2, decay[qi]))
            for g in range(pairs):
                o_ref[0, rows, lanes[g]] = acc_scr[qi, g].astype(BF16)
        return carry

    lax.fori_loop(0, s // BLOCK // q_unroll, q_group, 0)


def _sb_attention(proj, q_col, k_col, v_col, pairs, q_unroll, cast=()):
    b, s, _ = proj.shape
    width = pairs * LANES
    n_steps = B_HEADS // 2 // pairs
    cast_in, cast_out, cast_shapes = _cast_specs(cast, b * n_steps, lambda i, c: i * n_steps + c)
    assert q_col % pairs == 0 and k_col % pairs == 0 and v_col % pairs == 0
    assert (s // BLOCK) % q_unroll == 0

    def suffix(n):
        kk = jnp.arange(n)
        return (kk[:, None] >= kk[None, :]).astype(BF16)

    u_band = jnp.tile(suffix(2 * BLOCK), (2, 1))
    u_block = jnp.tile(jnp.concatenate([suffix(BLOCK), jnp.ones((BLOCK, BLOCK), BF16)], axis=1),
                       (2, 1))
    return pl.pallas_call(
        functools.partial(_sb_body, pairs=pairs, q_unroll=q_unroll, n_cast=len(cast)),
        grid=(b, n_steps),
        in_specs=[
            pl.BlockSpec((1, s, width), lambda i, c: (i, 0, q_col // pairs + c)),
            pl.BlockSpec((1, s, width), lambda i, c: (i, 0, k_col // pairs + c)),
            pl.BlockSpec((1, s, width), lambda i, c: (i, 0, v_col // pairs + c)),
            pl.BlockSpec((4 * BLOCK, 2 * BLOCK), lambda i, c: (0, 0)),
            pl.BlockSpec((2 * BLOCK, 2 * BLOCK), lambda i, c: (0, 0)),
        ] + cast_in,
        out_specs=[pl.BlockSpec((1, s, width), lambda i, c: (i, 0, c))] + cast_out,
        out_shape=[jax.ShapeDtypeStruct((b, s, B_HEADS * HEAD_DIM), BF16)] + cast_shapes,
        scratch_shapes=[pltpu.VMEM((s, width), BF16), pltpu.VMEM((s, width), BF16),
                        pltpu.VMEM((q_unroll, pairs, BLOCK, LANES), F32),
                        pltpu.VMEM((q_unroll, pairs, 2 * BLOCK, LANES), F32)],
        compiler_params=_params("arbitrary", "arbitrary"),
        name="stick_breaking",
    )(proj, proj, proj, u_band, u_block, *[w for w, _ in cast])


RESIDUE_STEP = 4


def _to_residue_major(src, dst, seg):
    part = seg // RESIDUE_STEP
    for base in range(0, src.shape[0], seg):
        for r in range(RESIDUE_STEP):
            dst[base + r * part:base + (r + 1) * part, :] = (
                src[pl.ds(base + r, part, stride=RESIDUE_STEP), :])


def _from_residue_major(src, dst, seg):
    part = seg // RESIDUE_STEP
    for base in range(0, src.shape[0], seg):
        for r in range(RESIDUE_STEP):
            dst[pl.ds(base + r, part, stride=RESIDUE_STEP), :] = (
                src[base + r * part:base + (r + 1) * part, :])


def _dil_body(*refs, n_cast):
    ins, cast_in, o_ref, cast_out, scratch = _split_refs(refs, 4, n_cast)
    q_ref, k_ref, v_ref, bias_ref = ins
    tmp_a, tmp_b, q4, k4, q16, k16, va1, vb1, va4, vb4, va16, vb16, st_a, st_b = scratch
    _cast_rows(cast_in, cast_out)
    s = q_ref.shape[1]
    va1[...], vb1[...] = _split_heads(v_ref[0])
    for src, d4, d16 in ((q_ref, (q4,), (q16,)), (k_ref, (k4,), (k16,)),
                         (v_ref, (va4, vb4), (va16, vb16))):
        tmp_a[...] = src[0].astype(F32)
        _to_residue_major(tmp_a, tmp_b, s)
        _to_residue_major(tmp_b, tmp_a, s // RESIDUE_STEP)
        for dsts, tmp in ((d4, tmp_b), (d16, tmp_a)):
            vals = tmp[...].astype(BF16)
            if len(dsts) == 1:
                dsts[0][...] = vals
            else:
                dsts[0][...], dsts[1][...] = _split_heads(vals)

    layouts = ((q_ref.at[0], k_ref.at[0], va1, vb1), (q4, k4, va4, vb4), (q16, k16, va16, vb16))
    state, spare = st_a, st_b
    for step, branch in enumerate(reversed(range(len(C_PAIRS)))):
        qb, kb, va, vb = layouts[branch]
        class_blocks = s // C_PAIRS[branch][1] // BLOCK
        if step > 0:
            for a in range(3):
                _from_residue_major(state.at[a], spare.at[a],
                                    s // RESIDUE_STEP if step == 1 else s)
            state, spare = spare, state

        def tile(t, carry, step=step, branch=branch, qb=qb, kb=kb, va=va, vb=vb, state=state,
                 class_blocks=class_blocks):
            rows = pl.ds(pl.multiple_of(t * BLOCK, BLOCK), BLOCK)
            lhs = _stack_heads(qb[rows, :])
            first = jnp.where(t % class_blocks == 0, 1, 0)
            num, m, den = _band_softmax(lhs, _band_rows(kb, t), _band_rows(va, t),
                                        _band_rows(vb, t), bias_ref[branch, 0, first])
            if step > 0:
                m_old = state[0, rows, :]
                m_new = jnp.maximum(m_old, m)
                a_old = jnp.exp2(m_old - m_new)
                a_cur = jnp.exp2(m - m_new)
                num = a_old * state[2, rows, :] + a_cur * num
                den = a_old * state[1, rows, :] + a_cur * den
                m = m_new
            if branch == 0:
                o_ref[0, rows, :] = (num / den).astype(BF16)
            else:
                state[0, rows, :] = m
                state[1, rows, :] = den
                state[2, rows, :] = num
            return carry

        lax.fori_loop(0, s // BLOCK, tile, 0, unroll=TILE_UNROLL)


def _dilated_attention(proj, slopes, cast=()):
    b, s, _ = proj.shape
    n_blocks = C_HEADS // 2
    cast_in, cast_out, cast_shapes = _cast_specs(cast, b * n_blocks,
                                                 lambda i, c: i * n_blocks + c)
    assert C_PAIRS[0][1] == 1 and C_PAIRS[1][1] == RESIDUE_STEP
    assert C_PAIRS[2][1] == RESIDUE_STEP ** 2
    bias = jnp.stack([_band_bias(slopes, w // d, d, strict=False)
                      for (w, d) in C_PAIRS])
    scr = ([pltpu.VMEM((s, LANES), F32)] * 2 + [pltpu.VMEM((s, LANES), BF16)] * 10
           + [pltpu.VMEM((3, s, LANES), F32)] * 2)
    return pl.pallas_call(
        functools.partial(_dil_body, n_cast=len(cast)),
        grid=(b, n_blocks),
        in_specs=[
            pl.BlockSpec((1, s, LANES), lambda i, c: (i, 0, c)),
            pl.BlockSpec((1, s, LANES), lambda i, c: (i, 0, n_blocks + c)),
            pl.BlockSpec((1, s, LANES), lambda i, c: (i, 0, 2 * n_blocks + c)),
            pl.BlockSpec((len(C_PAIRS), 1, 2, 2 * BLOCK, 2 * BLOCK),
                         lambda i, c: (0, c, 0, 0, 0)),
        ] + cast_in,
        out_specs=[pl.BlockSpec((1, s, LANES), lambda i, c: (i, 0, c))] + cast_out,
        out_shape=[jax.ShapeDtypeStruct((b, s, n_blocks * LANES), BF16)] + cast_shapes,
        scratch_shapes=scr,
        compiler_params=_params("arbitrary", "arbitrary"),
        name="dilated_mixture",
    )(proj, proj, proj, bias, *[w for w, _ in cast])


def _alibi_slopes(n):
    return jnp.exp2(-8.0 * jnp.arange(1, n + 1, dtype=F32) / n)


def kernel(x, attn_norm, ffn_norm, even_w_in, even_q_norm, even_k_norm, even_sinks, even_w_out,
           odd_w_in, odd_q_norm, odd_k_norm, odd_w_out, ffn_w_gate, ffn_w_up, ffn_w_down):
    b, s, d = x.shape
    depth = attn_norm.shape[0]
    scale = HEAD_DIM ** -0.5
    scale2 = scale * LOG2E
    slopes_a = _alibi_slopes(A_Q_HEADS)
    slopes_c = _alibi_slopes(C_HEADS)
    qa, kva, hb = A_Q_HEADS * HEAD_DIM, A_KV_HEADS * HEAD_DIM, B_HEADS * HEAD_DIM
    hc = C_HEADS * HEAD_DIM
    ones = lambda n: jnp.ones((n,), F32)
    zeros = lambda n: jnp.zeros((n,), F32)

    ffn_cast = lambda i: [(ffn_w_gate, i), (ffn_w_up, i), (ffn_w_down, i)]
    w_in = even_w_in[0].astype(BF16)

    x2 = x.reshape(b * s, d)
    for i in range(depth):
        j = i // 2
        if i % 2 == 0:
            colgain = jnp.concatenate([
                jnp.tile(even_q_norm[j].astype(F32), A_Q_HEADS) * scale2,
                jnp.tile(even_k_norm[j].astype(F32), A_KV_HEADS),
                ones(kva), ones(hb) * scale2, ones(hb), ones(hb)])
            colflag = jnp.concatenate([ones(qa + kva), zeros(kva + 3 * hb)])
            proj = _norm_proj(x2, attn_norm[i], w_in, colgain, colflag,
                              qa + kva, tm=512, chunk=512).reshape(b, s, -1)
            oa, w_gate, w_up, w_down = _swa_attention(proj, even_sinks[j], slopes_a,
                                                      cast=ffn_cast(i))
            qb_col = (qa + 2 * kva) // LANES
            ob, w_out, w_in = _sb_attention(
                proj, qb_col, qb_col + hb // LANES, qb_col + 2 * hb // LANES, pairs=2, q_unroll=4,
                cast=[(even_w_out, j), (odd_w_in, j)])
            acts = [oa.reshape(b * s, qa), ob.reshape(b * s, hb)]
        else:
            colgain = jnp.concatenate([
                jnp.tile(odd_q_norm[j].astype(F32), C_HEADS) * scale2,
                jnp.tile(odd_k_norm[j].astype(F32), C_HEADS), ones(hc)])
            colflag = jnp.concatenate([ones(2 * hc), zeros(hc)])
            proj = _norm_proj(x2, attn_norm[i], w_in, colgain, colflag,
                              2 * hc, tm=512, chunk=512).reshape(b, s, -1)
            cast = ffn_cast(i) + [(odd_w_out, j)]
            if i + 1 < depth:
                cast.append((even_w_in, j + 1))
            oc, w_gate, w_up, w_down, w_out, *rest = _dilated_attention(proj, slopes_c, cast=cast)
            w_in = rest[0] if rest else None
            acts = [oc.reshape(b * s, hc)]
        x2 = _out_proj(x2, acts, w_out, tm=512, tn=2048)
        x2 = _ffn(x2, ffn_norm[i], w_gate, w_up, w_down, tm=1024, tf=512)
    return x2.reshape(b, s, d)
```

```python
import functools

import jax
import jax.numpy as jnp
from jax import lax
from jax.experimental import pallas as pl
from jax.experimental.pallas import tpu as pltpu

HEAD_DIM = 64
LANES = 128
BLOCK = 128
A_Q_HEADS = 16
A_KV_HEADS = 2
B_HEADS = 16
C_HEADS = 32
C_PAIRS = ((128, 1), (512, 4), (2048, 16))
RMS_EPS = 1e-6
MASKED = -1e30
LOG2E = 1.4426950408889634
SB_DECAY_DONE = 127.0
VMEM_LIMIT = 56 * 1024 * 1024
TILE_UNROLL = 16
DILATED_UNROLL = 32

F32 = jnp.float32
BF16 = jnp.bfloat16


def _dot(a, b):
    return jnp.dot(a, b, preferred_element_type=F32)


def _dot_t(a, b):
    return lax.dot_general(a, b, (((1,), (1,)), ((), ())), preferred_element_type=F32)


def _params(*sem):
    return pltpu.CompilerParams(dimension_semantics=sem, vmem_limit_bytes=VMEM_LIMIT)


def _cast_specs(passengers, n_steps, step_of):
    in_specs, out_specs, out_shapes = [], [], []
    for w, layer in passengers:
        _, rows, cols = w.shape
        n_col = next(c for c in (1, 2, 4, 8) if n_steps % c == 0
                     and rows % (16 * (n_steps // c)) == 0 and cols % (LANES * c) == 0)
        slab = (rows // (n_steps // n_col), cols // n_col)
        in_specs.append(pl.BlockSpec(
            (1,) + slab,
            lambda *g, layer=layer, n_col=n_col: (layer, step_of(*g) // n_col, step_of(*g) % n_col)))
        out_specs.append(pl.BlockSpec(
            slab, lambda *g, n_col=n_col: (step_of(*g) // n_col, step_of(*g) % n_col)))
        out_shapes.append(jax.ShapeDtypeStruct((rows, cols), BF16))
    return in_specs, out_specs, out_shapes


def _cast_rows(src_refs, dst_refs):
    for src, dst in zip(src_refs, dst_refs):
        dst[...] = src[0].astype(BF16)


def _split_refs(refs, n_in, n_cast):
    a, b, c = n_in + n_cast, n_in + n_cast + 1, n_in + 2 * n_cast + 1
    return refs[:n_in], refs[n_in:a], refs[a], refs[b:c], refs[c:]


MXU_WIDTH = 256


def _proj_body(x_ref, g_ref, w_ref, cg_ref, cf_ref, s_ref, o_ref, *, n_norm_cols, chunk):
    x = x_ref[...]
    inv = lax.rsqrt(jnp.mean(x * x, axis=-1, keepdims=True) + RMS_EPS)
    h = ((x * inv) * g_ref[...]).astype(BF16)
    n_out = o_ref.shape[1]
    for c0 in range(0, n_out, chunk):
        width = min(chunk, n_out - c0)
        y = _dot(h, w_ref[:, c0:c0 + width])
        for c in range(c0, c0 + width, MXU_WIDTH):
            sl = slice(c, c + MXU_WIDTH)
            yc = y[:, c - c0:c - c0 + MXU_WIDTH]
            cg = cg_ref[:, sl]
            if c < n_norm_cols:
                ss = _dot((yc * yc).astype(BF16), s_ref[...])
                inv_h = lax.rsqrt(ss * (1.0 / HEAD_DIM) + RMS_EPS)
                scale = jnp.where(cf_ref[:, sl] > 0.0, inv_h * cg, cg)
            else:
                scale = cg
            o_ref[:, sl] = (yc * scale).astype(BF16)


def _norm_proj(x2, gain, w, colgain, colflag, n_norm_cols, tm, chunk):
    n, d = x2.shape
    n_out = w.shape[1]
    assert n % tm == 0 and n_out % MXU_WIDTH == 0 and chunk % MXU_WIDTH == 0
    idx = jnp.arange(MXU_WIDTH) // HEAD_DIM
    seg = (idx[:, None] == idx[None, :]).astype(BF16)
    const = lambda shape: pl.BlockSpec(shape, lambda i: (0, 0), pipeline_mode=pl.Buffered(1))
    return pl.pallas_call(
        functools.partial(_proj_body, n_norm_cols=n_norm_cols, chunk=chunk),
        grid=(n // tm,),
        in_specs=[
            pl.BlockSpec((tm, d), lambda i: (i, 0)),
            const((1, d)),
            const((d, n_out)),
            const((1, n_out)),
            const((1, n_out)),
            const((MXU_WIDTH, MXU_WIDTH)),
        ],
        out_specs=pl.BlockSpec((tm, n_out), lambda i: (i, 0)),
        out_shape=jax.ShapeDtypeStruct((n, n_out), BF16),
        compiler_params=_params("parallel"),
        name="norm_proj",
    )(x2, gain.reshape(1, d), w, colgain.reshape(1, n_out), colflag.reshape(1, n_out), seg)


def _out_body(*refs, n_pairs):
    x_ref = refs[0]
    o_ref = refs[1 + 2 * n_pairs]
    acc = x_ref[...]
    for p in range(n_pairs):
        acc = acc + _dot(refs[1 + 2 * p][...], refs[2 + 2 * p][...])
    o_ref[...] = acc


def _out_proj(x2, acts, w, tm, tn):
    n, d = x2.shape
    in_specs = [pl.BlockSpec((tm, tn), lambda i, j: (i, j))]
    args = [x2]
    offset = 0
    for a in acts:
        kk = a.shape[1]
        assert offset % kk == 0
        in_specs.append(pl.BlockSpec((tm, kk), lambda i, j: (i, 0)))
        in_specs.append(pl.BlockSpec((kk, tn), lambda i, j, r=offset // kk: (r, j)))
        args += [a, w]
        offset += kk
    assert offset == w.shape[0]
    return pl.pallas_call(
        functools.partial(_out_body, n_pairs=len(acts)),
        grid=(n // tm, d // tn),
        in_specs=in_specs,
        out_specs=pl.BlockSpec((tm, tn), lambda i, j: (i, j)),
        out_shape=jax.ShapeDtypeStruct((n, d), F32),
        compiler_params=_params("parallel", "arbitrary"),
        name="out_proj",
    )(*args)


def _ffn_body(x_ref, g_ref, wg_ref, wu_ref, wd_ref, o_ref, h_scr):
    j = pl.program_id(1)

    @pl.when(j == 0)
    def _():
        x = x_ref[...]
        inv = lax.rsqrt(jnp.mean(x * x, axis=-1, keepdims=True) + RMS_EPS)
        h_scr[...] = ((x * inv) * g_ref[...]).astype(BF16)
        o_ref[...] = x

    h = h_scr[...]
    gate = _dot(h, wg_ref[...])
    up = _dot(h, wu_ref[...])
    act = (gate * (1.0 / (1.0 + jnp.exp(-gate)))) * up
    o_ref[...] += _dot(act.astype(BF16), wd_ref[...])


def _ffn(x2, gain, wg, wu, wd, tm, tf):
    n, d = x2.shape
    dff = wg.shape[1]
    assert n % tm == 0 and dff % tf == 0
    return pl.pallas_call(
        _ffn_body,
        grid=(n // tm, dff // tf),
        in_specs=[
            pl.BlockSpec((tm, d), lambda i, j: (i, 0)),
            pl.BlockSpec((1, d), lambda i, j: (0, 0)),
            pl.BlockSpec((d, tf), lambda i, j: (0, j)),
            pl.BlockSpec((d, tf), lambda i, j: (0, j)),
            pl.BlockSpec((tf, d), lambda i, j: (j, 0)),
        ],
        out_specs=pl.BlockSpec((tm, d), lambda i, j: (i, 0)),
        out_shape=jax.ShapeDtypeStruct((n, d), F32),
        scratch_shapes=[pltpu.VMEM((tm, d), BF16)],
        compiler_params=_params("parallel", "arbitrary"),
        name="ffn",
    )(x2, gain.reshape(1, d), wg, wu, wd)


def _stack_heads(q):
    lo = lax.broadcasted_iota(jnp.int32, q.shape, 1) < HEAD_DIM
    zero = jnp.zeros_like(q)
    return jnp.concatenate([jnp.where(lo, q, zero), jnp.where(lo, zero, q)], axis=0)


def _unstack_heads(t):
    lo = lax.broadcasted_iota(jnp.int32, (BLOCK, LANES), 1) < HEAD_DIM
    return jnp.where(lo, t[:BLOCK], t[BLOCK:])


def _split_heads(v):
    lo = lax.broadcasted_iota(jnp.int32, v.shape, 1) < HEAD_DIM
    zero = jnp.zeros_like(v)
    return jnp.where(lo, v, zero), jnp.where(lo, zero, v)


def _band_rows(ref, n, lanes=slice(None)):
    r0 = pl.multiple_of(n * BLOCK, BLOCK)
    p0 = pl.multiple_of(jnp.maximum(n - 1, 0) * BLOCK, BLOCK)
    return jnp.concatenate([ref[pl.ds(p0, BLOCK), lanes], ref[pl.ds(r0, BLOCK), lanes]], axis=0)


def _band_softmax(lhs, k_band, va_band, vb_band, bias, sink=None):
    s = _dot_t(lhs, k_band) + bias
    m = jnp.broadcast_to(jnp.max(s, axis=1, keepdims=True), (2 * BLOCK, LANES))
    if sink is not None:
        m = jnp.maximum(m, sink)
    p = jnp.exp2(s - jnp.concatenate([m, m], axis=1)).astype(BF16)
    p_cat = jnp.concatenate([p[:BLOCK], p[BLOCK:]], axis=1)
    lo = lax.broadcasted_iota(jnp.int32, (2 * BLOCK, LANES), 1) < HEAD_DIM
    ones_a = jnp.where(lo, 1.0, 0.0).astype(BF16)
    ones_b = jnp.where(lo, 0.0, 1.0).astype(BF16)
    rhs = jnp.concatenate([jnp.concatenate([va_band, vb_band], axis=0),
                           jnp.concatenate([ones_a, ones_b], axis=0)], axis=1)
    ext = _dot(p_cat, rhs)
    return ext[:, :LANES], _unstack_heads(m), ext[:, LANES:]


def _band_bias(slopes, window_max, dist_scale, strict):
    i = jnp.arange(BLOCK)[:, None]
    j = jnp.arange(2 * BLOCK)[None, :]
    dist = i + BLOCK - j
    valid = (dist >= 0) & ((dist < window_max) if strict else (dist <= window_max))
    alibi = (slopes[:, None, None] * (dist * dist_scale).astype(F32)[None]) * LOG2E
    h = slopes.shape[0]
    table = jnp.stack([jnp.where(valid[None], -alibi, MASKED),
                       jnp.where((valid & (j >= BLOCK))[None], -alibi, MASKED)], axis=1)
    table = table.reshape(h // 2, 2, 2, BLOCK, 2 * BLOCK).transpose(0, 2, 1, 3, 4)
    return table.reshape(h // 2, 2, 2 * BLOCK, 2 * BLOCK)


def _pair_rows(t):
    h = t.shape[0]
    return t.reshape((h // 2, 2 * t.shape[1]) + t.shape[2:])


def _swa_body(*refs, n_cast):
    ins, cast_in, o_ref, cast_out, (kd_scr, va_scr, vb_scr) = _split_refs(refs, 5, n_cast)
    q_ref, k_ref, v_ref, bias_ref, sink_ref = ins
    _cast_rows(cast_in, cast_out)
    s = q_ref.shape[1]
    c = pl.program_id(1)
    kv_head = c // (A_Q_HEADS // A_KV_HEADS // 2)
    lane_head = lax.broadcasted_iota(jnp.int32, (s, LANES), 1) // HEAD_DIM
    sel = lane_head == kv_head
    kf = jnp.where(sel, k_ref[0].astype(F32), 0.0)
    kd_scr[...] = (kf + pltpu.roll(kf, HEAD_DIM, axis=1)).astype(BF16)
    vf = jnp.where(sel, v_ref[0].astype(F32), 0.0)
    vr = pltpu.roll(vf, HEAD_DIM, axis=1)
    va_scr[...] = jnp.where(kv_head == 0, vf, vr).astype(BF16)
    vb_scr[...] = jnp.where(kv_head == 0, vr, vf).astype(BF16)

    def block(n, carry):
        rows = pl.ds(pl.multiple_of(n * BLOCK, BLOCK), BLOCK)
        lhs = _stack_heads(q_ref[0, rows, :])
        first = jnp.where(n == 0, 1, 0)
        num, m, den = _band_softmax(lhs, _band_rows(kd_scr, n), _band_rows(va_scr, n),
                                    _band_rows(vb_scr, n), bias_ref[0, first],
                                    sink=sink_ref[0, :2 * BLOCK])
        den = den + jnp.exp2(sink_ref[0, 2 * BLOCK:] - m)
        o_ref[0, rows, :] = (num / den).astype(BF16)
        return carry

    lax.fori_loop(0, s // BLOCK, block, 0, unroll=TILE_UNROLL)


def _swa_attention(proj, sinks, slopes, cast=()):
    b, s, _ = proj.shape
    n_blocks = A_Q_HEADS // 2
    cast_in, cast_out, cast_shapes = _cast_specs(cast, b * n_blocks,
                                                 lambda i, c: i * n_blocks + c)
    k_col = A_Q_HEADS * HEAD_DIM // LANES
    bias = _band_bias(slopes, BLOCK, 1, strict=True)
    sink2 = sinks.astype(F32) * LOG2E
    stacked = _pair_rows(jnp.broadcast_to(sink2[:, None, None], (A_Q_HEADS, BLOCK, LANES)))
    by_lane = jnp.broadcast_to(jnp.repeat(sink2, HEAD_DIM).reshape(n_blocks, 1, LANES),
                               (n_blocks, BLOCK, LANES))
    sink = jnp.concatenate([stacked, by_lane], axis=1)
    return pl.pallas_call(
        functools.partial(_swa_body, n_cast=len(cast)),
        grid=(b, n_blocks),
        in_specs=[
            pl.BlockSpec((1, s, LANES), lambda i, c: (i, 0, c)),
            pl.BlockSpec((1, s, LANES), lambda i, c: (i, 0, k_col)),
            pl.BlockSpec((1, s, LANES), lambda i, c: (i, 0, k_col + 1)),
            pl.BlockSpec((1, 2, 2 * BLOCK, 2 * BLOCK), lambda i, c: (c, 0, 0, 0)),
            pl.BlockSpec((1, 3 * BLOCK, LANES), lambda i, c: (c, 0, 0)),
        ] + cast_in,
        out_specs=[pl.BlockSpec((1, s, LANES), lambda i, c: (i, 0, c))] + cast_out,
        out_shape=[jax.ShapeDtypeStruct((b, s, n_blocks * LANES), BF16)] + cast_shapes,
        scratch_shapes=[pltpu.VMEM((s, LANES), BF16)] * 3,
        compiler_params=_params("arbitrary", "arbitrary"),
        name="swa_attention",
    )(proj, proj, proj, bias, sink, *[w for w, _ in cast])


def _sb_softplus(z):
    return jnp.maximum(z, 0.0) + jnp.log2(1.0 + jnp.exp2(-jnp.abs(z)))


def _hi_lo(x):
    hi = x.astype(BF16)
    lo = (x - hi.astype(F32)).astype(BF16)
    return jnp.concatenate([hi, lo], axis=1)


def _sb_weighted_values(w, va, vb):
    return _dot(jnp.concatenate([w[:BLOCK], w[BLOCK:]], axis=1),
                jnp.concatenate([va, vb], axis=0))


def _sb_band_scores(lhs, k_band, first_pen, causal):
    z = _dot_t(lhs, k_band)
    z_prev, z_diag = z[:, :BLOCK] + first_pen, z[:, BLOCK:]
    drop = jnp.concatenate([_sb_softplus(z_prev), jnp.where(causal, _sb_softplus(z_diag), 0.0)],
                           axis=1)
    return z_prev, z_diag, _hi_lo(drop)


def _sb_band_weights(z_prev, z_diag, drop_hi_lo, u_band, causal):
    cs = _dot(drop_hi_lo, u_band)
    w = jnp.concatenate([jnp.exp2(z_prev - cs[:, :BLOCK]),
                         jnp.where(causal, jnp.exp2(z_diag - cs[:, BLOCK:]), 0.0)], axis=1)
    return w.astype(BF16), jnp.broadcast_to(cs[:, :1], (2 * BLOCK, LANES))


def _sb_body(*refs, pairs, q_unroll, n_cast):
    ins, cast_in, o_ref, cast_out, scratch = _split_refs(refs, 5, n_cast)
    q_ref, k_ref, v_ref, ub_ref, u_ref = ins
    va_scr, vb_scr, acc_scr, run_scr = scratch
    _cast_rows(cast_in, cast_out)
    s = q_ref.shape[1]
    lo = lax.broadcasted_iota(jnp.int32, (s, pairs * LANES), 1) % LANES < HEAD_DIM
    v_all = v_ref[0]
    va_scr[...] = jnp.where(lo, v_all, jnp.zeros_like(v_all))
    vb_scr[...] = jnp.where(lo, jnp.zeros_like(v_all), v_all)
    causal = (lax.broadcasted_iota(jnp.int32, (2 * BLOCK, BLOCK), 1)
              < lax.broadcasted_iota(jnp.int32, (2 * BLOCK, BLOCK), 0) % BLOCK)
    k2 = k_ref.at[0]
    lanes = [slice(g * LANES, (g + 1) * LANES) for g in range(pairs)]

    def q_group(i, carry):
        blocks = [i * q_unroll + qi for qi in range(q_unroll)]
        tiles = [(qi, n, g) for qi, n in enumerate(blocks) for g in range(pairs)]
        scores = {}
        for qi, n, g in tiles:
            rows = pl.ds(pl.multiple_of(n * BLOCK, BLOCK), BLOCK)
            lhs = _stack_heads(q_ref[0, rows, lanes[g]])
            scores[qi, g] = _sb_band_scores(lhs, _band_rows(k2, n, lanes[g]),
                                            jnp.where(n > 0, 0.0, MASKED), causal)
        weights = {}
        for qi, n, g in tiles:
            weights[qi, g] = _sb_band_weights(*scores[qi, g], ub_ref[...], causal)
        decay = [jnp.float32(jnp.inf)] * q_unroll
        for qi, n, g in tiles:
            w, total = weights[qi, g]
            acc_scr[qi, g] = _sb_weighted_values(w, _band_rows(va_scr, n, lanes[g]),
                                                 _band_rows(vb_scr, n, lanes[g]))
            run_scr[qi, g] = total
            decay[qi] = jnp.minimum(decay[qi], jnp.min(total))

        for qi, n in enumerate(blocks):
            rows = pl.ds(pl.multiple_of(n * BLOCK, BLOCK), BLOCK)

            def more(st):
                j, least = st
                return jnp.logical_and(j >= 0, least < SB_DECAY_DONE)

            def key_block(st, qi=qi, rows=rows):
                j, _ = st
                keys = pl.ds(pl.multiple_of(j * BLOCK, BLOCK), BLOCK)
                zs = [_dot_t(_stack_heads(q_ref[0, rows, lanes[g]]), k_ref[0, keys, lanes[g]])
                      for g in range(pairs)]
                cs = [_dot(_hi_lo(_sb_softplus(z)), u_ref[...]) for z in zs]
                least = jnp.float32(jnp.inf)
                for g in range(pairs):
                    run = run_scr[qi, g]
                    w = jnp.exp2(zs[g] - cs[g][:, :BLOCK] - run).astype(BF16)
                    acc_scr[qi, g] += _sb_weighted_values(w, va_scr[keys, lanes[g]],
                                                          vb_scr[keys, lanes[g]])
                    run = run + cs[g][:, BLOCK:]
                    run_scr[qi, g] = run
                    least = jnp.minimum(least, jnp.min(run))
                return j - 1, least

            lax.while_loop(more, key_block, (n - 2, decay[qi]))
            for g in range(pairs):
                o_ref[0, rows, lanes[g]] = acc_scr[qi, g].astype(BF16)
        return carry

    lax.fori_loop(0, s // BLOCK // q_unroll, q_group, 0)


def _sb_attention(proj, q_col, k_col, v_col, pairs, q_unroll, cast=()):
    b, s, _ = proj.shape
    width = pairs * LANES
    n_steps = B_HEADS // 2 // pairs
    cast_in, cast_out, cast_shapes = _cast_specs(cast, b * n_steps, lambda i, c: i * n_steps + c)
    assert q_col % pairs == 0 and k_col % pairs == 0 and v_col % pairs == 0
    assert (s // BLOCK) % q_unroll == 0

    def suffix(n):
        kk = jnp.arange(n)
        return (kk[:, None] >= kk[None, :]).astype(BF16)

    u_band = jnp.tile(suffix(2 * BLOCK), (2, 1))
    u_block = jnp.tile(jnp.concatenate([suffix(BLOCK), jnp.ones((BLOCK, BLOCK), BF16)], axis=1),
                       (2, 1))
    return pl.pallas_call(
        functools.partial(_sb_body, pairs=pairs, q_unroll=q_unroll, n_cast=len(cast)),
        grid=(b, n_steps),
        in_specs=[
            pl.BlockSpec((1, s, width), lambda i, c: (i, 0, q_col // pairs + c)),
            pl.BlockSpec((1, s, width), lambda i, c: (i, 0, k_col // pairs + c)),
            pl.BlockSpec((1, s, width), lambda i, c: (i, 0, v_col // pairs + c)),
            pl.BlockSpec((4 * BLOCK, 2 * BLOCK), lambda i, c: (0, 0)),
            pl.BlockSpec((2 * BLOCK, 2 * BLOCK), lambda i, c: (0, 0)),
        ] + cast_in,
        out_specs=[pl.BlockSpec((1, s, width), lambda i, c: (i, 0, c))] + cast_out,
        out_shape=[jax.ShapeDtypeStruct((b, s, B_HEADS * HEAD_DIM), BF16)] + cast_shapes,
        scratch_shapes=[pltpu.VMEM((s, width), BF16), pltpu.VMEM((s, width), BF16),
                        pltpu.VMEM((q_unroll, pairs, BLOCK, LANES), F32),
                        pltpu.VMEM((q_unroll, pairs, 2 * BLOCK, LANES), F32)],
        compiler_params=_params("arbitrary", "arbitrary"),
        name="stick_breaking",
    )(proj, proj, proj, u_band, u_block, *[w for w, _ in cast])


RESIDUE_STEP = 4


def _to_residue_major(src, dst, seg):
    part = seg // RESIDUE_STEP
    for base in range(0, src.shape[0], seg):
        for r in range(RESIDUE_STEP):
            dst[base + r * part:base + (r + 1) * part, :] = (
                src[pl.ds(base + r, part, stride=RESIDUE_STEP), :])


def _from_residue_major(src, dst, seg):
    part = seg // RESIDUE_STEP
    for base in range(0, src.shape[0], seg):
        for r in range(RESIDUE_STEP):
            dst[pl.ds(base + r, part, stride=RESIDUE_STEP), :] = (
                src[base + r * part:base + (r + 1) * part, :])


def _dil_body(*refs, n_cast):
    ins, cast_in, o_ref, cast_out, scratch = _split_refs(refs, 4, n_cast)
    q_ref, k_ref, v_ref, bias_ref = ins
    tmp_a, tmp_b, q4, k4, q16, k16, va1, vb1, va4, vb4, va16, vb16, st_a, st_b = scratch
    _cast_rows(cast_in, cast_out)
    s = q_ref.shape[1]
    va1[...], vb1[...] = _split_heads(v_ref[0])
    for src, d4, d16 in ((q_ref, (q4,), (q16,)), (k_ref, (k4,), (k16,)),
                         (v_ref, (va4, vb4), (va16, vb16))):
        tmp_a[...] = src[0].astype(F32)
        _to_residue_major(tmp_a, tmp_b, s)
        _to_residue_major(tmp_b, tmp_a, s // RESIDUE_STEP)
        for dsts, tmp in ((d4, tmp_b), (d16, tmp_a)):
            vals = tmp[...].astype(BF16)
            if len(dsts) == 1:
                dsts[0][...] = vals
            else:
                dsts[0][...], dsts[1][...] = _split_heads(vals)

    layouts = ((q_ref.at[0], k_ref.at[0], va1, vb1), (q4, k4, va4, vb4), (q16, k16, va16, vb16))
    state, spare = st_a, st_b
    for step, branch in enumerate(reversed(range(len(C_PAIRS)))):
        qb, kb, va, vb = layouts[branch]
        class_blocks = s // C_PAIRS[branch][1] // BLOCK
        if step > 0:
            for a in range(3):
                _from_residue_major(state.at[a], spare.at[a],
                                    s // RESIDUE_STEP if step == 1 else s)
            state, spare = spare, state

        def tile(t, carry, step=step, branch=branch, qb=qb, kb=kb, va=va, vb=vb, state=state,
                 class_blocks=class_blocks):
            rows = pl.ds(pl.multiple_of(t * BLOCK, BLOCK), BLOCK)
            lhs = _stack_heads(qb[rows, :])
            first = jnp.where(t % class_blocks == 0, 1, 0)
            num, m, den = _band_softmax(lhs, _band_rows(kb, t), _band_rows(va, t),
                                        _band_rows(vb, t), bias_ref[branch, 0, first])
            if step > 0:
                m_old = state[0, rows, :]
                m_new = jnp.maximum(m_old, m)
                a_old = jnp.exp2(m_old - m_new)
                a_cur = jnp.exp2(m - m_new)
                num = a_old * state[2, rows, :] + a_cur * num
                den = a_old * state[1, rows, :] + a_cur * den
                m = m_new
            if branch == 0:
                o_ref[0, rows, :] = (num / den).astype(BF16)
            else:
                state[0, rows, :] = m
                state[1, rows, :] = den
                state[2, rows, :] = num
            return carry

        lax.fori_loop(0, s // BLOCK, tile, 0, unroll=DILATED_UNROLL)


def _dilated_attention(proj, slopes, cast=()):
    b, s, _ = proj.shape
    n_blocks = C_HEADS // 2
    cast_in, cast_out, cast_shapes = _cast_specs(cast, b * n_blocks,
                                                 lambda i, c: i * n_blocks + c)
    assert C_PAIRS[0][1] == 1 and C_PAIRS[1][1] == RESIDUE_STEP
    assert C_PAIRS[2][1] == RESIDUE_STEP ** 2
    bias = jnp.stack([_band_bias(slopes, w // d, d, strict=False)
                      for (w, d) in C_PAIRS])
    scr = ([pltpu.VMEM((s, LANES), F32)] * 2 + [pltpu.VMEM((s, LANES), BF16)] * 10
           + [pltpu.VMEM((3, s, LANES), F32)] * 2)
    return pl.pallas_call(
        functools.partial(_dil_body, n_cast=len(cast)),
        grid=(b, n_blocks),
        in_specs=[
            pl.BlockSpec((1, s, LANES), lambda i, c: (i, 0, c)),
            pl.BlockSpec((1, s, LANES), lambda i, c: (i, 0, n_blocks + c)),
            pl.BlockSpec((1, s, LANES), lambda i, c: (i, 0, 2 * n_blocks + c)),
            pl.BlockSpec((len(C_PAIRS), 1, 2, 2 * BLOCK, 2 * BLOCK),
                         lambda i, c: (0, c, 0, 0, 0)),
        ] + cast_in,
        out_specs=[pl.BlockSpec((1, s, LANES), lambda i, c: (i, 0, c))] + cast_out,
        out_shape=[jax.ShapeDtypeStruct((b, s, n_blocks * LANES), BF16)] + cast_shapes,
        scratch_shapes=scr,
        compiler_params=_params("arbitrary", "arbitrary"),
        name="dilated_mixture",
    )(proj, proj, proj, bias, *[w for w, _ in cast])


def _alibi_slopes(n):
    return jnp.exp2(-8.0 * jnp.arange(1, n + 1, dtype=F32) / n)


def kernel(x, attn_norm, ffn_norm, even_w_in, even_q_norm, even_k_norm, even_sinks, even_w_out,
           odd_w_in, odd_q_norm, odd_k_norm, odd_w_out, ffn_w_gate, ffn_w_up, ffn_w_down):
    b, s, d = x.shape
    depth = attn_norm.shape[0]
    scale = HEAD_DIM ** -0.5
    scale2 = scale * LOG2E
    slopes_a = _alibi_slopes(A_Q_HEADS)
    slopes_c = _alibi_slopes(C_HEADS)
    qa, kva, hb = A_Q_HEADS * HEAD_DIM, A_KV_HEADS * HEAD_DIM, B_HEADS * HEAD_DIM
    hc = C_HEADS * HEAD_DIM
    ones = lambda n: jnp.ones((n,), F32)
    zeros = lambda n: jnp.zeros((n,), F32)

    ffn_cast = lambda i: [(ffn_w_gate, i), (ffn_w_up, i), (ffn_w_down, i)]
    w_in = even_w_in[0].astype(BF16)

    x2 = x.reshape(b * s, d)
    for i in range(depth):
        j = i // 2
        if i % 2 == 0:
            colgain = jnp.concatenate([
                jnp.tile(even_q_norm[j].astype(F32), A_Q_HEADS) * scale2,
                jnp.tile(even_k_norm[j].astype(F32), A_KV_HEADS),
                ones(kva), ones(hb) * scale2, ones(hb), ones(hb)])
            colflag = jnp.concatenate([ones(qa + kva), zeros(kva + 3 * hb)])
            proj = _norm_proj(x2, attn_norm[i], w_in, colgain, colflag,
                              qa + kva, tm=512, chunk=512).reshape(b, s, -1)
            oa, w_gate, w_up, w_down = _swa_attention(proj, even_sinks[j], slopes_a,
                                                      cast=ffn_cast(i))
            qb_col = (qa + 2 * kva) // LANES
            ob, w_out, w_in = _sb_attention(
                proj, qb_col, qb_col + hb // LANES, qb_col + 2 * hb // LANES, pairs=2, q_unroll=4,
                cast=[(even_w_out, j), (odd_w_in, j)])
            acts = [oa.reshape(b * s, qa), ob.reshape(b * s, hb)]
        else:
            colgain = jnp.concatenate([
                jnp.tile(odd_q_norm[j].astype(F32), C_HEADS) * scale2,
                jnp.tile(odd_k_norm[j].astype(F32), C_HEADS), ones(hc)])
            colflag = jnp.concatenate([ones(2 * hc), zeros(hc)])
            proj = _norm_proj(x2, attn_norm[i], w_in, colgain, colflag,
                              2 * hc, tm=512, chunk=512).reshape(b, s, -1)
            cast = ffn_cast(i) + [(odd_w_out, j)]
            if i + 1 < depth:
                cast.append((even_w_in, j + 1))
            oc, w_gate, w_up, w_down, w_out, *rest = _dilated_attention(proj, slopes_c, cast=cast)
            w_in = rest[0] if rest else None
            acts = [oc.reshape(b * s, hc)]
        x2 = _out_proj(x2, acts, w_out, tm=512, tn=2048)
        x2 = _ffn(x2, ffn_norm[i], w_gate, w_up, w_down, tm=1024, tf=512)
    return x2.reshape(b, s, d)
```

```python
import functools

import jax
import jax.numpy as jnp
from jax import lax
from jax.experimental import pallas as pl
from jax.experimental.pallas import tpu as pltpu

HEAD_DIM = 64
LANES = 128
BLOCK = 128
A_Q_HEADS = 16
A_KV_HEADS = 2
B_HEADS = 16
C_HEADS = 32
C_PAIRS = ((128, 1), (512, 4), (2048, 16))
RMS_EPS = 1e-6
MASKED = -1e30
LOG2E = 1.4426950408889634
SB_DECAY_DONE = 127.0
VMEM_LIMIT = 56 * 1024 * 1024
TILE_UNROLL = 16
DILATED_UNROLL = 32

F32 = jnp.float32
BF16 = jnp.bfloat16


def _dot(a, b):
    return jnp.dot(a, b, preferred_element_type=F32)


def _dot_t(a, b):
    return lax.dot_general(a, b, (((1,), (1,)), ((), ())), preferred_element_type=F32)


def _params(*sem):
    return pltpu.CompilerParams(dimension_semantics=sem, vmem_limit_bytes=VMEM_LIMIT)


def _cast_specs(passengers, n_steps, step_of):
    in_specs, out_specs, out_shapes = [], [], []
    for w, layer in passengers:
        _, rows, cols = w.shape
        n_col = next(c for c in (1, 2, 4, 8) if n_steps % c == 0
                     and rows % (16 * (n_steps // c)) == 0 and cols % (LANES * c) == 0)
        slab = (rows // (n_steps // n_col), cols // n_col)
        in_specs.append(pl.BlockSpec(
            (1,) + slab,
            lambda *g, layer=layer, n_col=n_col: (layer, step_of(*g) // n_col, step_of(*g) % n_col)))
        out_specs.append(pl.BlockSpec(
            slab, lambda *g, n_col=n_col: (step_of(*g) // n_col, step_of(*g) % n_col)))
        out_shapes.append(jax.ShapeDtypeStruct((rows, cols), BF16))
    return in_specs, out_specs, out_shapes


def _cast_rows(src_refs, dst_refs):
    for src, dst in zip(src_refs, dst_refs):
        dst[...] = src[0].astype(BF16)


def _split_refs(refs, n_in, n_cast):
    a, b, c = n_in + n_cast, n_in + n_cast + 1, n_in + 2 * n_cast + 1
    return refs[:n_in], refs[n_in:a], refs[a], refs[b:c], refs[c:]


MXU_WIDTH = 256


def _proj_body(x_ref, g_ref, w_ref, cg_ref, cf_ref, s_ref, o_ref, *, n_norm_cols, chunk):
    x = x_ref[...]
    inv = lax.rsqrt(jnp.mean(x * x, axis=-1, keepdims=True) + RMS_EPS)
    h = ((x * inv) * g_ref[...]).astype(BF16)
    n_out = o_ref.shape[1]
    for c0 in range(0, n_out, chunk):
        width = min(chunk, n_out - c0)
        y = _dot(h, w_ref[:, c0:c0 + width])
        for c in range(c0, c0 + width, MXU_WIDTH):
            sl = slice(c, c + MXU_WIDTH)
            yc = y[:, c - c0:c - c0 + MXU_WIDTH]
            cg = cg_ref[:, sl]
            if c < n_norm_cols:
                ss = _dot((yc * yc).astype(BF16), s_ref[...])
                inv_h = lax.rsqrt(ss * (1.0 / HEAD_DIM) + RMS_EPS)
                scale = jnp.where(cf_ref[:, sl] > 0.0, inv_h * cg, cg)
            else:
                scale = cg
            o_ref[:, sl] = (yc * scale).astype(BF16)


def _norm_proj(x2, gain, w, colgain, colflag, n_norm_cols, tm, chunk):
    n, d = x2.shape
    n_out = w.shape[1]
    assert n % tm == 0 and n_out % MXU_WIDTH == 0 and chunk % MXU_WIDTH == 0
    idx = jnp.arange(MXU_WIDTH) // HEAD_DIM
    seg = (idx[:, None] == idx[None, :]).astype(BF16)
    const = lambda shape: pl.BlockSpec(shape, lambda i: (0, 0), pipeline_mode=pl.Buffered(1))
    return pl.pallas_call(
        functools.partial(_proj_body, n_norm_cols=n_norm_cols, chunk=chunk),
        grid=(n // tm,),
        in_specs=[
            pl.BlockSpec((tm, d), lambda i: (i, 0)),
            const((1, d)),
            const((d, n_out)),
            const((1, n_out)),
            const((1, n_out)),
            const((MXU_WIDTH, MXU_WIDTH)),
        ],
        out_specs=pl.BlockSpec((tm, n_out), lambda i: (i, 0)),
        out_shape=jax.ShapeDtypeStruct((n, n_out), BF16),
        compiler_params=_params("parallel"),
        name="norm_proj",
    )(x2, gain.reshape(1, d), w, colgain.reshape(1, n_out), colflag.reshape(1, n_out), seg)


def _out_body(*refs, n_pairs):
    x_ref = refs[0]
    o_ref = refs[1 + 2 * n_pairs]
    acc = x_ref[...]
    for p in range(n_pairs):
        acc = acc + _dot(refs[1 + 2 * p][...], refs[2 + 2 * p][...])
    o_ref[...] = acc


def _out_proj(x2, acts, w, tm, tn):
    n, d = x2.shape
    in_specs = [pl.BlockSpec((tm, tn), lambda i, j: (i, j))]
    args = [x2]
    offset = 0
    for a in acts:
        kk = a.shape[1]
        assert offset % kk == 0
        in_specs.append(pl.BlockSpec((tm, kk), lambda i, j: (i, 0)))
        in_specs.append(pl.BlockSpec((kk, tn), lambda i, j, r=offset // kk: (r, j)))
        args += [a, w]
        offset += kk
    assert offset == w.shape[0]
    return pl.pallas_call(
        functools.partial(_out_body, n_pairs=len(acts)),
        grid=(n // tm, d // tn),
        in_specs=in_specs,
        out_specs=pl.BlockSpec((tm, tn), lambda i, j: (i, j)),
        out_shape=jax.ShapeDtypeStruct((n, d), F32),
        compiler_params=_params("parallel", "arbitrary"),
        name="out_proj",
    )(*args)


def _ffn_body(x_ref, g_ref, wg_ref, wu_ref, wd_ref, o_ref, h_scr):
    j = pl.program_id(1)

    @pl.when(j == 0)
    def _():
        x = x_ref[...]
        inv = lax.rsqrt(jnp.mean(x * x, axis=-1, keepdims=True) + RMS_EPS)
        h_scr[...] = ((x * inv) * g_ref[...]).astype(BF16)
        o_ref[...] = x

    h = h_scr[...]
    gate = _dot(h, wg_ref[...])
    up = _dot(h, wu_ref[...])
    act = (gate * (1.0 / (1.0 + jnp.exp(-gate)))) * up
    o_ref[...] += _dot(act.astype(BF16), wd_ref[...])


def _ffn(x2, gain, wg, wu, wd, tm, tf):
    n, d = x2.shape
    dff = wg.shape[1]
    assert n % tm == 0 and dff % tf == 0
    return pl.pallas_call(
        _ffn_body,
        grid=(n // tm, dff // tf),
        in_specs=[
            pl.BlockSpec((tm, d), lambda i, j: (i, 0)),
            pl.BlockSpec((1, d), lambda i, j: (0, 0)),
            pl.BlockSpec((d, tf), lambda i, j: (0, j)),
            pl.BlockSpec((d, tf), lambda i, j: (0, j)),
            pl.BlockSpec((tf, d), lambda i, j: (j, 0)),
        ],
        out_specs=pl.BlockSpec((tm, d), lambda i, j: (i, 0)),
        out_shape=jax.ShapeDtypeStruct((n, d), F32),
        scratch_shapes=[pltpu.VMEM((tm, d), BF16)],
        compiler_params=_params("parallel", "arbitrary"),
        name="ffn",
    )(x2, gain.reshape(1, d), wg, wu, wd)


def _stack_heads(q):
    lo = lax.broadcasted_iota(jnp.int32, q.shape, 1) < HEAD_DIM
    zero = jnp.zeros_like(q)
    return jnp.concatenate([jnp.where(lo, q, zero), jnp.where(lo, zero, q)], axis=0)


def _unstack_heads(t):
    lo = lax.broadcasted_iota(jnp.int32, (BLOCK, LANES), 1) < HEAD_DIM
    return jnp.where(lo, t[:BLOCK], t[BLOCK:])


def _split_heads(v):
    lo = lax.broadcasted_iota(jnp.int32, v.shape, 1) < HEAD_DIM
    zero = jnp.zeros_like(v)
    return jnp.where(lo, v, zero), jnp.where(lo, zero, v)


def _band_rows(ref, n, lanes=slice(None)):
    r0 = pl.multiple_of(n * BLOCK, BLOCK)
    p0 = pl.multiple_of(jnp.maximum(n - 1, 0) * BLOCK, BLOCK)
    return jnp.concatenate([ref[pl.ds(p0, BLOCK), lanes], ref[pl.ds(r0, BLOCK), lanes]], axis=0)


def _band_softmax(lhs, k_band, va_band, vb_band, bias, sink=None):
    s = _dot_t(lhs, k_band) + bias
    m = jnp.broadcast_to(jnp.max(s, axis=1, keepdims=True), (2 * BLOCK, LANES))
    if sink is not None:
        m = jnp.maximum(m, sink)
    p = jnp.exp2(s - jnp.concatenate([m, m], axis=1)).astype(BF16)
    p_cat = jnp.concatenate([p[:BLOCK], p[BLOCK:]], axis=1)
    lo = lax.broadcasted_iota(jnp.int32, (2 * BLOCK, LANES), 1) < HEAD_DIM
    ones_a = jnp.where(lo, 1.0, 0.0).astype(BF16)
    ones_b = jnp.where(lo, 0.0, 1.0).astype(BF16)
    rhs = jnp.concatenate([jnp.concatenate([va_band, vb_band], axis=0),
                           jnp.concatenate([ones_a, ones_b], axis=0)], axis=1)
    ext = _dot(p_cat, rhs)
    return ext[:, :LANES], _unstack_heads(m), ext[:, LANES:]


def _band_bias(slopes, window_max, dist_scale, strict):
    i = jnp.arange(BLOCK)[:, None]
    j = jnp.arange(2 * BLOCK)[None, :]
    dist = i + BLOCK - j
    valid = (dist >= 0) & ((dist < window_max) if strict else (dist <= window_max))
    alibi = (slopes[:, None, None] * (dist * dist_scale).astype(F32)[None]) * LOG2E
    h = slopes.shape[0]
    table = jnp.stack([jnp.where(valid[None], -alibi, MASKED),
                       jnp.where((valid & (j >= BLOCK))[None], -alibi, MASKED)], axis=1)
    table = table.reshape(h // 2, 2, 2, BLOCK, 2 * BLOCK).transpose(0, 2, 1, 3, 4)
    return table.reshape(h // 2, 2, 2 * BLOCK, 2 * BLOCK)


def _pair_rows(t):
    h = t.shape[0]
    return t.reshape((h // 2, 2 * t.shape[1]) + t.shape[2:])


def _swa_body(*refs, n_cast):
    ins, cast_in, o_ref, cast_out, (kd_scr, va_scr, vb_scr) = _split_refs(refs, 5, n_cast)
    q_ref, k_ref, v_ref, bias_ref, sink_ref = ins
    _cast_rows(cast_in, cast_out)
    s = q_ref.shape[1]
    c = pl.program_id(1)
    blocks_per_kv = A_Q_HEADS // A_KV_HEADS // 2
    kv_head = c // blocks_per_kv

    @pl.when(c % blocks_per_kv == 0)
    def _():
        lane_head = lax.broadcasted_iota(jnp.int32, (s, LANES), 1) // HEAD_DIM
        sel = lane_head == kv_head
        kf = jnp.where(sel, k_ref[0].astype(F32), 0.0)
        kd_scr[...] = (kf + pltpu.roll(kf, HEAD_DIM, axis=1)).astype(BF16)
        vf = jnp.where(sel, v_ref[0].astype(F32), 0.0)
        vr = pltpu.roll(vf, HEAD_DIM, axis=1)
        va_scr[...] = jnp.where(kv_head == 0, vf, vr).astype(BF16)
        vb_scr[...] = jnp.where(kv_head == 0, vr, vf).astype(BF16)

    def block(n, carry):
        rows = pl.ds(pl.multiple_of(n * BLOCK, BLOCK), BLOCK)
        lhs = _stack_heads(q_ref[0, rows, :])
        first = jnp.where(n == 0, 1, 0)
        num, m, den = _band_softmax(lhs, _band_rows(kd_scr, n), _band_rows(va_scr, n),
                                    _band_rows(vb_scr, n), bias_ref[0, first],
                                    sink=sink_ref[0, :2 * BLOCK])
        den = den + jnp.exp2(sink_ref[0, 2 * BLOCK:] - m)
        o_ref[0, rows, :] = (num / den).astype(BF16)
        return carry

    lax.fori_loop(0, s // BLOCK, block, 0, unroll=TILE_UNROLL)


def _swa_attention(proj, sinks, slopes, cast=()):
    b, s, _ = proj.shape
    n_blocks = A_Q_HEADS // 2
    cast_in, cast_out, cast_shapes = _cast_specs(cast, b * n_blocks,
                                                 lambda i, c: i * n_blocks + c)
    k_col = A_Q_HEADS * HEAD_DIM // LANES
    bias = _band_bias(slopes, BLOCK, 1, strict=True)
    sink2 = sinks.astype(F32) * LOG2E
    stacked = _pair_rows(jnp.broadcast_to(sink2[:, None, None], (A_Q_HEADS, BLOCK, LANES)))
    by_lane = jnp.broadcast_to(jnp.repeat(sink2, HEAD_DIM).reshape(n_blocks, 1, LANES),
                               (n_blocks, BLOCK, LANES))
    sink = jnp.concatenate([stacked, by_lane], axis=1)
    return pl.pallas_call(
        functools.partial(_swa_body, n_cast=len(cast)),
        grid=(b, n_blocks),
        in_specs=[
            pl.BlockSpec((1, s, LANES), lambda i, c: (i, 0, c)),
            pl.BlockSpec((1, s, LANES), lambda i, c: (i, 0, k_col)),
            pl.BlockSpec((1, s, LANES), lambda i, c: (i, 0, k_col + 1)),
            pl.BlockSpec((1, 2, 2 * BLOCK, 2 * BLOCK), lambda i, c: (c, 0, 0, 0)),
            pl.BlockSpec((1, 3 * BLOCK, LANES), lambda i, c: (c, 0, 0)),
        ] + cast_in,
        out_specs=[pl.BlockSpec((1, s, LANES), lambda i, c: (i, 0, c))] + cast_out,
        out_shape=[jax.ShapeDtypeStruct((b, s, n_blocks * LANES), BF16)] + cast_shapes,
        scratch_shapes=[pltpu.VMEM((s, LANES), BF16)] * 3,
        compiler_params=_params("arbitrary", "arbitrary"),
        name="swa_attention",
    )(proj, proj, proj, bias, sink, *[w for w, _ in cast])


def _sb_softplus(z):
    return jnp.maximum(z, 0.0) + jnp.log2(1.0 + jnp.exp2(-jnp.abs(z)))


def _hi_lo(x):
    hi = x.astype(BF16)
    lo = (x - hi.astype(F32)).astype(BF16)
    return jnp.concatenate([hi, lo], axis=1)


def _sb_weighted_values(w, va, vb):
    return _dot(jnp.concatenate([w[:BLOCK], w[BLOCK:]], axis=1),
                jnp.concatenate([va, vb], axis=0))


def _sb_band_scores(lhs, k_band, first_pen, causal):
    z = _dot_t(lhs, k_band)
    z_prev, z_diag = z[:, :BLOCK] + first_pen, z[:, BLOCK:]
    drop = jnp.concatenate([_sb_softplus(z_prev), jnp.where(causal, _sb_softplus(z_diag), 0.0)],
                           axis=1)
    return z_prev, z_diag, _hi_lo(drop)


def _sb_band_weights(z_prev, z_diag, drop_hi_lo, u_band, causal):
    cs = _dot(drop_hi_lo, u_band)
    w = jnp.concatenate([jnp.exp2(z_prev - cs[:, :BLOCK]),
                         jnp.where(causal, jnp.exp2(z_diag - cs[:, BLOCK:]), 0.0)], axis=1)
    return w.astype(BF16), jnp.broadcast_to(cs[:, :1], (2 * BLOCK, LANES))


def _sb_body(*refs, pairs, q_unroll, n_cast):
    ins, cast_in, o_ref, cast_out, scratch = _split_refs(refs, 5, n_cast)
    q_ref, k_ref, v_ref, ub_ref, u_ref = ins
    va_scr, vb_scr, acc_scr, run_scr = scratch
    _cast_rows(cast_in, cast_out)
    s = q_ref.shape[1]
    lo = lax.broadcasted_iota(jnp.int32, (s, pairs * LANES), 1) % LANES < HEAD_DIM
    v_all = v_ref[0]
    va_scr[...] = jnp.where(lo, v_all, jnp.zeros_like(v_all))
    vb_scr[...] = jnp.where(lo, jnp.zeros_like(v_all), v_all)
    causal = (lax.broadcasted_iota(jnp.int32, (2 * BLOCK, BLOCK), 1)
              < lax.broadcasted_iota(jnp.int32, (2 * BLOCK, BLOCK), 0) % BLOCK)
    k2 = k_ref.at[0]
    lanes = [slice(g * LANES, (g + 1) * LANES) for g in range(pairs)]

    def q_group(i, carry):
        blocks = [i * q_unroll + qi for qi in range(q_unroll)]
        tiles = [(qi, n, g) for qi, n in enumerate(blocks) for g in range(pairs)]
        scores = {}
        for qi, n, g in tiles:
            rows = pl.ds(pl.multiple_of(n * BLOCK, BLOCK), BLOCK)
            lhs = _stack_heads(q_ref[0, rows, lanes[g]])
            scores[qi, g] = _sb_band_scores(lhs, _band_rows(k2, n, lanes[g]),
                                            jnp.where(n > 0, 0.0, MASKED), causal)
        weights = {}
        for qi, n, g in tiles:
            weights[qi, g] = _sb_band_weights(*scores[qi, g], ub_ref[...], causal)
        decay = [jnp.float32(jnp.inf)] * q_unroll
        for qi, n, g in tiles:
            w, total = weights[qi, g]
            acc_scr[qi, g] = _sb_weighted_values(w, _band_rows(va_scr, n, lanes[g]),
                                                 _band_rows(vb_scr, n, lanes[g]))
            run_scr[qi, g] = total
            decay[qi] = jnp.minimum(decay[qi], jnp.min(total))

        for qi, n in enumerate(blocks):
            rows = pl.ds(pl.multiple_of(n * BLOCK, BLOCK), BLOCK)

            def more(st):
                j, least = st
                return jnp.logical_and(j >= 0, least < SB_DECAY_DONE)

            def key_block(st, qi=qi, rows=rows):
                j, _ = st
                keys = pl.ds(pl.multiple_of(j * BLOCK, BLOCK), BLOCK)
                zs = [_dot_t(_stack_heads(q_ref[0, rows, lanes[g]]), k_ref[0, keys, lanes[g]])
                      for g in range(pairs)]
                cs = [_dot(_hi_lo(_sb_softplus(z)), u_ref[...]) for z in zs]
                least = jnp.float32(jnp.inf)
                for g in range(pairs):
                    run = run_scr[qi, g]
                    w = jnp.exp2(zs[g] - cs[g][:, :BLOCK] - run).astype(BF16)
                    acc_scr[qi, g] += _sb_weighted_values(w, va_scr[keys, lanes[g]],
                                                          vb_scr[keys, lanes[g]])
                    run = run + cs[g][:, BLOCK:]
                    run_scr[qi, g] = run
                    least = jnp.minimum(least, jnp.min(run))
                return j - 1, least

            lax.while_loop(more, key_block, (n - 2, decay[qi]))
            for g in range(pairs):
                o_ref[0, rows, lanes[g]] = acc_scr[qi, g].astype(BF16)
        return carry

    lax.fori_loop(0, s // BLOCK // q_unroll, q_group, 0)


def _sb_attention(proj, q_col, k_col, v_col, pairs, q_unroll, cast=()):
    b, s, _ = proj.shape
    width = pairs * LANES
    n_steps = B_HEADS // 2 // pairs
    cast_in, cast_out, cast_shapes = _cast_specs(cast, b * n_steps, lambda i, c: i * n_steps + c)
    assert q_col % pairs == 0 and k_col % pairs == 0 and v_col % pairs == 0
    assert (s // BLOCK) % q_unroll == 0

    def suffix(n):
        kk = jnp.arange(n)
        return (kk[:, None] >= kk[None, :]).astype(BF16)

    u_band = jnp.tile(suffix(2 * BLOCK), (2, 1))
    u_block = jnp.tile(jnp.concatenate([suffix(BLOCK), jnp.ones((BLOCK, BLOCK), BF16)], axis=1),
                       (2, 1))
    return pl.pallas_call(
        functools.partial(_sb_body, pairs=pairs, q_unroll=q_unroll, n_cast=len(cast)),
        grid=(b, n_steps),
        in_specs=[
            pl.BlockSpec((1, s, width), lambda i, c: (i, 0, q_col // pairs + c)),
            pl.BlockSpec((1, s, width), lambda i, c: (i, 0, k_col // pairs + c)),
            pl.BlockSpec((1, s, width), lambda i, c: (i, 0, v_col // pairs + c)),
            pl.BlockSpec((4 * BLOCK, 2 * BLOCK), lambda i, c: (0, 0)),
            pl.BlockSpec((2 * BLOCK, 2 * BLOCK), lambda i, c: (0, 0)),
        ] + cast_in,
        out_specs=[pl.BlockSpec((1, s, width), lambda i, c: (i, 0, c))] + cast_out,
        out_shape=[jax.ShapeDtypeStruct((b, s, B_HEADS * HEAD_DIM), BF16)] + cast_shapes,
        scratch_shapes=[pltpu.VMEM((s, width), BF16), pltpu.VMEM((s, width), BF16),
                        pltpu.VMEM((q_unroll, pairs, BLOCK, LANES), F32),
                        pltpu.VMEM((q_unroll, pairs, 2 * BLOCK, LANES), F32)],
        compiler_params=_params("arbitrary", "arbitrary"),
        name="stick_breaking",
    )(proj, proj, proj, u_band, u_block, *[w for w, _ in cast])


RESIDUE_STEP = 4


def _to_residue_major(src, dst, seg):
    part = seg // RESIDUE_STEP
    for base in range(0, src.shape[0], seg):
        for r in range(RESIDUE_STEP):
            dst[base + r * part:base + (r + 1) * part, :] = (
                src[pl.ds(base + r, part, stride=RESIDUE_STEP), :])


def _from_residue_major(src, dst, seg):
    part = seg // RESIDUE_STEP
    for base in range(0, src.shape[0], seg):
        for r in range(RESIDUE_STEP):
            dst[pl.ds(base + r, part, stride=RESIDUE_STEP), :] = (
                src[base + r * part:base + (r + 1) * part, :])


def _dil_body(*refs, n_cast):
    ins, cast_in, o_ref, cast_out, scratch = _split_refs(refs, 4, n_cast)
    q_ref, k_ref, v_ref, bias_ref = ins
    tmp_a, tmp_b, q4, k4, q16, k16, va1, vb1, va4, vb4, va16, vb16, st_a, st_b = scratch
    _cast_rows(cast_in, cast_out)
    s = q_ref.shape[1]
    va1[...], vb1[...] = _split_heads(v_ref[0])
    for src, d4, d16 in ((q_ref, (q4,), (q16,)), (k_ref, (k4,), (k16,)),
                         (v_ref, (va4, vb4), (va16, vb16))):
        tmp_a[...] = src[0].astype(F32)
        _to_residue_major(tmp_a, tmp_b, s)
        _to_residue_major(tmp_b, tmp_a, s // RESIDUE_STEP)
        for dsts, tmp in ((d4, tmp_b), (d16, tmp_a)):
            vals = tmp[...].astype(BF16)
            if len(dsts) == 1:
                dsts[0][...] = vals
            else:
                dsts[0][...], dsts[1][...] = _split_heads(vals)

    layouts = ((q_ref.at[0], k_ref.at[0], va1, vb1), (q4, k4, va4, vb4), (q16, k16, va16, vb16))
    state, spare = st_a, st_b
    for step, branch in enumerate(reversed(range(len(C_PAIRS)))):
        qb, kb, va, vb = layouts[branch]
        class_blocks = s // C_PAIRS[branch][1] // BLOCK
        if step > 0:
            for a in range(3):
                _from_residue_major(state.at[a], spare.at[a],
                                    s // RESIDUE_STEP if step == 1 else s)
            state, spare = spare, state

        def tile(t, carry, step=step, branch=branch, qb=qb, kb=kb, va=va, vb=vb, state=state,
                 class_blocks=class_blocks):
            rows = pl.ds(pl.multiple_of(t * BLOCK, BLOCK), BLOCK)
            lhs = _stack_heads(qb[rows, :])
            first = jnp.where(t % class_blocks == 0, 1, 0)
            num, m, den = _band_softmax(lhs, _band_rows(kb, t), _band_rows(va, t),
                                        _band_rows(vb, t), bias_ref[branch, 0, first])
            if step > 0:
                m_old = state[0, rows, :]
                m_new = jnp.maximum(m_old, m)
                a_old = jnp.exp2(m_old - m_new)
                a_cur = jnp.exp2(m - m_new)
                num = a_old * state[2, rows, :] + a_cur * num
                den = a_old * state[1, rows, :] + a_cur * den
                m = m_new
            if branch == 0:
                o_ref[0, rows, :] = (num / den).astype(BF16)
            else:
                state[0, rows, :] = m
                state[1, rows, :] = den
                state[2, rows, :] = num
            return carry

        lax.fori_loop(0, s // BLOCK, tile, 0, unroll=DILATED_UNROLL)


def _dilated_attention(proj, slopes, cast=()):
    b, s, _ = proj.shape
    n_blocks = C_HEADS // 2
    cast_in, cast_out, cast_shapes = _cast_specs(cast, b * n_blocks,
                                                 lambda i, c: i * n_blocks + c)
    assert C_PAIRS[0][1] == 1 and C_PAIRS[1][1] == RESIDUE_STEP
    assert C_PAIRS[2][1] == RESIDUE_STEP ** 2
    bias = jnp.stack([_band_bias(slopes, w // d, d, strict=False)
                      for (w, d) in C_PAIRS])
    scr = ([pltpu.VMEM((s, LANES), F32)] * 2 + [pltpu.VMEM((s, LANES), BF16)] * 10
           + [pltpu.VMEM((3, s, LANES), F32)] * 2)
    return pl.pallas_call(
        functools.partial(_dil_body, n_cast=len(cast)),
        grid=(b, n_blocks),
        in_specs=[
            pl.BlockSpec((1, s, LANES), lambda i, c: (i, 0, c)),
            pl.BlockSpec((1, s, LANES), lambda i, c: (i, 0, n_blocks + c)),
            pl.BlockSpec((1, s, LANES), lambda i, c: (i, 0, 2 * n_blocks + c)),
            pl.BlockSpec((len(C_PAIRS), 1, 2, 2 * BLOCK, 2 * BLOCK),
                         lambda i, c: (0, c, 0, 0, 0)),
        ] + cast_in,
        out_specs=[pl.BlockSpec((1, s, LANES), lambda i, c: (i, 0, c))] + cast_out,
        out_shape=[jax.ShapeDtypeStruct((b, s, n_blocks * LANES), BF16)] + cast_shapes,
        scratch_shapes=scr,
        compiler_params=_params("arbitrary", "arbitrary"),
        name="dilated_mixture",
    )(proj, proj, proj, bias, *[w for w, _ in cast])


def _alibi_slopes(n):
    return jnp.exp2(-8.0 * jnp.arange(1, n + 1, dtype=F32) / n)


def kernel(x, attn_norm, ffn_norm, even_w_in, even_q_norm, even_k_norm, even_sinks, even_w_out,
           odd_w_in, odd_q_norm, odd_k_norm, odd_w_out, ffn_w_gate, ffn_w_up, ffn_w_down):
    b, s, d = x.shape
    depth = attn_norm.shape[0]
    scale = HEAD_DIM ** -0.5
    scale2 = scale * LOG2E
    slopes_a = _alibi_slopes(A_Q_HEADS)
    slopes_c = _alibi_slopes(C_HEADS)
    qa, kva, hb = A_Q_HEADS * HEAD_DIM, A_KV_HEADS * HEAD_DIM, B_HEADS * HEAD_DIM
    hc = C_HEADS * HEAD_DIM
    ones = lambda n: jnp.ones((n,), F32)
    zeros = lambda n: jnp.zeros((n,), F32)

    ffn_cast = lambda i: [(ffn_w_gate, i), (ffn_w_up, i), (ffn_w_down, i)]
    w_in = even_w_in[0].astype(BF16)

    x2 = x.reshape(b * s, d)
    for i in range(depth):
        j = i // 2
        if i % 2 == 0:
            colgain = jnp.concatenate([
                jnp.tile(even_q_norm[j].astype(F32), A_Q_HEADS) * scale2,
                jnp.tile(even_k_norm[j].astype(F32), A_KV_HEADS),
                ones(kva), ones(hb) * scale2, ones(hb), ones(hb)])
            colflag = jnp.concatenate([ones(qa + kva), zeros(kva + 3 * hb)])
            proj = _norm_proj(x2, attn_norm[i], w_in, colgain, colflag,
                              qa + kva, tm=512, chunk=512).reshape(b, s, -1)
            oa, w_gate, w_up, w_down = _swa_attention(proj, even_sinks[j], slopes_a,
                                                      cast=ffn_cast(i))
            qb_col = (qa + 2 * kva) // LANES
            ob, w_out, w_in = _sb_attention(
                proj, qb_col, qb_col + hb // LANES, qb_col + 2 * hb // LANES, pairs=2, q_unroll=4,
                cast=[(even_w_out, j), (odd_w_in, j)])
            acts = [oa.reshape(b * s, qa), ob.reshape(b * s, hb)]
        else:
            colgain = jnp.concatenate([
                jnp.tile(odd_q_norm[j].astype(F32), C_HEADS) * scale2,
                jnp.tile(odd_k_norm[j].astype(F32), C_HEADS), ones(hc)])
            colflag = jnp.concatenate([ones(2 * hc), zeros(hc)])
            proj = _norm_proj(x2, attn_norm[i], w_in, colgain, colflag,
                              2 * hc, tm=512, chunk=512).reshape(b, s, -1)
            cast = ffn_cast(i) + [(odd_w_out, j)]
            if i + 1 < depth:
                cast.append((even_w_in, j + 1))
            oc, w_gate, w_up, w_down, w_out, *rest = _dilated_attention(proj, slopes_c, cast=cast)
            w_in = rest[0] if rest else None
            acts = [oc.reshape(b * s, hc)]
        x2 = _out_proj(x2, acts, w_out, tm=512, tn=2048)
        x2 = _ffn(x2, ffn_norm[i], w_gate, w_up, w_down, tm=1024, tf=512)
    return x2.reshape(b, s, d)
```

```python
import functools

import jax
import jax.numpy as jnp
from jax import lax
from jax.experimental import pallas as pl
from jax.experimental.pallas import tpu as pltpu

HEAD_DIM = 64
LANES = 128
BLOCK = 128
A_Q_HEADS = 16
A_KV_HEADS = 2
B_HEADS = 16
C_HEADS = 32
C_PAIRS = ((128, 1), (512, 4), (2048, 16))
RMS_EPS = 1e-6
MASKED = -1e30
LOG2E = 1.4426950408889634
SB_DECAY_DONE = 127.0
VMEM_LIMIT = 56 * 1024 * 1024
MXU_WIDTH = 256
BF16_SUBLANES = 16
TILE_UNROLL = 16
DILATED_UNROLL = 32
SB_PAIRS = 2
SB_Q_UNROLL = 4
PROJ_ROWS = 512
PROJ_CHUNK_EVEN = 1024
PROJ_CHUNK_ODD = 512
OUT_ROWS = 512
FFN_ROWS = 1024
FFN_COLS = 512

F32 = jnp.float32
BF16 = jnp.bfloat16


def _dot(a, b):
    return jnp.dot(a, b, preferred_element_type=F32)


def _dot_t(a, b):
    return lax.dot_general(a, b, (((1,), (1,)), ((), ())), preferred_element_type=F32)


def _params(*sem):
    return pltpu.CompilerParams(dimension_semantics=sem, vmem_limit_bytes=VMEM_LIMIT)


def _cast_specs(passengers, n_steps, step_of):
    in_specs, out_specs, out_shapes = [], [], []
    for w, layer in passengers:
        _, rows, cols = w.shape
        n_col = next(c for c in (1, 2, 4, 8) if n_steps % c == 0
                     and rows % (BF16_SUBLANES * (n_steps // c)) == 0
                     and cols % (LANES * c) == 0)
        slab = (rows // (n_steps // n_col), cols // n_col)
        in_specs.append(pl.BlockSpec(
            (1,) + slab,
            lambda *g, layer=layer, n_col=n_col: (layer, step_of(*g) // n_col, step_of(*g) % n_col)))
        out_specs.append(pl.BlockSpec(
            slab, lambda *g, n_col=n_col: (step_of(*g) // n_col, step_of(*g) % n_col)))
        out_shapes.append(jax.ShapeDtypeStruct((rows, cols), BF16))
    return in_specs, out_specs, out_shapes


def _cast_rows(src_refs, dst_refs):
    for src, dst in zip(src_refs, dst_refs):
        dst[...] = src[0].astype(BF16)


def _split_refs(refs, n_in, n_cast):
    a, b, c = n_in + n_cast, n_in + n_cast + 1, n_in + 2 * n_cast + 1
    return refs[:n_in], refs[n_in:a], refs[a], refs[b:c], refs[c:]


def _proj_body(x_ref, g_ref, w_ref, cg_ref, cf_ref, s_ref, o_ref, *, n_norm_cols, chunk):
    x = x_ref[...]
    inv = lax.rsqrt(jnp.mean(x * x, axis=-1, keepdims=True) + RMS_EPS)
    h = ((x * inv) * g_ref[...]).astype(BF16)
    n_out = o_ref.shape[1]
    for c0 in range(0, n_out, chunk):
        width = min(chunk, n_out - c0)
        y = _dot(h, w_ref[:, c0:c0 + width])
        for c in range(c0, c0 + width, MXU_WIDTH):
            sl = slice(c, c + MXU_WIDTH)
            yc = y[:, c - c0:c - c0 + MXU_WIDTH]
            cg = cg_ref[:, sl]
            if c < n_norm_cols:
                ss = _dot((yc * yc).astype(BF16), s_ref[...])
                inv_h = lax.rsqrt(ss * (1.0 / HEAD_DIM) + RMS_EPS)
                scale = jnp.where(cf_ref[:, sl] > 0.0, inv_h * cg, cg)
            else:
                scale = cg
            o_ref[:, sl] = (yc * scale).astype(BF16)


def _norm_proj(x2, gain, w, colgain, colflag, n_norm_cols, tm, chunk):
    n, d = x2.shape
    n_out = w.shape[1]
    assert n % tm == 0 and n_out % MXU_WIDTH == 0 and chunk % MXU_WIDTH == 0
    idx = jnp.arange(MXU_WIDTH) // HEAD_DIM
    seg = (idx[:, None] == idx[None, :]).astype(BF16)
    const = lambda shape: pl.BlockSpec(shape, lambda i: (0, 0), pipeline_mode=pl.Buffered(1))
    return pl.pallas_call(
        functools.partial(_proj_body, n_norm_cols=n_norm_cols, chunk=chunk),
        grid=(n // tm,),
        in_specs=[
            pl.BlockSpec((tm, d), lambda i: (i, 0)),
            const((1, d)),
            const((d, n_out)),
            const((1, n_out)),
            const((1, n_out)),
            const((MXU_WIDTH, MXU_WIDTH)),
        ],
        out_specs=pl.BlockSpec((tm, n_out), lambda i: (i, 0)),
        out_shape=jax.ShapeDtypeStruct((n, n_out), BF16),
        compiler_params=_params("parallel"),
        name="norm_proj",
    )(x2, gain.reshape(1, d), w, colgain.reshape(1, n_out), colflag.reshape(1, n_out), seg)


def _out_body(*refs, n_pairs):
    x_ref = refs[0]
    o_ref = refs[1 + 2 * n_pairs]
    acc = x_ref[...]
    for p in range(n_pairs):
        acc = acc + _dot(refs[1 + 2 * p][...], refs[2 + 2 * p][...])
    o_ref[...] = acc


def _out_proj(x2, acts, w, tm):
    n, d = x2.shape
    in_specs = [pl.BlockSpec((tm, d), lambda i: (i, 0))]
    args = [x2]
    offset = 0
    for a in acts:
        kk = a.shape[1]
        assert offset % kk == 0
        in_specs.append(pl.BlockSpec((tm, kk), lambda i: (i, 0)))
        in_specs.append(pl.BlockSpec((kk, d), lambda i, r=offset // kk: (r, 0)))
        args += [a, w]
        offset += kk
    assert offset == w.shape[0] and n % tm == 0
    return pl.pallas_call(
        functools.partial(_out_body, n_pairs=len(acts)),
        grid=(n // tm,),
        in_specs=in_specs,
        out_specs=pl.BlockSpec((tm, d), lambda i: (i, 0)),
        out_shape=jax.ShapeDtypeStruct((n, d), F32),
        compiler_params=_params("parallel"),
        name="out_proj",
    )(*args)


def _ffn_body(x_ref, g_ref, wg_ref, wu_ref, wd_ref, o_ref, h_scr):
    j = pl.program_id(1)

    @pl.when(j == 0)
    def _():
        x = x_ref[...]
        inv = lax.rsqrt(jnp.mean(x * x, axis=-1, keepdims=True) + RMS_EPS)
        h_scr[...] = ((x * inv) * g_ref[...]).astype(BF16)
        o_ref[...] = x

    h = h_scr[...]
    gate = _dot(h, wg_ref[...])
    up = _dot(h, wu_ref[...])
    act = (gate * (1.0 / (1.0 + jnp.exp(-gate)))) * up
    o_ref[...] += _dot(act.astype(BF16), wd_ref[...])


def _ffn(x2, gain, wg, wu, wd, tm, tf):
    n, d = x2.shape
    dff = wg.shape[1]
    assert n % tm == 0 and dff % tf == 0
    return pl.pallas_call(
        _ffn_body,
        grid=(n // tm, dff // tf),
        in_specs=[
            pl.BlockSpec((tm, d), lambda i, j: (i, 0)),
            pl.BlockSpec((1, d), lambda i, j: (0, 0)),
            pl.BlockSpec((d, tf), lambda i, j: (0, j)),
            pl.BlockSpec((d, tf), lambda i, j: (0, j)),
            pl.BlockSpec((tf, d), lambda i, j: (j, 0)),
        ],
        out_specs=pl.BlockSpec((tm, d), lambda i, j: (i, 0)),
        out_shape=jax.ShapeDtypeStruct((n, d), F32),
        scratch_shapes=[pltpu.VMEM((tm, d), BF16)],
        compiler_params=_params("parallel", "arbitrary"),
        name="ffn",
    )(x2, gain.reshape(1, d), wg, wu, wd)


def _stack_heads(q):
    lo = lax.broadcasted_iota(jnp.int32, q.shape, 1) < HEAD_DIM
    zero = jnp.zeros_like(q)
    return jnp.concatenate([jnp.where(lo, q, zero), jnp.where(lo, zero, q)], axis=0)


def _unstack_heads(t):
    lo = lax.broadcasted_iota(jnp.int32, (BLOCK, LANES), 1) < HEAD_DIM
    return jnp.where(lo, t[:BLOCK], t[BLOCK:])


def _split_heads(v):
    lo = lax.broadcasted_iota(jnp.int32, v.shape, 1) < HEAD_DIM
    zero = jnp.zeros_like(v)
    return jnp.where(lo, v, zero), jnp.where(lo, zero, v)


def _band_rows(ref, n, lanes=slice(None)):
    r0 = pl.multiple_of(n * BLOCK, BLOCK)
    p0 = pl.multiple_of(jnp.maximum(n - 1, 0) * BLOCK, BLOCK)
    return jnp.concatenate([ref[pl.ds(p0, BLOCK), lanes], ref[pl.ds(r0, BLOCK), lanes]], axis=0)


def _band_softmax(lhs, k_band, va_band, vb_band, bias, sink=None):
    s = _dot_t(lhs, k_band) + bias
    m = jnp.broadcast_to(jnp.max(s, axis=1, keepdims=True), (2 * BLOCK, LANES))
    if sink is not None:
        m = jnp.maximum(m, sink)
    p = jnp.exp2(s - jnp.concatenate([m, m], axis=1)).astype(BF16)
    p_cat = jnp.concatenate([p[:BLOCK], p[BLOCK:]], axis=1)
    lo = lax.broadcasted_iota(jnp.int32, (2 * BLOCK, LANES), 1) < HEAD_DIM
    ones_a = jnp.where(lo, 1.0, 0.0).astype(BF16)
    ones_b = jnp.where(lo, 0.0, 1.0).astype(BF16)
    rhs = jnp.concatenate([jnp.concatenate([va_band, vb_band], axis=0),
                           jnp.concatenate([ones_a, ones_b], axis=0)], axis=1)
    ext = _dot(p_cat, rhs)
    return ext[:, :LANES], _unstack_heads(m), ext[:, LANES:]


def _band_bias(slopes, window_max, dist_scale, strict):
    i = jnp.arange(BLOCK)[:, None]
    j = jnp.arange(2 * BLOCK)[None, :]
    dist = i + BLOCK - j
    valid = (dist >= 0) & ((dist < window_max) if strict else (dist <= window_max))
    alibi = (slopes[:, None, None] * (dist * dist_scale).astype(F32)[None]) * LOG2E
    h = slopes.shape[0]
    table = jnp.stack([jnp.where(valid[None], -alibi, MASKED),
                       jnp.where((valid & (j >= BLOCK))[None], -alibi, MASKED)], axis=1)
    table = table.reshape(h // 2, 2, 2, BLOCK, 2 * BLOCK).transpose(0, 2, 1, 3, 4)
    return table.reshape(h // 2, 2, 2 * BLOCK, 2 * BLOCK)


def _pair_rows(t):
    h = t.shape[0]
    return t.reshape((h // 2, 2 * t.shape[1]) + t.shape[2:])


def _swa_body(*refs, n_cast):
    ins, cast_in, o_ref, cast_out, (kd_scr, va_scr, vb_scr) = _split_refs(refs, 5, n_cast)
    q_ref, k_ref, v_ref, bias_ref, sink_ref = ins
    _cast_rows(cast_in, cast_out)
    s = q_ref.shape[1]
    c = pl.program_id(1)
    blocks_per_kv = A_Q_HEADS // A_KV_HEADS // 2
    kv_head = c // blocks_per_kv

    @pl.when(c % blocks_per_kv == 0)
    def _():
        lane_head = lax.broadcasted_iota(jnp.int32, (s, LANES), 1) // HEAD_DIM
        sel = lane_head == kv_head
        kf = jnp.where(sel, k_ref[0].astype(F32), 0.0)
        kd_scr[...] = (kf + pltpu.roll(kf, HEAD_DIM, axis=1)).astype(BF16)
        vf = jnp.where(sel, v_ref[0].astype(F32), 0.0)
        vr = pltpu.roll(vf, HEAD_DIM, axis=1)
        va_scr[...] = jnp.where(kv_head == 0, vf, vr).astype(BF16)
        vb_scr[...] = jnp.where(kv_head == 0, vr, vf).astype(BF16)

    def block(n, carry):
        rows = pl.ds(pl.multiple_of(n * BLOCK, BLOCK), BLOCK)
        lhs = _stack_heads(q_ref[0, rows, :])
        first = jnp.where(n == 0, 1, 0)
        num, m, den = _band_softmax(lhs, _band_rows(kd_scr, n), _band_rows(va_scr, n),
                                    _band_rows(vb_scr, n), bias_ref[0, first],
                                    sink=sink_ref[0, :2 * BLOCK])
        den = den + jnp.exp2(sink_ref[0, 2 * BLOCK:] - m)
        o_ref[0, rows, :] = (num / den).astype(BF16)
        return carry

    lax.fori_loop(0, s // BLOCK, block, 0, unroll=TILE_UNROLL)


def _swa_attention(proj, sinks, slopes, cast=()):
    b, s, _ = proj.shape
    n_blocks = A_Q_HEADS // 2
    cast_in, cast_out, cast_shapes = _cast_specs(cast, b * n_blocks,
                                                 lambda i, c: i * n_blocks + c)
    k_col = A_Q_HEADS * HEAD_DIM // LANES
    bias = _band_bias(slopes, BLOCK, 1, strict=True)
    sink2 = sinks.astype(F32) * LOG2E
    stacked = _pair_rows(jnp.broadcast_to(sink2[:, None, None], (A_Q_HEADS, BLOCK, LANES)))
    by_lane = jnp.broadcast_to(jnp.repeat(sink2, HEAD_DIM).reshape(n_blocks, 1, LANES),
                               (n_blocks, BLOCK, LANES))
    sink = jnp.concatenate([stacked, by_lane], axis=1)
    return pl.pallas_call(
        functools.partial(_swa_body, n_cast=len(cast)),
        grid=(b, n_blocks),
        in_specs=[
            pl.BlockSpec((1, s, LANES), lambda i, c: (i, 0, c)),
            pl.BlockSpec((1, s, LANES), lambda i, c: (i, 0, k_col)),
            pl.BlockSpec((1, s, LANES), lambda i, c: (i, 0, k_col + 1)),
            pl.BlockSpec((1, 2, 2 * BLOCK, 2 * BLOCK), lambda i, c: (c, 0, 0, 0)),
            pl.BlockSpec((1, 3 * BLOCK, LANES), lambda i, c: (c, 0, 0)),
        ] + cast_in,
        out_specs=[pl.BlockSpec((1, s, LANES), lambda i, c: (i, 0, c))] + cast_out,
        out_shape=[jax.ShapeDtypeStruct((b, s, n_blocks * LANES), BF16)] + cast_shapes,
        scratch_shapes=[pltpu.VMEM((s, LANES), BF16)] * 3,
        compiler_params=_params("arbitrary", "arbitrary"),
        name="swa_attention",
    )(proj, proj, proj, bias, sink, *[w for w, _ in cast])


def _sb_softplus(z):
    return jnp.maximum(z, 0.0) + jnp.log2(1.0 + jnp.exp2(-jnp.abs(z)))


def _hi_lo(x):
    hi = x.astype(BF16)
    lo = (x - hi.astype(F32)).astype(BF16)
    return jnp.concatenate([hi, lo], axis=1)


def _sb_weighted_values(w, va, vb):
    return _dot(jnp.concatenate([w[:BLOCK], w[BLOCK:]], axis=1),
                jnp.concatenate([va, vb], axis=0))


def _sb_band_scores(lhs, k_band, first_pen, causal):
    z = _dot_t(lhs, k_band)
    z_prev, z_diag = z[:, :BLOCK] + first_pen, z[:, BLOCK:]
    drop = jnp.concatenate([_sb_softplus(z_prev), jnp.where(causal, _sb_softplus(z_diag), 0.0)],
                           axis=1)
    return z_prev, z_diag, _hi_lo(drop)


def _sb_band_weights(z_prev, z_diag, drop_hi_lo, u_band, causal):
    cs = _dot(drop_hi_lo, u_band)
    w = jnp.concatenate([jnp.exp2(z_prev - cs[:, :BLOCK]),
                         jnp.where(causal, jnp.exp2(z_diag - cs[:, BLOCK:]), 0.0)], axis=1)
    return w.astype(BF16), jnp.broadcast_to(cs[:, :1], (2 * BLOCK, LANES))


def _sb_body(*refs, pairs, q_unroll, n_cast):
    ins, cast_in, o_ref, cast_out, scratch = _split_refs(refs, 5, n_cast)
    q_ref, k_ref, v_ref, ub_ref, u_ref = ins
    va_scr, vb_scr, acc_scr, run_scr = scratch
    _cast_rows(cast_in, cast_out)
    s = q_ref.shape[1]
    lo = lax.broadcasted_iota(jnp.int32, (s, pairs * LANES), 1) % LANES < HEAD_DIM
    v_all = v_ref[0]
    va_scr[...] = jnp.where(lo, v_all, jnp.zeros_like(v_all))
    vb_scr[...] = jnp.where(lo, jnp.zeros_like(v_all), v_all)
    causal = (lax.broadcasted_iota(jnp.int32, (2 * BLOCK, BLOCK), 1)
              < lax.broadcasted_iota(jnp.int32, (2 * BLOCK, BLOCK), 0) % BLOCK)
    k2 = k_ref.at[0]
    lanes = [slice(g * LANES, (g + 1) * LANES) for g in range(pairs)]

    def q_group(i, carry):
        blocks = [i * q_unroll + qi for qi in range(q_unroll)]
        tiles = [(qi, n, g) for qi, n in enumerate(blocks) for g in range(pairs)]
        scores = {}
        for qi, n, g in tiles:
            rows = pl.ds(pl.multiple_of(n * BLOCK, BLOCK), BLOCK)
            lhs = _stack_heads(q_ref[0, rows, lanes[g]])
            scores[qi, g] = _sb_band_scores(lhs, _band_rows(k2, n, lanes[g]),
                                            jnp.where(n > 0, 0.0, MASKED), causal)
        weights = {}
        for qi, n, g in tiles:
            weights[qi, g] = _sb_band_weights(*scores[qi, g], ub_ref[...], causal)
        decay = [jnp.float32(jnp.inf)] * q_unroll
        for qi, n, g in tiles:
            w, total = weights[qi, g]
            acc_scr[qi, g] = _sb_weighted_values(w, _band_rows(va_scr, n, lanes[g]),
                                                 _band_rows(vb_scr, n, lanes[g]))
            run_scr[qi, g] = total
            decay[qi] = jnp.minimum(decay[qi], jnp.min(total))

        for qi, n in enumerate(blocks):
            rows = pl.ds(pl.multiple_of(n * BLOCK, BLOCK), BLOCK)

            def more(st):
                j, least = st
                return jnp.logical_and(j >= 0, least < SB_DECAY_DONE)

            def key_block(st, qi=qi, rows=rows):
                j, _ = st
                keys = pl.ds(pl.multiple_of(j * BLOCK, BLOCK), BLOCK)
                zs = [_dot_t(_stack_heads(q_ref[0, rows, lanes[g]]), k_ref[0, keys, lanes[g]])
                      for g in range(pairs)]
                cs = [_dot(_hi_lo(_sb_softplus(z)), u_ref[...]) for z in zs]
                least = jnp.float32(jnp.inf)
                for g in range(pairs):
                    run = run_scr[qi, g]
                    w = jnp.exp2(zs[g] - cs[g][:, :BLOCK] - run).astype(BF16)
                    acc_scr[qi, g] += _sb_weighted_values(w, va_scr[keys, lanes[g]],
                                                          vb_scr[keys, lanes[g]])
                    run = run + cs[g][:, BLOCK:]
                    run_scr[qi, g] = run
                    least = jnp.minimum(least, jnp.min(run))
                return j - 1, least

            lax.while_loop(more, key_block, (n - 2, decay[qi]))
            for g in range(pairs):
                o_ref[0, rows, lanes[g]] = acc_scr[qi, g].astype(BF16)
        return carry

    lax.fori_loop(0, s // BLOCK // q_unroll, q_group, 0)


def _sb_attention(proj, q_col, k_col, v_col, pairs, q_unroll, cast=()):
    b, s, _ = proj.shape
    width = pairs * LANES
    n_steps = B_HEADS // 2 // pairs
    cast_in, cast_out, cast_shapes = _cast_specs(cast, b * n_steps, lambda i, c: i * n_steps + c)
    assert q_col % pairs == 0 and k_col % pairs == 0 and v_col % pairs == 0
    assert (s // BLOCK) % q_unroll == 0

    def suffix(n):
        kk = jnp.arange(n)
        return (kk[:, None] >= kk[None, :]).astype(BF16)

    u_band = jnp.tile(suffix(2 * BLOCK), (2, 1))
    u_block = jnp.tile(jnp.concatenate([suffix(BLOCK), jnp.ones((BLOCK, BLOCK), BF16)], axis=1),
                       (2, 1))
    return pl.pallas_call(
        functools.partial(_sb_body, pairs=pairs, q_unroll=q_unroll, n_cast=len(cast)),
        grid=(b, n_steps),
        in_specs=[
            pl.BlockSpec((1, s, width), lambda i, c: (i, 0, q_col // pairs + c)),
            pl.BlockSpec((1, s, width), lambda i, c: (i, 0, k_col // pairs + c)),
            pl.BlockSpec((1, s, width), lambda i, c: (i, 0, v_col // pairs + c)),
            pl.BlockSpec((4 * BLOCK, 2 * BLOCK), lambda i, c: (0, 0)),
            pl.BlockSpec((2 * BLOCK, 2 * BLOCK), lambda i, c: (0, 0)),
        ] + cast_in,
        out_specs=[pl.BlockSpec((1, s, width), lambda i, c: (i, 0, c))] + cast_out,
        out_shape=[jax.ShapeDtypeStruct((b, s, B_HEADS * HEAD_DIM), BF16)] + cast_shapes,
        scratch_shapes=[pltpu.VMEM((s, width), BF16), pltpu.VMEM((s, width), BF16),
                        pltpu.VMEM((q_unroll, pairs, BLOCK, LANES), F32),
                        pltpu.VMEM((q_unroll, pairs, 2 * BLOCK, LANES), F32)],
        compiler_params=_params("arbitrary", "arbitrary"),
        name="stick_breaking",
    )(proj, proj, proj, u_band, u_block, *[w for w, _ in cast])


RESIDUE_STEP = 4


def _to_residue_major(src, dst, seg):
    part = seg // RESIDUE_STEP
    for base in range(0, src.shape[0], seg):
        for r in range(RESIDUE_STEP):
            dst[base + r * part:base + (r + 1) * part, :] = (
                src[pl.ds(base + r, part, stride=RESIDUE_STEP), :])


def _from_residue_major(src, dst, seg):
    part = seg // RESIDUE_STEP
    for base in range(0, src.shape[0], seg):
        for r in range(RESIDUE_STEP):
            dst[pl.ds(base + r, part, stride=RESIDUE_STEP), :] = (
                src[base + r * part:base + (r + 1) * part, :])


def _dil_body(*refs, n_cast):
    ins, cast_in, o_ref, cast_out, scratch = _split_refs(refs, 4, n_cast)
    q_ref, k_ref, v_ref, bias_ref = ins
    tmp_a, tmp_b, q4, k4, q16, k16, va1, vb1, va4, vb4, va16, vb16, st_a, st_b = scratch
    _cast_rows(cast_in, cast_out)
    s = q_ref.shape[1]
    va1[...], vb1[...] = _split_heads(v_ref[0])
    for src, d4, d16 in ((q_ref, (q4,), (q16,)), (k_ref, (k4,), (k16,)),
                         (v_ref, (va4, vb4), (va16, vb16))):
        tmp_a[...] = src[0].astype(F32)
        _to_residue_major(tmp_a, tmp_b, s)
        _to_residue_major(tmp_b, tmp_a, s // RESIDUE_STEP)
        for dsts, tmp in ((d4, tmp_b), (d16, tmp_a)):
            vals = tmp[...].astype(BF16)
            if len(dsts) == 1:
                dsts[0][...] = vals
            else:
                dsts[0][...], dsts[1][...] = _split_heads(vals)

    layouts = ((q_ref.at[0], k_ref.at[0], va1, vb1), (q4, k4, va4, vb4), (q16, k16, va16, vb16))
    state, spare = st_a, st_b
    for step, branch in enumerate(reversed(range(len(C_PAIRS)))):
        qb, kb, va, vb = layouts[branch]
        class_blocks = s // C_PAIRS[branch][1] // BLOCK
        if step > 0:
            for a in range(3):
                _from_residue_major(state.at[a], spare.at[a],
                                    s // RESIDUE_STEP if step == 1 else s)
            state, spare = spare, state

        def tile(t, carry, step=step, branch=branch, qb=qb, kb=kb, va=va, vb=vb, state=state,
                 class_blocks=class_blocks):
            rows = pl.ds(pl.multiple_of(t * BLOCK, BLOCK), BLOCK)
            lhs = _stack_heads(qb[rows, :])
            first = jnp.where(t % class_blocks == 0, 1, 0)
            num, m, den = _band_softmax(lhs, _band_rows(kb, t), _band_rows(va, t),
                                        _band_rows(vb, t), bias_ref[branch, 0, first])
            if step > 0:
                m_old = state[0, rows, :]
                m_new = jnp.maximum(m_old, m)
                a_old = jnp.exp2(m_old - m_new)
                a_cur = jnp.exp2(m - m_new)
                num = a_old * state[2, rows, :] + a_cur * num
                den = a_old * state[1, rows, :] + a_cur * den
                m = m_new
            if branch == 0:
                o_ref[0, rows, :] = (num / den).astype(BF16)
            else:
                state[0, rows, :] = m
                state[1, rows, :] = den
                state[2, rows, :] = num
            return carry

        lax.fori_loop(0, s // BLOCK, tile, 0, unroll=DILATED_UNROLL)


def _dilated_attention(proj, slopes, cast=()):
    b, s, _ = proj.shape
    n_blocks = C_HEADS // 2
    cast_in, cast_out, cast_shapes = _cast_specs(cast, b * n_blocks,
                                                 lambda i, c: i * n_blocks + c)
    assert C_PAIRS[0][1] == 1 and C_PAIRS[1][1] == RESIDUE_STEP
    assert C_PAIRS[2][1] == RESIDUE_STEP ** 2
    bias = jnp.stack([_band_bias(slopes, w // d, d, strict=False)
                      for (w, d) in C_PAIRS])
    scr = ([pltpu.VMEM((s, LANES), F32)] * 2 + [pltpu.VMEM((s, LANES), BF16)] * 10
           + [pltpu.VMEM((3, s, LANES), F32)] * 2)
    return pl.pallas_call(
        functools.partial(_dil_body, n_cast=len(cast)),
        grid=(b, n_blocks),
        in_specs=[
            pl.BlockSpec((1, s, LANES), lambda i, c: (i, 0, c)),
            pl.BlockSpec((1, s, LANES), lambda i, c: (i, 0, n_blocks + c)),
            pl.BlockSpec((1, s, LANES), lambda i, c: (i, 0, 2 * n_blocks + c)),
            pl.BlockSpec((len(C_PAIRS), 1, 2, 2 * BLOCK, 2 * BLOCK),
                         lambda i, c: (0, c, 0, 0, 0)),
        ] + cast_in,
        out_specs=[pl.BlockSpec((1, s, LANES), lambda i, c: (i, 0, c))] + cast_out,
        out_shape=[jax.ShapeDtypeStruct((b, s, n_blocks * LANES), BF16)] + cast_shapes,
        scratch_shapes=scr,
        compiler_params=_params("arbitrary", "arbitrary"),
        name="dilated_mixture",
    )(proj, proj, proj, bias, *[w for w, _ in cast])


def _alibi_slopes(n):
    return jnp.exp2(-8.0 * jnp.arange(1, n + 1, dtype=F32) / n)


def kernel(x, attn_norm, ffn_norm, even_w_in, even_q_norm, even_k_norm, even_sinks, even_w_out,
           odd_w_in, odd_q_norm, odd_k_norm, odd_w_out, ffn_w_gate, ffn_w_up, ffn_w_down):
    b, s, d = x.shape
    depth = attn_norm.shape[0]
    scale = HEAD_DIM ** -0.5
    scale2 = scale * LOG2E
    slopes_a = _alibi_slopes(A_Q_HEADS)
    slopes_c = _alibi_slopes(C_HEADS)
    qa, kva, hb = A_Q_HEADS * HEAD_DIM, A_KV_HEADS * HEAD_DIM, B_HEADS * HEAD_DIM
    hc = C_HEADS * HEAD_DIM
    ones = lambda n: jnp.ones((n,), F32)
    zeros = lambda n: jnp.zeros((n,), F32)

    ffn_cast = lambda i: [(ffn_w_gate, i), (ffn_w_up, i), (ffn_w_down, i)]
    w_in = even_w_in[0].astype(BF16)

    x2 = x.reshape(b * s, d)
    for i in range(depth):
        j = i // 2
        if i % 2 == 0:
            colgain = jnp.concatenate([
                jnp.tile(even_q_norm[j].astype(F32), A_Q_HEADS) * scale2,
                jnp.tile(even_k_norm[j].astype(F32), A_KV_HEADS),
                ones(kva), ones(hb) * scale2, ones(hb), ones(hb)])
            colflag = jnp.concatenate([ones(qa + kva), zeros(kva + 3 * hb)])
            proj = _norm_proj(x2, attn_norm[i], w_in, colgain, colflag,
                              qa + kva, PROJ_ROWS, PROJ_CHUNK_EVEN).reshape(b, s, -1)
            oa, w_gate, w_up, w_down = _swa_attention(proj, even_sinks[j], slopes_a,
                                                      cast=ffn_cast(i))
            qb_col = (qa + 2 * kva) // LANES
            ob, w_out, w_in = _sb_attention(
                proj, qb_col, qb_col + hb // LANES, qb_col + 2 * hb // LANES, SB_PAIRS, SB_Q_UNROLL,
                cast=[(even_w_out, j), (odd_w_in, j)])
            acts = [oa.reshape(b * s, qa), ob.reshape(b * s, hb)]
        else:
            colgain = jnp.concatenate([
                jnp.tile(odd_q_norm[j].astype(F32), C_HEADS) * scale2,
                jnp.tile(odd_k_norm[j].astype(F32), C_HEADS), ones(hc)])
            colflag = jnp.concatenate([ones(2 * hc), zeros(hc)])
            proj = _norm_proj(x2, attn_norm[i], w_in, colgain, colflag,
                              2 * hc, PROJ_ROWS, PROJ_CHUNK_ODD).reshape(b, s, -1)
            cast = ffn_cast(i) + [(odd_w_out, j)]
            if i + 1 < depth:
                cast.append((even_w_in, j + 1))
            oc, w_gate, w_up, w_down, w_out, *rest = _dilated_attention(proj, slopes_c, cast=cast)
            w_in = rest[0] if rest else None
            acts = [oc.reshape(b * s, hc)]
        x2 = _out_proj(x2, acts, w_out, OUT_ROWS)
        x2 = _ffn(x2, ffn_norm[i], w_gate, w_up, w_down, FFN_ROWS, FFN_COLS)
    return x2.reshape(b, s, d)
```

```python
import functools

import jax
import jax.numpy as jnp
from jax import lax
from jax.experimental import pallas as pl
from jax.experimental.pallas import tpu as pltpu

HEAD_DIM = 64
LANES = 128
BLOCK = 128
A_Q_HEADS = 16
A_KV_HEADS = 2
B_HEADS = 16
C_HEADS = 32
C_PAIRS = ((128, 1), (512, 4), (2048, 16))
RMS_EPS = 1e-6
MASKED = -1e30
LOG2E = 1.4426950408889634
SB_DECAY_DONE = 127.0
VMEM_LIMIT = 56 * 1024 * 1024
MXU_WIDTH = 256
BF16_SUBLANES = 16
TILE_UNROLL = 16
DILATED_UNROLL = 32
SB_PAIRS = 2
SB_Q_UNROLL = 4
PROJ_ROWS = 512
PROJ_CHUNK_EVEN = 1024
PROJ_CHUNK_ODD = 512
OUT_ROWS = 512
FFN_ROWS = 1024
FFN_COLS = 512

F32 = jnp.float32
BF16 = jnp.bfloat16


def _dot(a, b):
    return jnp.dot(a, b, preferred_element_type=F32)


def _dot_t(a, b):
    return lax.dot_general(a, b, (((1,), (1,)), ((), ())), preferred_element_type=F32)


def _params(*sem):
    return pltpu.CompilerParams(dimension_semantics=sem, vmem_limit_bytes=VMEM_LIMIT)


def _cast_specs(passengers, n_steps, step_of):
    in_specs, out_specs, out_shapes = [], [], []
    for w, layer in passengers:
        _, rows, cols = w.shape
        n_col = next(c for c in (1, 2, 4, 8) if n_steps % c == 0
                     and rows % (BF16_SUBLANES * (n_steps // c)) == 0
                     and cols % (LANES * c) == 0)
        slab = (rows // (n_steps // n_col), cols // n_col)
        in_specs.append(pl.BlockSpec(
            (1,) + slab,
            lambda *g, layer=layer, n_col=n_col: (layer, step_of(*g) // n_col, step_of(*g) % n_col)))
        out_specs.append(pl.BlockSpec(
            slab, lambda *g, n_col=n_col: (step_of(*g) // n_col, step_of(*g) % n_col)))
        out_shapes.append(jax.ShapeDtypeStruct((rows, cols), BF16))
    return in_specs, out_specs, out_shapes


def _cast_rows(src_refs, dst_refs):
    for src, dst in zip(src_refs, dst_refs):
        dst[...] = src[0].astype(BF16)


def _split_refs(refs, n_in, n_cast):
    a, b, c = n_in + n_cast, n_in + n_cast + 1, n_in + 2 * n_cast + 1
    return refs[:n_in], refs[n_in:a], refs[a], refs[b:c], refs[c:]


def _proj_body(x_ref, g_ref, w_ref, cg_ref, cf_ref, s_ref, o_ref, *, n_norm_cols, chunk):
    x = x_ref[...]
    inv = lax.rsqrt(jnp.mean(x * x, axis=-1, keepdims=True) + RMS_EPS)
    h = ((x * inv) * g_ref[...]).astype(BF16)
    n_out = o_ref.shape[1]
    for c0 in range(0, n_out, chunk):
        width = min(chunk, n_out - c0)
        y = _dot(h, w_ref[:, c0:c0 + width])
        for c in range(c0, c0 + width, MXU_WIDTH):
            sl = slice(c, c + MXU_WIDTH)
            yc = y[:, c - c0:c - c0 + MXU_WIDTH]
            cg = cg_ref[:, sl]
            if c < n_norm_cols:
                ss = _dot((yc * yc).astype(BF16), s_ref[...])
                inv_h = lax.rsqrt(ss * (1.0 / HEAD_DIM) + RMS_EPS)
                scale = jnp.where(cf_ref[:, sl] > 0.0, inv_h * cg, cg)
            else:
                scale = cg
            o_ref[:, sl] = (yc * scale).astype(BF16)


def _norm_proj(x2, gain, w, colgain, colflag, n_norm_cols, tm, chunk):
    n, d = x2.shape
    n_out = w.shape[1]
    assert n % tm == 0 and n_out % MXU_WIDTH == 0 and chunk % MXU_WIDTH == 0
    idx = jnp.arange(MXU_WIDTH) // HEAD_DIM
    seg = (idx[:, None] == idx[None, :]).astype(BF16)
    const = lambda shape: pl.BlockSpec(shape, lambda i: (0, 0), pipeline_mode=pl.Buffered(1))
    return pl.pallas_call(
        functools.partial(_proj_body, n_norm_cols=n_norm_cols, chunk=chunk),
        grid=(n // tm,),
        in_specs=[
            pl.BlockSpec((tm, d), lambda i: (i, 0)),
            const((1, d)),
            const((d, n_out)),
            const((1, n_out)),
            const((1, n_out)),
            const((MXU_WIDTH, MXU_WIDTH)),
        ],
        out_specs=pl.BlockSpec((tm, n_out), lambda i: (i, 0)),
        out_shape=jax.ShapeDtypeStruct((n, n_out), BF16),
        compiler_params=_params("parallel"),
        name="norm_proj",
    )(x2, gain.reshape(1, d), w, colgain.reshape(1, n_out), colflag.reshape(1, n_out), seg)


def _out_body(*refs, n_pairs):
    x_ref = refs[0]
    o_ref = refs[1 + 2 * n_pairs]
    acc = x_ref[...]
    for p in range(n_pairs):
        acc = acc + _dot(refs[1 + 2 * p][...], refs[2 + 2 * p][...])
    o_ref[...] = acc


def _out_proj(x2, acts, w, tm):
    n, d = x2.shape
    in_specs = [pl.BlockSpec((tm, d), lambda i: (i, 0))]
    args = [x2]
    offset = 0
    for a in acts:
        kk = a.shape[1]
        assert offset % kk == 0
        in_specs.append(pl.BlockSpec((tm, kk), lambda i: (i, 0)))
        in_specs.append(pl.BlockSpec((kk, d), lambda i, r=offset // kk: (r, 0)))
        args += [a, w]
        offset += kk
    assert offset == w.shape[0] and n % tm == 0
    return pl.pallas_call(
        functools.partial(_out_body, n_pairs=len(acts)),
        grid=(n // tm,),
        in_specs=in_specs,
        out_specs=pl.BlockSpec((tm, d), lambda i: (i, 0)),
        out_shape=jax.ShapeDtypeStruct((n, d), F32),
        compiler_params=_params("parallel"),
        name="out_proj",
    )(*args)


def _ffn_body(x_ref, g_ref, wg_ref, wu_ref, wd_ref, o_ref, h_scr):
    j = pl.program_id(1)

    @pl.when(j == 0)
    def _():
        x = x_ref[...]
        inv = lax.rsqrt(jnp.mean(x * x, axis=-1, keepdims=True) + RMS_EPS)
        h_scr[...] = ((x * inv) * g_ref[...]).astype(BF16)
        o_ref[...] = x

    h = h_scr[...]
    gate = _dot(h, wg_ref[...])
    up = _dot(h, wu_ref[...])
    act = (gate * (1.0 / (1.0 + jnp.exp(-gate)))) * up
    o_ref[...] += _dot(act.astype(BF16), wd_ref[...])


def _ffn(x2, gain, wg, wu, wd, tm, tf):
    n, d = x2.shape
    dff = wg.shape[1]
    assert n % tm == 0 and dff % tf == 0
    return pl.pallas_call(
        _ffn_body,
        grid=(n // tm, dff // tf),
        in_specs=[
            pl.BlockSpec((tm, d), lambda i, j: (i, 0)),
            pl.BlockSpec((1, d), lambda i, j: (0, 0)),
            pl.BlockSpec((d, tf), lambda i, j: (0, j)),
            pl.BlockSpec((d, tf), lambda i, j: (0, j)),
            pl.BlockSpec((tf, d), lambda i, j: (j, 0)),
        ],
        out_specs=pl.BlockSpec((tm, d), lambda i, j: (i, 0)),
        out_shape=jax.ShapeDtypeStruct((n, d), F32),
        scratch_shapes=[pltpu.VMEM((tm, d), BF16)],
        compiler_params=_params("parallel", "arbitrary"),
        name="ffn",
    )(x2, gain.reshape(1, d), wg, wu, wd)


def _stack_heads(q):
    lo = lax.broadcasted_iota(jnp.int32, q.shape, 1) < HEAD_DIM
    zero = jnp.zeros_like(q)
    return jnp.concatenate([jnp.where(lo, q, zero), jnp.where(lo, zero, q)], axis=0)


def _unstack_heads(t):
    lo = lax.broadcasted_iota(jnp.int32, (BLOCK, LANES), 1) < HEAD_DIM
    return jnp.where(lo, t[:BLOCK], t[BLOCK:])


def _split_heads(v):
    lo = lax.broadcasted_iota(jnp.int32, v.shape, 1) < HEAD_DIM
    zero = jnp.zeros_like(v)
    return jnp.where(lo, v, zero), jnp.where(lo, zero, v)


def _band_rows(ref, n, lanes=slice(None)):
    r0 = pl.multiple_of(n * BLOCK, BLOCK)
    p0 = pl.multiple_of(jnp.maximum(n - 1, 0) * BLOCK, BLOCK)
    return jnp.concatenate([ref[pl.ds(p0, BLOCK), lanes], ref[pl.ds(r0, BLOCK), lanes]], axis=0)


def _band_softmax(lhs, k_band, va_band, vb_band, bias, sink=None):
    s = _dot_t(lhs, k_band) + bias
    m = jnp.broadcast_to(jnp.max(s, axis=1, keepdims=True), (2 * BLOCK, LANES))
    if sink is not None:
        m = jnp.maximum(m, sink)
    p = jnp.exp2(s - jnp.concatenate([m, m], axis=1)).astype(BF16)
    p_cat = jnp.concatenate([p[:BLOCK], p[BLOCK:]], axis=1)
    lo = lax.broadcasted_iota(jnp.int32, (2 * BLOCK, LANES), 1) < HEAD_DIM
    ones_a = jnp.where(lo, 1.0, 0.0).astype(BF16)
    ones_b = jnp.where(lo, 0.0, 1.0).astype(BF16)
    rhs = jnp.concatenate([jnp.concatenate([va_band, vb_band], axis=0),
                           jnp.concatenate([ones_a, ones_b], axis=0)], axis=1)
    ext = _dot(p_cat, rhs)
    return ext[:, :LANES], _unstack_heads(m), ext[:, LANES:]


def _band_bias(slopes, window_max, dist_scale, strict):
    i = jnp.arange(BLOCK)[:, None]
    j = jnp.arange(2 * BLOCK)[None, :]
    dist = i + BLOCK - j
    valid = (dist >= 0) & ((dist < window_max) if strict else (dist <= window_max))
    alibi = (slopes[:, None, None] * (dist * dist_scale).astype(F32)[None]) * LOG2E
    h = slopes.shape[0]
    table = jnp.stack([jnp.where(valid[None], -alibi, MASKED),
                       jnp.where((valid & (j >= BLOCK))[None], -alibi, MASKED)], axis=1)
    table = table.reshape(h // 2, 2, 2, BLOCK, 2 * BLOCK).transpose(0, 2, 1, 3, 4)
    return table.reshape(h // 2, 2, 2 * BLOCK, 2 * BLOCK)


def _pair_rows(t):
    h = t.shape[0]
    return t.reshape((h // 2, 2 * t.shape[1]) + t.shape[2:])


def _swa_body(*refs, n_cast):
    ins, cast_in, o_ref, cast_out, (kd_scr, va_scr, vb_scr) = _split_refs(refs, 5, n_cast)
    q_ref, k_ref, v_ref, bias_ref, sink_ref = ins
    _cast_rows(cast_in, cast_out)
    s = q_ref.shape[1]
    c = pl.program_id(1)
    blocks_per_kv = A_Q_HEADS // A_KV_HEADS // 2
    kv_head = c // blocks_per_kv

    @pl.when(c % blocks_per_kv == 0)
    def _():
        lane_head = lax.broadcasted_iota(jnp.int32, (s, LANES), 1) // HEAD_DIM
        sel = lane_head == kv_head
        kf = jnp.where(sel, k_ref[0].astype(F32), 0.0)
        kd_scr[...] = (kf + pltpu.roll(kf, HEAD_DIM, axis=1)).astype(BF16)
        vf = jnp.where(sel, v_ref[0].astype(F32), 0.0)
        vr = pltpu.roll(vf, HEAD_DIM, axis=1)
        va_scr[...] = jnp.where(kv_head == 0, vf, vr).astype(BF16)
        vb_scr[...] = jnp.where(kv_head == 0, vr, vf).astype(BF16)

    def block(n, carry):
        rows = pl.ds(pl.multiple_of(n * BLOCK, BLOCK), BLOCK)
        lhs = _stack_heads(q_ref[0, rows, :])
        first = jnp.where(n == 0, 1, 0)
        num, m, den = _band_softmax(lhs, _band_rows(kd_scr, n), _band_rows(va_scr, n),
                                    _band_rows(vb_scr, n), bias_ref[0, first],
                                    sink=sink_ref[0, :2 * BLOCK])
        den = den + jnp.exp2(sink_ref[0, 2 * BLOCK:] - m)
        o_ref[0, rows, :] = (num / den).astype(BF16)
        return carry

    lax.fori_loop(0, s // BLOCK, block, 0, unroll=TILE_UNROLL)


def _swa_attention(proj, sinks, slopes, cast=()):
    b, s, _ = proj.shape
    n_blocks = A_Q_HEADS // 2
    cast_in, cast_out, cast_shapes = _cast_specs(cast, b * n_blocks,
                                                 lambda i, c: i * n_blocks + c)
    k_col = A_Q_HEADS * HEAD_DIM // LANES
    bias = _band_bias(slopes, BLOCK, 1, strict=True)
    sink2 = sinks.astype(F32) * LOG2E
    stacked = _pair_rows(jnp.broadcast_to(sink2[:, None, None], (A_Q_HEADS, BLOCK, LANES)))
    by_lane = jnp.broadcast_to(jnp.repeat(sink2, HEAD_DIM).reshape(n_blocks, 1, LANES),
                               (n_blocks, BLOCK, LANES))
    sink = jnp.concatenate([stacked, by_lane], axis=1)
    return pl.pallas_call(
        functools.partial(_swa_body, n_cast=len(cast)),
        grid=(b, n_blocks),
        in_specs=[
            pl.BlockSpec((1, s, LANES), lambda i, c: (i, 0, c)),
            pl.BlockSpec((1, s, LANES), lambda i, c: (i, 0, k_col)),
            pl.BlockSpec((1, s, LANES), lambda i, c: (i, 0, k_col + 1)),
            pl.BlockSpec((1, 2, 2 * BLOCK, 2 * BLOCK), lambda i, c: (c, 0, 0, 0)),
            pl.BlockSpec((1, 3 * BLOCK, LANES), lambda i, c: (c, 0, 0)),
        ] + cast_in,
        out_specs=[pl.BlockSpec((1, s, LANES), lambda i, c: (i, 0, c))] + cast_out,
        out_shape=[jax.ShapeDtypeStruct((b, s, n_blocks * LANES), BF16)] + cast_shapes,
        scratch_shapes=[pltpu.VMEM((s, LANES), BF16)] * 3,
        compiler_params=_params("arbitrary", "arbitrary"),
        name="swa_attention",
    )(proj, proj, proj, bias, sink, *[w for w, _ in cast])


def _sb_softplus(z):
    return jnp.maximum(z, 0.0) + jnp.log2(1.0 + jnp.exp2(-jnp.abs(z)))


def _hi_lo(x):
    hi = x.astype(BF16)
    lo = (x - hi.astype(F32)).astype(BF16)
    return jnp.concatenate([hi, lo], axis=1)


def _sb_weighted_values(w, va, vb):
    return _dot(jnp.concatenate([w[:BLOCK], w[BLOCK:]], axis=1),
                jnp.concatenate([va, vb], axis=0))


def _sb_band_scores(lhs, k_band, first_pen, causal):
    z = _dot_t(lhs, k_band)
    z_prev, z_diag = z[:, :BLOCK] + first_pen, z[:, BLOCK:]
    drop = jnp.concatenate([_sb_softplus(z_prev), jnp.where(causal, _sb_softplus(z_diag), 0.0)],
                           axis=1)
    return z_prev, z_diag, _hi_lo(drop)


def _sb_band_weights(z_prev, z_diag, drop_hi_lo, u_band, causal):
    cs = _dot(drop_hi_lo, u_band)
    w = jnp.concatenate([jnp.exp2(z_prev - cs[:, :BLOCK]),
                         jnp.where(causal, jnp.exp2(z_diag - cs[:, BLOCK:]), 0.0)], axis=1)
    return w.astype(BF16), jnp.broadcast_to(cs[:, :1], (2 * BLOCK, LANES))


def _sb_body(*refs, pairs, q_unroll, n_cast):
    ins, cast_in, o_ref, cast_out, scratch = _split_refs(refs, 5, n_cast)
    q_ref, k_ref, v_ref, ub_ref, u_ref = ins
    va_scr, vb_scr, acc_scr, run_scr = scratch
    _cast_rows(cast_in, cast_out)
    s = q_ref.shape[1]
    lo = lax.broadcasted_iota(jnp.int32, (s, pairs * LANES), 1) % LANES < HEAD_DIM
    v_all = v_ref[0]
    va_scr[...] = jnp.where(lo, v_all, jnp.zeros_like(v_all))
    vb_scr[...] = jnp.where(lo, jnp.zeros_like(v_all), v_all)
    causal = (lax.broadcasted_iota(jnp.int32, (2 * BLOCK, BLOCK), 1)
              < lax.broadcasted_iota(jnp.int32, (2 * BLOCK, BLOCK), 0) % BLOCK)
    k2 = k_ref.at[0]
    lanes = [slice(g * LANES, (g + 1) * LANES) for g in range(pairs)]

    def q_group(i, carry):
        blocks = [i * q_unroll + qi for qi in range(q_unroll)]
        tiles = [(qi, n, g) for qi, n in enumerate(blocks) for g in range(pairs)]
        scores = {}
        for qi, n, g in tiles:
            rows = pl.ds(pl.multiple_of(n * BLOCK, BLOCK), BLOCK)
            lhs = _stack_heads(q_ref[0, rows, lanes[g]])
            scores[qi, g] = _sb_band_scores(lhs, _band_rows(k2, n, lanes[g]),
                                            jnp.where(n > 0, 0.0, MASKED), causal)
        weights = {}
        for qi, n, g in tiles:
            weights[qi, g] = _sb_band_weights(*scores[qi, g], ub_ref[...], causal)
        decay = [jnp.float32(jnp.inf)] * q_unroll
        for qi, n, g in tiles:
            w, total = weights[qi, g]
            acc_scr[qi, g] = _sb_weighted_values(w, _band_rows(va_scr, n, lanes[g]),
                                                 _band_rows(vb_scr, n, lanes[g]))
            run_scr[qi, g] = total
            decay[qi] = jnp.minimum(decay[qi], jnp.min(total))

        for qi, n in enumerate(blocks):
            rows = pl.ds(pl.multiple_of(n * BLOCK, BLOCK), BLOCK)

            def more(st):
                j, least = st
                return jnp.logical_and(j >= 0, least < SB_DECAY_DONE)

            def key_block(st, qi=qi, rows=rows):
                j, _ = st
                keys = pl.ds(pl.multiple_of(j * BLOCK, BLOCK), BLOCK)
                zs = [_dot_t(_stack_heads(q_ref[0, rows, lanes[g]]), k_ref[0, keys, lanes[g]])
                      for g in range(pairs)]
                cs = [_dot(_hi_lo(_sb_softplus(z)), u_ref[...]) for z in zs]
                least = jnp.float32(jnp.inf)
                for g in range(pairs):
                    run = run_scr[qi, g]
                    w = jnp.exp2(zs[g] - cs[g][:, :BLOCK] - run).astype(BF16)
                    acc_scr[qi, g] += _sb_weighted_values(w, va_scr[keys, lanes[g]],
                                                          vb_scr[keys, lanes[g]])
                    run = run + cs[g][:, BLOCK:]
                    run_scr[qi, g] = run
                    least = jnp.minimum(least, jnp.min(run))
                return j - 1, least

            lax.while_loop(more, key_block, (n - 2, decay[qi]))
            for g in range(pairs):
                o_ref[0, rows, lanes[g]] = acc_scr[qi, g].astype(BF16)
        return carry

    lax.fori_loop(0, s // BLOCK // q_unroll, q_group, 0)


def _sb_attention(proj, q_col, k_col, v_col, pairs, q_unroll, cast=()):
    b, s, _ = proj.shape
    width = pairs * LANES
    n_steps = B_HEADS // 2 // pairs
    cast_in, cast_out, cast_shapes = _cast_specs(cast, b * n_steps, lambda i, c: i * n_steps + c)
    assert q_col % pairs == 0 and k_col % pairs == 0 and v_col % pairs == 0
    assert (s // BLOCK) % q_unroll == 0

    def suffix(n):
        kk = jnp.arange(n)
        return (kk[:, None] >= kk[None, :]).astype(BF16)

    u_band = jnp.tile(suffix(2 * BLOCK), (2, 1))
    u_block = jnp.tile(jnp.concatenate([suffix(BLOCK), jnp.ones((BLOCK, BLOCK), BF16)], axis=1),
                       (2, 1))
    return pl.pallas_call(
        functools.partial(_sb_body, pairs=pairs, q_unroll=q_unroll, n_cast=len(cast)),
        grid=(b, n_steps),
        in_specs=[
            pl.BlockSpec((1, s, width), lambda i, c: (i, 0, q_col // pairs + c)),
            pl.BlockSpec((1, s, width), lambda i, c: (i, 0, k_col // pairs + c)),
            pl.BlockSpec((1, s, width), lambda i, c: (i, 0, v_col // pairs + c)),
            pl.BlockSpec((4 * BLOCK, 2 * BLOCK), lambda i, c: (0, 0)),
            pl.BlockSpec((2 * BLOCK, 2 * BLOCK), lambda i, c: (0, 0)),
        ] + cast_in,
        out_specs=[pl.BlockSpec((1, s, width), lambda i, c: (i, 0, c))] + cast_out,
        out_shape=[jax.ShapeDtypeStruct((b, s, B_HEADS * HEAD_DIM), BF16)] + cast_shapes,
        scratch_shapes=[pltpu.VMEM((s, width), BF16), pltpu.VMEM((s, width), BF16),
                        pltpu.VMEM((q_unroll, pairs, BLOCK, LANES), F32),
                        pltpu.VMEM((q_unroll, pairs, 2 * BLOCK, LANES), F32)],
        compiler_params=_params("arbitrary", "arbitrary"),
        name="stick_breaking",
    )(proj, proj, proj, u_band, u_block, *[w for w, _ in cast])


RESIDUE_STEP = 4


def _to_residue_major(src, dst, seg):
    part = seg // RESIDUE_STEP
    for base in range(0, src.shape[0], seg):
        for r in range(RESIDUE_STEP):
            dst[base + r * part:base + (r + 1) * part, :] = (
                src[pl.ds(base + r, part, stride=RESIDUE_STEP), :])


def _from_residue_major(src, dst, seg):
    part = seg // RESIDUE_STEP
    for base in range(0, src.shape[0], seg):
        for r in range(RESIDUE_STEP):
            dst[pl.ds(base + r, part, stride=RESIDUE_STEP), :] = (
                src[base + r * part:base + (r + 1) * part, :])


def _dil_body(*refs, n_cast):
    ins, cast_in, o_ref, cast_out, scratch = _split_refs(refs, 4, n_cast)
    q_ref, k_ref, v_ref, bias_ref = ins
    tmp_a, tmp_b, q4, k4, q16, k16, va1, vb1, va4, vb4, va16, vb16, st_a, st_b = scratch
    _cast_rows(cast_in, cast_out)
    s = q_ref.shape[1]
    va1[...], vb1[...] = _split_heads(v_ref[0])
    for src, d4, d16 in ((q_ref, (q4,), (q16,)), (k_ref, (k4,), (k16,)),
                         (v_ref, (va4, vb4), (va16, vb16))):
        tmp_a[...] = src[0].astype(F32)
        _to_residue_major(tmp_a, tmp_b, s)
        _to_residue_major(tmp_b, tmp_a, s // RESIDUE_STEP)
        for dsts, tmp in ((d4, tmp_b), (d16, tmp_a)):
            vals = tmp[...].astype(BF16)
            if len(dsts) == 1:
                dsts[0][...] = vals
            else:
                dsts[0][...], dsts[1][...] = _split_heads(vals)

    layouts = ((q_ref.at[0], k_ref.at[0], va1, vb1), (q4, k4, va4, vb4), (q16, k16, va16, vb16))
    state, spare = st_a, st_b
    for step, branch in enumerate(reversed(range(len(C_PAIRS)))):
        qb, kb, va, vb = layouts[branch]
        class_blocks = s // C_PAIRS[branch][1] // BLOCK
        if step > 0:
            for a in range(3):
                _from_residue_major(state.at[a], spare.at[a],
                                    s // RESIDUE_STEP if step == 1 else s)
            state, spare = spare, state

        def tile(t, carry, step=step, branch=branch, qb=qb, kb=kb, va=va, vb=vb, state=state,
                 class_blocks=class_blocks):
            rows = pl.ds(pl.multiple_of(t * BLOCK, BLOCK), BLOCK)
            lhs = _stack_heads(qb[rows, :])
            first = jnp.where(t % class_blocks == 0, 1, 0)
            num, m, den = _band_softmax(lhs, _band_rows(kb, t), _band_rows(va, t),
                                        _band_rows(vb, t), bias_ref[branch, 0, first])
            if step > 0:
                m_old = state[0, rows, :]
                m_new = jnp.maximum(m_old, m)
                a_old = jnp.exp2(m_old - m_new)
                a_cur = jnp.exp2(m - m_new)
                num = a_old * state[2, rows, :] + a_cur * num
                den = a_old * state[1, rows, :] + a_cur * den
                m = m_new
            if branch == 0:
                o_ref[0, rows, :] = (num / den).astype(BF16)
            else:
                state[0, rows, :] = m
                state[1, rows, :] = den
                state[2, rows, :] = num
            return carry

        lax.fori_loop(0, s // BLOCK, tile, 0, unroll=DILATED_UNROLL)


def _dilated_attention(proj, slopes, cast=()):
    b, s, _ = proj.shape
    n_blocks = C_HEADS // 2
    cast_in, cast_out, cast_shapes = _cast_specs(cast, b * n_blocks,
                                                 lambda i, c: i * n_blocks + c)
    assert C_PAIRS[0][1] == 1 and C_PAIRS[1][1] == RESIDUE_STEP
    assert C_PAIRS[2][1] == RESIDUE_STEP ** 2
    bias = jnp.stack([_band_bias(slopes, w // d, d, strict=False)
                      for (w, d) in C_PAIRS])
    scr = ([pltpu.VMEM((s, LANES), F32)] * 2 + [pltpu.VMEM((s, LANES), BF16)] * 10
           + [pltpu.VMEM((3, s, LANES), F32)] * 2)
    return pl.pallas_call(
        functools.partial(_dil_body, n_cast=len(cast)),
        grid=(b, n_blocks),
        in_specs=[
            pl.BlockSpec((1, s, LANES), lambda i, c: (i, 0, c)),
            pl.BlockSpec((1, s, LANES), lambda i, c: (i, 0, n_blocks + c)),
            pl.BlockSpec((1, s, LANES), lambda i, c: (i, 0, 2 * n_blocks + c)),
            pl.BlockSpec((len(C_PAIRS), 1, 2, 2 * BLOCK, 2 * BLOCK),
                         lambda i, c: (0, c, 0, 0, 0)),
        ] + cast_in,
        out_specs=[pl.BlockSpec((1, s, LANES), lambda i, c: (i, 0, c))] + cast_out,
        out_shape=[jax.ShapeDtypeStruct((b, s, n_blocks * LANES), BF16)] + cast_shapes,
        scratch_shapes=scr,
        compiler_params=_params("arbitrary", "arbitrary"),
        name="dilated_mixture",
    )(proj, proj, proj, bias, *[w for w, _ in cast])


def _alibi_slopes(n):
    return jnp.exp2(-8.0 * jnp.arange(1, n + 1, dtype=F32) / n)


def kernel(x, attn_norm, ffn_norm, even_w_in, even_q_norm, even_k_norm, even_sinks, even_w_out,
           odd_w_in, odd_q_norm, odd_k_norm, odd_w_out, ffn_w_gate, ffn_w_up, ffn_w_down):
    b, s, d = x.shape
    depth = attn_norm.shape[0]
    scale = HEAD_DIM ** -0.5
    scale2 = scale * LOG2E
    slopes_a = _alibi_slopes(A_Q_HEADS)
    slopes_c = _alibi_slopes(C_HEADS)
    qa, kva, hb = A_Q_HEADS * HEAD_DIM, A_KV_HEADS * HEAD_DIM, B_HEADS * HEAD_DIM
    hc = C_HEADS * HEAD_DIM
    ones = lambda n: jnp.ones((n,), F32)
    zeros = lambda n: jnp.zeros((n,), F32)

    ffn_cast = lambda i: [(ffn_w_gate, i), (ffn_w_up, i), (ffn_w_down, i)]
    w_in = even_w_in[0].astype(BF16)

    x2 = x.reshape(b * s, d)
    for i in range(depth):
        j = i // 2
        if i % 2 == 0:
            colgain = jnp.concatenate([
                jnp.tile(even_q_norm[j].astype(F32), A_Q_HEADS) * scale2,
                jnp.tile(even_k_norm[j].astype(F32), A_KV_HEADS),
                ones(kva), ones(hb) * scale2, ones(hb), ones(hb)])
            colflag = jnp.concatenate([ones(qa + kva), zeros(kva + 3 * hb)])
            proj = _norm_proj(x2, attn_norm[i], w_in, colgain, colflag,
                              qa + kva, PROJ_ROWS, PROJ_CHUNK_EVEN).reshape(b, s, -1)
            oa, w_gate = _swa_attention(proj, even_sinks[j], slopes_a, cast=ffn_cast(i)[:1])
            qb_col = (qa + 2 * kva) // LANES
            ob, w_up, w_down, w_out, w_in = _sb_attention(
                proj, qb_col, qb_col + hb // LANES, qb_col + 2 * hb // LANES, SB_PAIRS, SB_Q_UNROLL,
                cast=ffn_cast(i)[1:] + [(even_w_out, j), (odd_w_in, j)])
            acts = [oa.reshape(b * s, qa), ob.reshape(b * s, hb)]
        else:
            colgain = jnp.concatenate([
                jnp.tile(odd_q_norm[j].astype(F32), C_HEADS) * scale2,
                jnp.tile(odd_k_norm[j].astype(F32), C_HEADS), ones(hc)])
            colflag = jnp.concatenate([ones(2 * hc), zeros(hc)])
            proj = _norm_proj(x2, attn_norm[i], w_in, colgain, colflag,
                              2 * hc, PROJ_ROWS, PROJ_CHUNK_ODD).reshape(b, s, -1)
            cast = ffn_cast(i) + [(odd_w_out, j)]
            if i + 1 < depth:
                cast.append((even_w_in, j + 1))
            oc, w_gate, w_up, w_down, w_out, *rest = _dilated_attention(proj, slopes_c, cast=cast)
            w_in = rest[0] if rest else None
            acts = [oc.reshape(b * s, hc)]
        x2 = _out_proj(x2, acts, w_out, OUT_ROWS)
        x2 = _ffn(x2, ffn_norm[i], w_gate, w_up, w_down, FFN_ROWS, FFN_COLS)
    return x2.reshape(b, s, d)
```

```python
import functools

import jax
import jax.numpy as jnp
from jax import lax
from jax.experimental import pallas as pl
from jax.experimental.pallas import tpu as pltpu

HEAD_DIM = 64
LANES = 128
BLOCK = 128
A_Q_HEADS = 16
A_KV_HEADS = 2
B_HEADS = 16
C_HEADS = 32
C_PAIRS = ((128, 1), (512, 4), (2048, 16))
RMS_EPS = 1e-6
MASKED = -1e30
LOG2E = 1.4426950408889634
SB_DECAY_DONE = 127.0
VMEM_LIMIT = 56 * 1024 * 1024
MXU_WIDTH = 256
BF16_SUBLANES = 16
TILE_UNROLL = 32
SB_PAIRS = 2
SB_Q_UNROLL = 4
PROJ_ROWS = 512
PROJ_CHUNK_EVEN = 1024
PROJ_CHUNK_ODD = 512
OUT_ROWS = 512
FFN_ROWS = 1024
FFN_COLS = 512

F32 = jnp.float32
BF16 = jnp.bfloat16


def _dot(a, b):
    return jnp.dot(a, b, preferred_element_type=F32)


def _dot_t(a, b):
    return lax.dot_general(a, b, (((1,), (1,)), ((), ())), preferred_element_type=F32)


def _params(*sem):
    return pltpu.CompilerParams(dimension_semantics=sem, vmem_limit_bytes=VMEM_LIMIT)


def _cast_specs(passengers, n_steps, step_of):
    in_specs, out_specs, out_shapes = [], [], []
    for w, layer in passengers:
        _, rows, cols = w.shape
        n_col = next(c for c in (1, 2, 4, 8) if n_steps % c == 0
                     and rows % (BF16_SUBLANES * (n_steps // c)) == 0
                     and cols % (LANES * c) == 0)
        slab = (rows // (n_steps // n_col), cols // n_col)
        in_specs.append(pl.BlockSpec(
            (1,) + slab,
            lambda *g, layer=layer, n_col=n_col: (layer, step_of(*g) // n_col, step_of(*g) % n_col)))
        out_specs.append(pl.BlockSpec(
            slab, lambda *g, n_col=n_col: (step_of(*g) // n_col, step_of(*g) % n_col)))
        out_shapes.append(jax.ShapeDtypeStruct((rows, cols), BF16))
    return in_specs, out_specs, out_shapes


def _cast_rows(src_refs, dst_refs):
    for src, dst in zip(src_refs, dst_refs):
        dst[...] = src[0].astype(BF16)


def _split_refs(refs, n_in, n_cast):
    a, b, c = n_in + n_cast, n_in + n_cast + 1, n_in + 2 * n_cast + 1
    return refs[:n_in], refs[n_in:a], refs[a], refs[b:c], refs[c:]


def _proj_body(x_ref, g_ref, w_ref, cg_ref, cf_ref, s_ref, o_ref, *, n_norm_cols, chunk):
    x = x_ref[...]
    inv = lax.rsqrt(jnp.mean(x * x, axis=-1, keepdims=True) + RMS_EPS)
    h = ((x * inv) * g_ref[...]).astype(BF16)
    n_out = o_ref.shape[1]
    for c0 in range(0, n_out, chunk):
        width = min(chunk, n_out - c0)
        y = _dot(h, w_ref[:, c0:c0 + width])
        for c in range(c0, c0 + width, MXU_WIDTH):
            sl = slice(c, c + MXU_WIDTH)
            yc = y[:, c - c0:c - c0 + MXU_WIDTH]
            cg = cg_ref[:, sl]
            if c < n_norm_cols:
                ss = _dot((yc * yc).astype(BF16), s_ref[...])
                inv_h = lax.rsqrt(ss * (1.0 / HEAD_DIM) + RMS_EPS)
                scale = jnp.where(cf_ref[:, sl] > 0.0, inv_h * cg, cg)
            else:
                scale = cg
            o_ref[:, sl] = (yc * scale).astype(BF16)


def _norm_proj(x2, gain, w, colgain, colflag, n_norm_cols, tm, chunk):
    n, d = x2.shape
    n_out = w.shape[1]
    assert n % tm == 0 and n_out % MXU_WIDTH == 0 and chunk % MXU_WIDTH == 0
    idx = jnp.arange(MXU_WIDTH) // HEAD_DIM
    seg = (idx[:, None] == idx[None, :]).astype(BF16)
    const = lambda shape: pl.BlockSpec(shape, lambda i: (0, 0), pipeline_mode=pl.Buffered(1))
    return pl.pallas_call(
        functools.partial(_proj_body, n_norm_cols=n_norm_cols, chunk=chunk),
        grid=(n // tm,),
        in_specs=[
            pl.BlockSpec((tm, d), lambda i: (i, 0)),
            const((1, d)),
            const((d, n_out)),
            const((1, n_out)),
            const((1, n_out)),
            const((MXU_WIDTH, MXU_WIDTH)),
        ],
        out_specs=pl.BlockSpec((tm, n_out), lambda i: (i, 0)),
        out_shape=jax.ShapeDtypeStruct((n, n_out), BF16),
        compiler_params=_params("parallel"),
        name="norm_proj",
    )(x2, gain.reshape(1, d), w, colgain.reshape(1, n_out), colflag.reshape(1, n_out), seg)


def _out_body(*refs, n_pairs):
    x_ref = refs[0]
    o_ref = refs[1 + 2 * n_pairs]
    acc = x_ref[...]
    for p in range(n_pairs):
        acc = acc + _dot(refs[1 + 2 * p][...], refs[2 + 2 * p][...])
    o_ref[...] = acc


def _out_proj(x2, acts, w, tm):
    n, d = x2.shape
    in_specs = [pl.BlockSpec((tm, d), lambda i: (i, 0))]
    args = [x2]
    offset = 0
    for a in acts:
        kk = a.shape[1]
        assert offset % kk == 0
        in_specs.append(pl.BlockSpec((tm, kk), lambda i: (i, 0)))
        in_specs.append(pl.BlockSpec((kk, d), lambda i, r=offset // kk: (r, 0)))
        args += [a, w]
        offset += kk
    assert offset == w.shape[0] and n % tm == 0
    return pl.pallas_call(
        functools.partial(_out_body, n_pairs=len(acts)),
        grid=(n // tm,),
        in_specs=in_specs,
        out_specs=pl.BlockSpec((tm, d), lambda i: (i, 0)),
        out_shape=jax.ShapeDtypeStruct((n, d), F32),
        compiler_params=_params("parallel"),
        name="out_proj",
    )(*args)


def _ffn_body(x_ref, g_ref, wg_ref, wu_ref, wd_ref, o_ref, h_scr):
    j = pl.program_id(1)

    @pl.when(j == 0)
    def _():
        x = x_ref[...]
        inv = lax.rsqrt(jnp.mean(x * x, axis=-1, keepdims=True) + RMS_EPS)
        h_scr[...] = ((x * inv) * g_ref[...]).astype(BF16)
        o_ref[...] = x

    h = h_scr[...]
    gate = _dot(h, wg_ref[...])
    up = _dot(h, wu_ref[...])
    act = (gate * (1.0 / (1.0 + jnp.exp(-gate)))) * up
    o_ref[...] += _dot(act.astype(BF16), wd_ref[...])


def _ffn(x2, gain, wg, wu, wd, tm, tf):
    n, d = x2.shape
    dff = wg.shape[1]
    assert n % tm == 0 and dff % tf == 0
    return pl.pallas_call(
        _ffn_body,
        grid=(n // tm, dff // tf),
        in_specs=[
            pl.BlockSpec((tm, d), lambda i, j: (i, 0)),
            pl.BlockSpec((1, d), lambda i, j: (0, 0)),
            pl.BlockSpec((d, tf), lambda i, j: (0, j)),
            pl.BlockSpec((d, tf), lambda i, j: (0, j)),
            pl.BlockSpec((tf, d), lambda i, j: (j, 0)),
        ],
        out_specs=pl.BlockSpec((tm, d), lambda i, j: (i, 0)),
        out_shape=jax.ShapeDtypeStruct((n, d), F32),
        scratch_shapes=[pltpu.VMEM((tm, d), BF16)],
        compiler_params=_params("parallel", "arbitrary"),
        name="ffn",
    )(x2, gain.reshape(1, d), wg, wu, wd)


def _stack_heads(q):
    lo = lax.broadcasted_iota(jnp.int32, q.shape, 1) < HEAD_DIM
    zero = jnp.zeros_like(q)
    return jnp.concatenate([jnp.where(lo, q, zero), jnp.where(lo, zero, q)], axis=0)


def _unstack_heads(t):
    lo = lax.broadcasted_iota(jnp.int32, (BLOCK, LANES), 1) < HEAD_DIM
    return jnp.where(lo, t[:BLOCK], t[BLOCK:])


def _split_heads(v):
    lo = lax.broadcasted_iota(jnp.int32, v.shape, 1) < HEAD_DIM
    zero = jnp.zeros_like(v)
    return jnp.where(lo, v, zero), jnp.where(lo, zero, v)


def _band_rows(ref, n, lanes=slice(None)):
    r0 = pl.multiple_of(n * BLOCK, BLOCK)
    p0 = pl.multiple_of(jnp.maximum(n - 1, 0) * BLOCK, BLOCK)
    return jnp.concatenate([ref[pl.ds(p0, BLOCK), lanes], ref[pl.ds(r0, BLOCK), lanes]], axis=0)


def _band_softmax(lhs, k_band, va_band, vb_band, bias, sink=None):
    s = _dot_t(lhs, k_band) + bias
    m = jnp.broadcast_to(jnp.max(s, axis=1, keepdims=True), (2 * BLOCK, LANES))
    if sink is not None:
        m = jnp.maximum(m, sink)
    p = jnp.exp2(s - jnp.concatenate([m, m], axis=1)).astype(BF16)
    p_cat = jnp.concatenate([p[:BLOCK], p[BLOCK:]], axis=1)
    lo = lax.broadcasted_iota(jnp.int32, (2 * BLOCK, LANES), 1) < HEAD_DIM
    ones_a = jnp.where(lo, 1.0, 0.0).astype(BF16)
    ones_b = jnp.where(lo, 0.0, 1.0).astype(BF16)
    rhs = jnp.concatenate([jnp.concatenate([va_band, vb_band], axis=0),
                           jnp.concatenate([ones_a, ones_b], axis=0)], axis=1)
    ext = _dot(p_cat, rhs)
    return ext[:, :LANES], _unstack_heads(m), ext[:, LANES:]


def _band_bias(slopes, window_max, dist_scale, strict):
    i = jnp.arange(BLOCK)[:, None]
    j = jnp.arange(2 * BLOCK)[None, :]
    dist = i + BLOCK - j
    valid = (dist >= 0) & ((dist < window_max) if strict else (dist <= window_max))
    alibi = (slopes[:, None, None] * (dist * dist_scale).astype(F32)[None]) * LOG2E
    h = slopes.shape[0]
    table = jnp.stack([jnp.where(valid[None], -alibi, MASKED),
                       jnp.where((valid & (j >= BLOCK))[None], -alibi, MASKED)], axis=1)
    table = table.reshape(h // 2, 2, 2, BLOCK, 2 * BLOCK).transpose(0, 2, 1, 3, 4)
    return table.reshape(h // 2, 2, 2 * BLOCK, 2 * BLOCK)


def _pair_rows(t):
    h = t.shape[0]
    return t.reshape((h // 2, 2 * t.shape[1]) + t.shape[2:])


def _swa_body(*refs, n_cast):
    ins, cast_in, o_ref, cast_out, (kd_scr, va_scr, vb_scr) = _split_refs(refs, 5, n_cast)
    q_ref, k_ref, v_ref, bias_ref, sink_ref = ins
    _cast_rows(cast_in, cast_out)
    s = q_ref.shape[1]
    c = pl.program_id(1)
    blocks_per_kv = A_Q_HEADS // A_KV_HEADS // 2
    kv_head = c // blocks_per_kv

    @pl.when(c % blocks_per_kv == 0)
    def _():
        lane_head = lax.broadcasted_iota(jnp.int32, (s, LANES), 1) // HEAD_DIM
        sel = lane_head == kv_head
        kf = jnp.where(sel, k_ref[0].astype(F32), 0.0)
        kd_scr[...] = (kf + pltpu.roll(kf, HEAD_DIM, axis=1)).astype(BF16)
        vf = jnp.where(sel, v_ref[0].astype(F32), 0.0)
        vr = pltpu.roll(vf, HEAD_DIM, axis=1)
        va_scr[...] = jnp.where(kv_head == 0, vf, vr).astype(BF16)
        vb_scr[...] = jnp.where(kv_head == 0, vr, vf).astype(BF16)

    def block(n, carry):
        rows = pl.ds(pl.multiple_of(n * BLOCK, BLOCK), BLOCK)
        lhs = _stack_heads(q_ref[0, rows, :])
        first = jnp.where(n == 0, 1, 0)
        num, m, den = _band_softmax(lhs, _band_rows(kd_scr, n), _band_rows(va_scr, n),
                                    _band_rows(vb_scr, n), bias_ref[0, first],
                                    sink=sink_ref[0, :2 * BLOCK])
        den = den + jnp.exp2(sink_ref[0, 2 * BLOCK:] - m)
        o_ref[0, rows, :] = (num / den).astype(BF16)
        return carry

    lax.fori_loop(0, s // BLOCK, block, 0, unroll=TILE_UNROLL)


def _swa_attention(proj, sinks, slopes, cast=()):
    b, s, _ = proj.shape
    n_blocks = A_Q_HEADS // 2
    cast_in, cast_out, cast_shapes = _cast_specs(cast, b * n_blocks,
                                                 lambda i, c: i * n_blocks + c)
    k_col = A_Q_HEADS * HEAD_DIM // LANES
    bias = _band_bias(slopes, BLOCK, 1, strict=True)
    sink2 = sinks.astype(F32) * LOG2E
    stacked = _pair_rows(jnp.broadcast_to(sink2[:, None, None], (A_Q_HEADS, BLOCK, LANES)))
    by_lane = jnp.broadcast_to(jnp.repeat(sink2, HEAD_DIM).reshape(n_blocks, 1, LANES),
                               (n_blocks, BLOCK, LANES))
    sink = jnp.concatenate([stacked, by_lane], axis=1)
    return pl.pallas_call(
        functools.partial(_swa_body, n_cast=len(cast)),
        grid=(b, n_blocks),
        in_specs=[
            pl.BlockSpec((1, s, LANES), lambda i, c: (i, 0, c)),
            pl.BlockSpec((1, s, LANES), lambda i, c: (i, 0, k_col)),
            pl.BlockSpec((1, s, LANES), lambda i, c: (i, 0, k_col + 1)),
            pl.BlockSpec((1, 2, 2 * BLOCK, 2 * BLOCK), lambda i, c: (c, 0, 0, 0)),
            pl.BlockSpec((1, 3 * BLOCK, LANES), lambda i, c: (c, 0, 0)),
        ] + cast_in,
        out_specs=[pl.BlockSpec((1, s, LANES), lambda i, c: (i, 0, c))] + cast_out,
        out_shape=[jax.ShapeDtypeStruct((b, s, n_blocks * LANES), BF16)] + cast_shapes,
        scratch_shapes=[pltpu.VMEM((s, LANES), BF16)] * 3,
        compiler_params=_params("arbitrary", "arbitrary"),
        name="swa_attention",
    )(proj, proj, proj, bias, sink, *[w for w, _ in cast])


def _sb_softplus(z):
    return jnp.maximum(z, 0.0) + jnp.log2(1.0 + jnp.exp2(-jnp.abs(z)))


def _hi_lo(x):
    hi = x.astype(BF16)
    lo = (x - hi.astype(F32)).astype(BF16)
    return jnp.concatenate([hi, lo], axis=1)


def _sb_weighted_values(w, va, vb):
    return _dot(jnp.concatenate([w[:BLOCK], w[BLOCK:]], axis=1),
                jnp.concatenate([va, vb], axis=0))


def _sb_band_scores(lhs, k_band, first_pen, causal):
    z = _dot_t(lhs, k_band)
    z_prev, z_diag = z[:, :BLOCK] + first_pen, z[:, BLOCK:]
    drop = jnp.concatenate([_sb_softplus(z_prev), jnp.where(causal, _sb_softplus(z_diag), 0.0)],
                           axis=1)
    return z_prev, z_diag, _hi_lo(drop)


def _sb_band_weights(z_prev, z_diag, drop_hi_lo, u_band, causal):
    cs = _dot(drop_hi_lo, u_band)
    w = jnp.concatenate([jnp.exp2(z_prev - cs[:, :BLOCK]),
                         jnp.where(causal, jnp.exp2(z_diag - cs[:, BLOCK:]), 0.0)], axis=1)
    return w.astype(BF16), jnp.broadcast_to(cs[:, :1], (2 * BLOCK, LANES))


def _sb_body(*refs, pairs, q_unroll, n_cast):
    ins, cast_in, o_ref, cast_out, scratch = _split_refs(refs, 5, n_cast)
    q_ref, k_ref, v_ref, ub_ref, u_ref = ins
    va_scr, vb_scr, acc_scr, run_scr = scratch
    _cast_rows(cast_in, cast_out)
    s = q_ref.shape[1]
    lo = lax.broadcasted_iota(jnp.int32, (s, pairs * LANES), 1) % LANES < HEAD_DIM
    v_all = v_ref[0]
    va_scr[...] = jnp.where(lo, v_all, jnp.zeros_like(v_all))
    vb_scr[...] = jnp.where(lo, jnp.zeros_like(v_all), v_all)
    causal = (lax.broadcasted_iota(jnp.int32, (2 * BLOCK, BLOCK), 1)
              < lax.broadcasted_iota(jnp.int32, (2 * BLOCK, BLOCK), 0) % BLOCK)
    k2 = k_ref.at[0]
    lanes = [slice(g * LANES, (g + 1) * LANES) for g in range(pairs)]

    def q_group(i, carry):
        blocks = [i * q_unroll + qi for qi in range(q_unroll)]
        tiles = [(qi, n, g) for qi, n in enumerate(blocks) for g in range(pairs)]
        scores = {}
        for qi, n, g in tiles:
            rows = pl.ds(pl.multiple_of(n * BLOCK, BLOCK), BLOCK)
            lhs = _stack_heads(q_ref[0, rows, lanes[g]])
            scores[qi, g] = _sb_band_scores(lhs, _band_rows(k2, n, lanes[g]),
                                            jnp.where(n > 0, 0.0, MASKED), causal)
        weights = {}
        for qi, n, g in tiles:
            weights[qi, g] = _sb_band_weights(*scores[qi, g], ub_ref[...], causal)
        decay = [jnp.float32(jnp.inf)] * q_unroll
        for qi, n, g in tiles:
            w, total = weights[qi, g]
            acc_scr[qi, g] = _sb_weighted_values(w, _band_rows(va_scr, n, lanes[g]),
                                                 _band_rows(vb_scr, n, lanes[g]))
            run_scr[qi, g] = total
            decay[qi] = jnp.minimum(decay[qi], jnp.min(total))

        for qi, n in enumerate(blocks):
            rows = pl.ds(pl.multiple_of(n * BLOCK, BLOCK), BLOCK)

            def more(st):
                j, least = st
                return jnp.logical_and(j >= 0, least < SB_DECAY_DONE)

            def key_block(st, qi=qi, rows=rows):
                j, _ = st
                keys = pl.ds(pl.multiple_of(j * BLOCK, BLOCK), BLOCK)
                zs = [_dot_t(_stack_heads(q_ref[0, rows, lanes[g]]), k_ref[0, keys, lanes[g]])
                      for g in range(pairs)]
                cs = [_dot(_hi_lo(_sb_softplus(z)), u_ref[...]) for z in zs]
                least = jnp.float32(jnp.inf)
                for g in range(pairs):
                    run = run_scr[qi, g]
                    w = jnp.exp2(zs[g] - cs[g][:, :BLOCK] - run).astype(BF16)
                    acc_scr[qi, g] += _sb_weighted_values(w, va_scr[keys, lanes[g]],
                                                          vb_scr[keys, lanes[g]])
                    run = run + cs[g][:, BLOCK:]
                    run_scr[qi, g] = run
                    least = jnp.minimum(least, jnp.min(run))
                return j - 1, least

            lax.while_loop(more, key_block, (n - 2, decay[qi]))
            for g in range(pairs):
                o_ref[0, rows, lanes[g]] = acc_scr[qi, g].astype(BF16)
        return carry

    lax.fori_loop(0, s // BLOCK // q_unroll, q_group, 0)


def _sb_attention(proj, q_col, k_col, v_col, pairs, q_unroll, cast=()):
    b, s, _ = proj.shape
    width = pairs * LANES
    n_steps = B_HEADS // 2 // pairs
    cast_in, cast_out, cast_shapes = _cast_specs(cast, b * n_steps, lambda i, c: i * n_steps + c)
    assert q_col % pairs == 0 and k_col % pairs == 0 and v_col % pairs == 0
    assert (s // BLOCK) % q_unroll == 0

    def suffix(n):
        kk = jnp.arange(n)
        return (kk[:, None] >= kk[None, :]).astype(BF16)

    u_band = jnp.tile(suffix(2 * BLOCK), (2, 1))
    u_block = jnp.tile(jnp.concatenate([suffix(BLOCK), jnp.ones((BLOCK, BLOCK), BF16)], axis=1),
                       (2, 1))
    return pl.pallas_call(
        functools.partial(_sb_body, pairs=pairs, q_unroll=q_unroll, n_cast=len(cast)),
        grid=(b, n_steps),
        in_specs=[
            pl.BlockSpec((1, s, width), lambda i, c: (i, 0, q_col // pairs + c)),
            pl.BlockSpec((1, s, width), lambda i, c: (i, 0, k_col // pairs + c)),
            pl.BlockSpec((1, s, width), lambda i, c: (i, 0, v_col // pairs + c)),
            pl.BlockSpec((4 * BLOCK, 2 * BLOCK), lambda i, c: (0, 0)),
            pl.BlockSpec((2 * BLOCK, 2 * BLOCK), lambda i, c: (0, 0)),
        ] + cast_in,
        out_specs=[pl.BlockSpec((1, s, width), lambda i, c: (i, 0, c))] + cast_out,
        out_shape=[jax.ShapeDtypeStruct((b, s, B_HEADS * HEAD_DIM), BF16)] + cast_shapes,
        scratch_shapes=[pltpu.VMEM((s, width), BF16), pltpu.VMEM((s, width), BF16),
                        pltpu.VMEM((q_unroll, pairs, BLOCK, LANES), F32),
                        pltpu.VMEM((q_unroll, pairs, 2 * BLOCK, LANES), F32)],
        compiler_params=_params("arbitrary", "arbitrary"),
        name="stick_breaking",
    )(proj, proj, proj, u_band, u_block, *[w for w, _ in cast])


RESIDUE_STEP = 4


def _to_residue_major(src, dst, seg):
    part = seg // RESIDUE_STEP
    for base in range(0, src.shape[0], seg):
        for r in range(RESIDUE_STEP):
            dst[base + r * part:base + (r + 1) * part, :] = (
                src[pl.ds(base + r, part, stride=RESIDUE_STEP), :])


def _from_residue_major(src, dst, seg):
    part = seg // RESIDUE_STEP
    for base in range(0, src.shape[0], seg):
        for r in range(RESIDUE_STEP):
            dst[pl.ds(base + r, part, stride=RESIDUE_STEP), :] = (
                src[base + r * part:base + (r + 1) * part, :])


def _dil_body(*refs, n_cast):
    ins, cast_in, o_ref, cast_out, scratch = _split_refs(refs, 4, n_cast)
    q_ref, k_ref, v_ref, bias_ref = ins
    tmp_a, tmp_b, q4, k4, q16, k16, va1, vb1, va4, vb4, va16, vb16, st_a, st_b = scratch
    _cast_rows(cast_in, cast_out)
    s = q_ref.shape[1]
    va1[...], vb1[...] = _split_heads(v_ref[0])
    for src, d4, d16 in ((q_ref, (q4,), (q16,)), (k_ref, (k4,), (k16,)),
                         (v_ref, (va4, vb4), (va16, vb16))):
        tmp_a[...] = src[0].astype(F32)
        _to_residue_major(tmp_a, tmp_b, s)
        _to_residue_major(tmp_b, tmp_a, s // RESIDUE_STEP)
        for dsts, tmp in ((d4, tmp_b), (d16, tmp_a)):
            vals = tmp[...].astype(BF16)
            if len(dsts) == 1:
                dsts[0][...] = vals
            else:
                dsts[0][...], dsts[1][...] = _split_heads(vals)

    layouts = ((q_ref.at[0], k_ref.at[0], va1, vb1), (q4, k4, va4, vb4), (q16, k16, va16, vb16))
    state, spare = st_a, st_b
    for step, branch in enumerate(reversed(range(len(C_PAIRS)))):
        qb, kb, va, vb = layouts[branch]
        class_blocks = s // C_PAIRS[branch][1] // BLOCK
        if step > 0:
            for a in range(3):
                _from_residue_major(state.at[a], spare.at[a],
                                    s // RESIDUE_STEP if step == 1 else s)
            state, spare = spare, state

        def tile(t, carry, step=step, branch=branch, qb=qb, kb=kb, va=va, vb=vb, state=state,
                 class_blocks=class_blocks):
            rows = pl.ds(pl.multiple_of(t * BLOCK, BLOCK), BLOCK)
            lhs = _stack_heads(qb[rows, :])
            first = jnp.where(t % class_blocks == 0, 1, 0)
            num, m, den = _band_softmax(lhs, _band_rows(kb, t), _band_rows(va, t),
                                        _band_rows(vb, t), bias_ref[branch, 0, first])
            if step > 0:
                m_old = state[0, rows, :]
                m_new = jnp.maximum(m_old, m)
                a_old = jnp.exp2(m_old - m_new)
                a_cur = jnp.exp2(m - m_new)
                num = a_old * state[2, rows, :] + a_cur * num
                den = a_old * state[1, rows, :] + a_cur * den
                m = m_new
            if branch == 0:
                o_ref[0, rows, :] = (num / den).astype(BF16)
            else:
                state[0, rows, :] = m
                state[1, rows, :] = den
                state[2, rows, :] = num
            return carry

        lax.fori_loop(0, s // BLOCK, tile, 0, unroll=TILE_UNROLL)


def _dilated_attention(proj, slopes, cast=()):
    b, s, _ = proj.shape
    n_blocks = C_HEADS // 2
    cast_in, cast_out, cast_shapes = _cast_specs(cast, b * n_blocks,
                                                 lambda i, c: i * n_blocks + c)
    assert C_PAIRS[0][1] == 1 and C_PAIRS[1][1] == RESIDUE_STEP
    assert C_PAIRS[2][1] == RESIDUE_STEP ** 2
    bias = jnp.stack([_band_bias(slopes, w // d, d, strict=False)
                      for (w, d) in C_PAIRS])
    scr = ([pltpu.VMEM((s, LANES), F32)] * 2 + [pltpu.VMEM((s, LANES), BF16)] * 10
           + [pltpu.VMEM((3, s, LANES), F32)] * 2)
    return pl.pallas_call(
        functools.partial(_dil_body, n_cast=len(cast)),
        grid=(b, n_blocks),
        in_specs=[
            pl.BlockSpec((1, s, LANES), lambda i, c: (i, 0, c)),
            pl.BlockSpec((1, s, LANES), lambda i, c: (i, 0, n_blocks + c)),
            pl.BlockSpec((1, s, LANES), lambda i, c: (i, 0, 2 * n_blocks + c)),
            pl.BlockSpec((len(C_PAIRS), 1, 2, 2 * BLOCK, 2 * BLOCK),
                         lambda i, c: (0, c, 0, 0, 0)),
        ] + cast_in,
        out_specs=[pl.BlockSpec((1, s, LANES), lambda i, c: (i, 0, c))] + cast_out,
        out_shape=[jax.ShapeDtypeStruct((b, s, n_blocks * LANES), BF16)] + cast_shapes,
        scratch_shapes=scr,
        compiler_params=_params("arbitrary", "arbitrary"),
        name="dilated_mixture",
    )(proj, proj, proj, bias, *[w for w, _ in cast])


def _alibi_slopes(n):
    return jnp.exp2(-8.0 * jnp.arange(1, n + 1, dtype=F32) / n)


def kernel(x, attn_norm, ffn_norm, even_w_in, even_q_norm, even_k_norm, even_sinks, even_w_out,
           odd_w_in, odd_q_norm, odd_k_norm, odd_w_out, ffn_w_gate, ffn_w_up, ffn_w_down):
    b, s, d = x.shape
    depth = attn_norm.shape[0]
    scale = HEAD_DIM ** -0.5
    scale2 = scale * LOG2E
    slopes_a = _alibi_slopes(A_Q_HEADS)
    slopes_c = _alibi_slopes(C_HEADS)
    qa, kva, hb = A_Q_HEADS * HEAD_DIM, A_KV_HEADS * HEAD_DIM, B_HEADS * HEAD_DIM
    hc = C_HEADS * HEAD_DIM
    ones = lambda n: jnp.ones((n,), F32)
    zeros = lambda n: jnp.zeros((n,), F32)

    ffn_cast = lambda i: [(ffn_w_gate, i), (ffn_w_up, i), (ffn_w_down, i)]
    w_in = even_w_in[0].astype(BF16)

    x2 = x.reshape(b * s, d)
    for i in range(depth):
        j = i // 2
        if i % 2 == 0:
            colgain = jnp.concatenate([
                jnp.tile(even_q_norm[j].astype(F32), A_Q_HEADS) * scale2,
                jnp.tile(even_k_norm[j].astype(F32), A_KV_HEADS),
                ones(kva), ones(hb) * scale2, ones(hb), ones(hb)])
            colflag = jnp.concatenate([ones(qa + kva), zeros(kva + 3 * hb)])
            proj = _norm_proj(x2, attn_norm[i], w_in, colgain, colflag,
                              qa + kva, PROJ_ROWS, PROJ_CHUNK_EVEN).reshape(b, s, -1)
            oa, w_gate = _swa_attention(proj, even_sinks[j], slopes_a, cast=ffn_cast(i)[:1])
            qb_col = (qa + 2 * kva) // LANES
            ob, w_up, w_down, w_out, w_in = _sb_attention(
                proj, qb_col, qb_col + hb // LANES, qb_col + 2 * hb // LANES, SB_PAIRS, SB_Q_UNROLL,
                cast=ffn_cast(i)[1:] + [(even_w_out, j), (odd_w_in, j)])
            acts = [oa.reshape(b * s, qa), ob.reshape(b * s, hb)]
        else:
            colgain = jnp.concatenate([
                jnp.tile(odd_q_norm[j].astype(F32), C_HEADS) * scale2,
                jnp.tile(odd_k_norm[j].astype(F32), C_HEADS), ones(hc)])
            colflag = jnp.concatenate([ones(2 * hc), zeros(hc)])
            proj = _norm_proj(x2, attn_norm[i], w_in, colgain, colflag,
                              2 * hc, PROJ_ROWS, PROJ_CHUNK_ODD).reshape(b, s, -1)
            cast = ffn_cast(i) + [(odd_w_out, j)]
            if i + 1 < depth:
                cast.append((even_w_in, j + 1))
            oc, w_gate, w_up, w_down, w_out, *rest = _dilated_attention(proj, slopes_c, cast=cast)
            w_in = rest[0] if rest else None
            acts = [oc.reshape(b * s, hc)]
        x2 = _out_proj(x2, acts, w_out, OUT_ROWS)
        x2 = _ffn(x2, ffn_norm[i], w_gate, w_up, w_down, FFN_ROWS, FFN_COLS)
    return x2.reshape(b, s, d)
```

```python
import functools

import jax
import jax.numpy as jnp
from jax import lax
from jax.experimental import pallas as pl
from jax.experimental.pallas import tpu as pltpu

HEAD_DIM = 64
LANES = 128
BLOCK = 128
A_Q_HEADS = 16
A_KV_HEADS = 2
B_HEADS = 16
C_HEADS = 32
C_PAIRS = ((128, 1), (512, 4), (2048, 16))
RMS_EPS = 1e-6
MASKED = -1e30
LOG2E = 1.4426950408889634
SB_DECAY_DONE = 127.0
VMEM_LIMIT = 56 * 1024 * 1024
MXU_WIDTH = 256
BF16_SUBLANES = 16
TILE_UNROLL = 32
SB_PAIRS = 2
SB_Q_UNROLL = 4
PROJ_ROWS = 512
PROJ_CHUNK_EVEN = 1024
PROJ_CHUNK_ODD = 512
OUT_ROWS = 512
FFN_ROWS = 1024
FFN_COLS = 512

F32 = jnp.float32
BF16 = jnp.bfloat16


def _dot(a, b):
    return jnp.dot(a, b, preferred_element_type=F32)


def _dot_t(a, b):
    return lax.dot_general(a, b, (((1,), (1,)), ((), ())), preferred_element_type=F32)


def _params(*sem):
    return pltpu.CompilerParams(dimension_semantics=sem, vmem_limit_bytes=VMEM_LIMIT)


def _cast_specs(passengers, n_steps, step_of):
    in_specs, out_specs, out_shapes = [], [], []
    for w, layer in passengers:
        _, rows, cols = w.shape
        n_col = next(c for c in (1, 2, 4, 8) if n_steps % c == 0
                     and rows % (BF16_SUBLANES * (n_steps // c)) == 0
                     and cols % (LANES * c) == 0)
        slab = (rows // (n_steps // n_col), cols // n_col)
        in_specs.append(pl.BlockSpec(
            (1,) + slab,
            lambda *g, layer=layer, n_col=n_col: (layer, step_of(*g) // n_col, step_of(*g) % n_col)))
        out_specs.append(pl.BlockSpec(
            slab, lambda *g, n_col=n_col: (step_of(*g) // n_col, step_of(*g) % n_col)))
        out_shapes.append(jax.ShapeDtypeStruct((rows, cols), BF16))
    return in_specs, out_specs, out_shapes


def _cast_rows(src_refs, dst_refs):
    for src, dst in zip(src_refs, dst_refs):
        dst[...] = src[0].astype(BF16)


def _split_refs(refs, n_in, n_cast):
    a, b, c = n_in + n_cast, n_in + n_cast + 1, n_in + 2 * n_cast + 1
    return refs[:n_in], refs[n_in:a], refs[a], refs[b:c], refs[c:]


def _proj_body(x_ref, g_ref, w_ref, cg_ref, cf_ref, s_ref, o_ref, *, n_norm_cols, chunk):
    x = x_ref[...]
    inv = lax.rsqrt(jnp.mean(x * x, axis=-1, keepdims=True) + RMS_EPS)
    h = ((x * inv) * g_ref[...]).astype(BF16)
    n_out = o_ref.shape[1]
    for c0 in range(0, n_out, chunk):
        width = min(chunk, n_out - c0)
        y = _dot(h, w_ref[:, c0:c0 + width])
        for c in range(c0, c0 + width, MXU_WIDTH):
            sl = slice(c, c + MXU_WIDTH)
            yc = y[:, c - c0:c - c0 + MXU_WIDTH]
            cg = cg_ref[:, sl]
            if c < n_norm_cols:
                ss = _dot((yc * yc).astype(BF16), s_ref[...])
                inv_h = lax.rsqrt(ss * (1.0 / HEAD_DIM) + RMS_EPS)
                scale = jnp.where(cf_ref[:, sl] > 0.0, inv_h * cg, cg)
            else:
                scale = cg
            o_ref[:, sl] = (yc * scale).astype(BF16)


def _norm_proj(x2, gain, w, colgain, colflag, n_norm_cols, tm, chunk):
    n, d = x2.shape
    n_out = w.shape[1]
    assert n % tm == 0 and n_out % MXU_WIDTH == 0 and chunk % MXU_WIDTH == 0
    idx = jnp.arange(MXU_WIDTH) // HEAD_DIM
    seg = (idx[:, None] == idx[None, :]).astype(BF16)
    const = lambda shape: pl.BlockSpec(shape, lambda i: (0, 0), pipeline_mode=pl.Buffered(1))
    return pl.pallas_call(
        functools.partial(_proj_body, n_norm_cols=n_norm_cols, chunk=chunk),
        grid=(n // tm,),
        in_specs=[
            pl.BlockSpec((tm, d), lambda i: (i, 0)),
            const((1, d)),
            const((d, n_out)),
            const((1, n_out)),
            const((1, n_out)),
            const((MXU_WIDTH, MXU_WIDTH)),
        ],
        out_specs=pl.BlockSpec((tm, n_out), lambda i: (i, 0)),
        out_shape=jax.ShapeDtypeStruct((n, n_out), BF16),
        compiler_params=_params("parallel"),
        name="norm_proj",
    )(x2, gain.reshape(1, d), w, colgain.reshape(1, n_out), colflag.reshape(1, n_out), seg)


def _out_body(*refs, n_pairs):
    x_ref = refs[0]
    o_ref = refs[1 + 2 * n_pairs]
    acc = x_ref[...]
    for p in range(n_pairs):
        acc = acc + _dot(refs[1 + 2 * p][...], refs[2 + 2 * p][...])
    o_ref[...] = acc


def _out_proj(x2, acts, w, tm):
    n, d = x2.shape
    in_specs = [pl.BlockSpec((tm, d), lambda i: (i, 0))]
    args = [x2]
    offset = 0
    for a in acts:
        kk = a.shape[1]
        assert offset % kk == 0
        in_specs.append(pl.BlockSpec((tm, kk), lambda i: (i, 0)))
        in_specs.append(pl.BlockSpec((kk, d), lambda i, r=offset // kk: (r, 0)))
        args += [a, w]
        offset += kk
    assert offset == w.shape[0] and n % tm == 0
    return pl.pallas_call(
        functools.partial(_out_body, n_pairs=len(acts)),
        grid=(n // tm,),
        in_specs=in_specs,
        out_specs=pl.BlockSpec((tm, d), lambda i: (i, 0)),
        out_shape=jax.ShapeDtypeStruct((n, d), F32),
        compiler_params=_params("parallel"),
        name="out_proj",
    )(*args)


def _ffn_step(x_ref, g_ref, wg_ref, wu_ref, wd_ref, o_ref, *, h_scr, step_scr, n_cols):
    step = step_scr[0]
    step_scr[0] = step + 1
    j = lax.rem(step, n_cols)

    @pl.when(j == 0)
    def _():
        x = x_ref[...]
        inv = lax.rsqrt(jnp.mean(x * x, axis=-1, keepdims=True) + RMS_EPS)
        h_scr[...] = ((x * inv) * g_ref[...]).astype(BF16)
        o_ref[...] = x

    h = h_scr[...]
    gate = _dot(h, wg_ref[...])
    up = _dot(h, wu_ref[...])
    act = (gate * (1.0 / (1.0 + jnp.exp(-gate)))) * up
    o_ref[...] += _dot(act.astype(BF16), wd_ref[...])


def _ffn_body(x_hbm, g_hbm, wg_hbm, wu_hbm, wd_hbm, o_hbm, h_scr, step_scr, *, tm, tf):
    n, d = x_hbm.shape
    dff = wg_hbm.shape[1]
    step_scr[0] = 0
    pltpu.emit_pipeline(
        functools.partial(_ffn_step, h_scr=h_scr, step_scr=step_scr, n_cols=dff // tf),
        grid=(n // tm, dff // tf),
        in_specs=[
            pl.BlockSpec((tm, d), lambda i, j: (i, 0)),
            pl.BlockSpec((1, d), lambda i, j: (0, 0)),
            pl.BlockSpec((d, tf), lambda i, j: (0, j)),
            pl.BlockSpec((d, tf), lambda i, j: (0, j)),
            pl.BlockSpec((tf, d), lambda i, j: (j, 0)),
        ],
        out_specs=[pl.BlockSpec((tm, d), lambda i, j: (i, 0))],
    )(x_hbm, g_hbm, wg_hbm, wu_hbm, wd_hbm, o_hbm)


def _ffn(x2, gain, wg, wu, wd, tm, tf):
    n, d = x2.shape
    dff = wg.shape[1]
    assert n % tm == 0 and dff % tf == 0
    any_space = pl.BlockSpec(memory_space=pl.ANY)
    return pl.pallas_call(
        functools.partial(_ffn_body, tm=tm, tf=tf),
        in_specs=[any_space] * 5,
        out_specs=any_space,
        out_shape=jax.ShapeDtypeStruct((n, d), F32),
        scratch_shapes=[pltpu.VMEM((tm, d), BF16), pltpu.SMEM((1,), jnp.int32)],
        compiler_params=pltpu.CompilerParams(vmem_limit_bytes=VMEM_LIMIT),
        name="ffn",
    )(x2, gain.reshape(1, d), wg, wu, wd)


def _stack_heads(q):
    lo = lax.broadcasted_iota(jnp.int32, q.shape, 1) < HEAD_DIM
    zero = jnp.zeros_like(q)
    return jnp.concatenate([jnp.where(lo, q, zero), jnp.where(lo, zero, q)], axis=0)


def _unstack_heads(t):
    lo = lax.broadcasted_iota(jnp.int32, (BLOCK, LANES), 1) < HEAD_DIM
    return jnp.where(lo, t[:BLOCK], t[BLOCK:])


def _split_heads(v):
    lo = lax.broadcasted_iota(jnp.int32, v.shape, 1) < HEAD_DIM
    zero = jnp.zeros_like(v)
    return jnp.where(lo, v, zero), jnp.where(lo, zero, v)


def _band_rows(ref, n, lanes=slice(None)):
    r0 = pl.multiple_of(n * BLOCK, BLOCK)
    p0 = pl.multiple_of(jnp.maximum(n - 1, 0) * BLOCK, BLOCK)
    return jnp.concatenate([ref[pl.ds(p0, BLOCK), lanes], ref[pl.ds(r0, BLOCK), lanes]], axis=0)


def _band_softmax(lhs, k_band, va_band, vb_band, bias, sink=None):
    s = _dot_t(lhs, k_band) + bias
    m = jnp.broadcast_to(jnp.max(s, axis=1, keepdims=True), (2 * BLOCK, LANES))
    if sink is not None:
        m = jnp.maximum(m, sink)
    p = jnp.exp2(s - jnp.concatenate([m, m], axis=1)).astype(BF16)
    p_cat = jnp.concatenate([p[:BLOCK], p[BLOCK:]], axis=1)
    lo = lax.broadcasted_iota(jnp.int32, (2 * BLOCK, LANES), 1) < HEAD_DIM
    ones_a = jnp.where(lo, 1.0, 0.0).astype(BF16)
    ones_b = jnp.where(lo, 0.0, 1.0).astype(BF16)
    rhs = jnp.concatenate([jnp.concatenate([va_band, vb_band], axis=0),
                           jnp.concatenate([ones_a, ones_b], axis=0)], axis=1)
    ext = _dot(p_cat, rhs)
    return ext[:, :LANES], _unstack_heads(m), ext[:, LANES:]


def _band_bias(slopes, window_max, dist_scale, strict):
    i = jnp.arange(BLOCK)[:, None]
    j = jnp.arange(2 * BLOCK)[None, :]
    dist = i + BLOCK - j
    valid = (dist >= 0) & ((dist < window_max) if strict else (dist <= window_max))
    alibi = (slopes[:, None, None] * (dist * dist_scale).astype(F32)[None]) * LOG2E
    h = slopes.shape[0]
    table = jnp.stack([jnp.where(valid[None], -alibi, MASKED),
                       jnp.where((valid & (j >= BLOCK))[None], -alibi, MASKED)], axis=1)
    table = table.reshape(h // 2, 2, 2, BLOCK, 2 * BLOCK).transpose(0, 2, 1, 3, 4)
    return table.reshape(h // 2, 2, 2 * BLOCK, 2 * BLOCK)


def _pair_rows(t):
    h = t.shape[0]
    return t.reshape((h // 2, 2 * t.shape[1]) + t.shape[2:])


def _swa_body(*refs, n_cast):
    ins, cast_in, o_ref, cast_out, (kd_scr, va_scr, vb_scr) = _split_refs(refs, 5, n_cast)
    q_ref, k_ref, v_ref, bias_ref, sink_ref = ins
    _cast_rows(cast_in, cast_out)
    s = q_ref.shape[1]
    c = pl.program_id(1)
    blocks_per_kv = A_Q_HEADS // A_KV_HEADS // 2
    kv_head = c // blocks_per_kv

    @pl.when(c % blocks_per_kv == 0)
    def _():
        lane_head = lax.broadcasted_iota(jnp.int32, (s, LANES), 1) // HEAD_DIM
        sel = lane_head == kv_head
        kf = jnp.where(sel, k_ref[0].astype(F32), 0.0)
        kd_scr[...] = (kf + pltpu.roll(kf, HEAD_DIM, axis=1)).astype(BF16)
        vf = jnp.where(sel, v_ref[0].astype(F32), 0.0)
        vr = pltpu.roll(vf, HEAD_DIM, axis=1)
        va_scr[...] = jnp.where(kv_head == 0, vf, vr).astype(BF16)
        vb_scr[...] = jnp.where(kv_head == 0, vr, vf).astype(BF16)

    def block(n, carry):
        rows = pl.ds(pl.multiple_of(n * BLOCK, BLOCK), BLOCK)
        lhs = _stack_heads(q_ref[0, rows, :])
        first = jnp.where(n == 0, 1, 0)
        num, m, den = _band_softmax(lhs, _band_rows(kd_scr, n), _band_rows(va_scr, n),
                                    _band_rows(vb_scr, n), bias_ref[0, first],
                                    sink=sink_ref[0, :2 * BLOCK])
        den = den + jnp.exp2(sink_ref[0, 2 * BLOCK:] - m)
        o_ref[0, rows, :] = (num / den).astype(BF16)
        return carry

    lax.fori_loop(0, s // BLOCK, block, 0, unroll=TILE_UNROLL)


def _swa_attention(proj, sinks, slopes, cast=()):
    b, s, _ = proj.shape
    n_blocks = A_Q_HEADS // 2
    cast_in, cast_out, cast_shapes = _cast_specs(cast, b * n_blocks,
                                                 lambda i, c: i * n_blocks + c)
    k_col = A_Q_HEADS * HEAD_DIM // LANES
    bias = _band_bias(slopes, BLOCK, 1, strict=True)
    sink2 = sinks.astype(F32) * LOG2E
    stacked = _pair_rows(jnp.broadcast_to(sink2[:, None, None], (A_Q_HEADS, BLOCK, LANES)))
    by_lane = jnp.broadcast_to(jnp.repeat(sink2, HEAD_DIM).reshape(n_blocks, 1, LANES),
                               (n_blocks, BLOCK, LANES))
    sink = jnp.concatenate([stacked, by_lane], axis=1)
    return pl.pallas_call(
        functools.partial(_swa_body, n_cast=len(cast)),
        grid=(b, n_blocks),
        in_specs=[
            pl.BlockSpec((1, s, LANES), lambda i, c: (i, 0, c)),
            pl.BlockSpec((1, s, LANES), lambda i, c: (i, 0, k_col)),
            pl.BlockSpec((1, s, LANES), lambda i, c: (i, 0, k_col + 1)),
            pl.BlockSpec((1, 2, 2 * BLOCK, 2 * BLOCK), lambda i, c: (c, 0, 0, 0)),
            pl.BlockSpec((1, 3 * BLOCK, LANES), lambda i, c: (c, 0, 0)),
        ] + cast_in,
        out_specs=[pl.BlockSpec((1, s, LANES), lambda i, c: (i, 0, c))] + cast_out,
        out_shape=[jax.ShapeDtypeStruct((b, s, n_blocks * LANES), BF16)] + cast_shapes,
        scratch_shapes=[pltpu.VMEM((s, LANES), BF16)] * 3,
        compiler_params=_params("arbitrary", "arbitrary"),
        name="swa_attention",
    )(proj, proj, proj, bias, sink, *[w for w, _ in cast])


def _sb_softplus(z):
    return jnp.maximum(z, 0.0) + jnp.log2(1.0 + jnp.exp2(-jnp.abs(z)))


def _hi_lo(x):
    hi = x.astype(BF16)
    lo = (x - hi.astype(F32)).astype(BF16)
    return jnp.concatenate([hi, lo], axis=1)


def _sb_weighted_values(w, va, vb):
    return _dot(jnp.concatenate([w[:BLOCK], w[BLOCK:]], axis=1),
                jnp.concatenate([va, vb], axis=0))


def _sb_band_scores(lhs, k_band, first_pen, causal):
    z = _dot_t(lhs, k_band)
    z_prev, z_diag = z[:, :BLOCK] + first_pen, z[:, BLOCK:]
    drop = jnp.concatenate([_sb_softplus(z_prev), jnp.where(causal, _sb_softplus(z_diag), 0.0)],
                           axis=1)
    return z_prev, z_diag, _hi_lo(drop)


def _sb_band_weights(z_prev, z_diag, drop_hi_lo, u_band, causal):
    cs = _dot(drop_hi_lo, u_band)
    w = jnp.concatenate([jnp.exp2(z_prev - cs[:, :BLOCK]),
                         jnp.where(causal, jnp.exp2(z_diag - cs[:, BLOCK:]), 0.0)], axis=1)
    return w.astype(BF16), jnp.broadcast_to(cs[:, :1], (2 * BLOCK, LANES))


def _sb_body(*refs, pairs, q_unroll, n_cast):
    ins, cast_in, o_ref, cast_out, scratch = _split_refs(refs, 5, n_cast)
    q_ref, k_ref, v_ref, ub_ref, u_ref = ins
    va_scr, vb_scr, acc_scr, run_scr = scratch
    _cast_rows(cast_in, cast_out)
    s = q_ref.shape[1]
    lo = lax.broadcasted_iota(jnp.int32, (s, pairs * LANES), 1) % LANES < HEAD_DIM
    v_all = v_ref[0]
    va_scr[...] = jnp.where(lo, v_all, jnp.zeros_like(v_all))
    vb_scr[...] = jnp.where(lo, jnp.zeros_like(v_all), v_all)
    causal = (lax.broadcasted_iota(jnp.int32, (2 * BLOCK, BLOCK), 1)
              < lax.broadcasted_iota(jnp.int32, (2 * BLOCK, BLOCK), 0) % BLOCK)
    k2 = k_ref.at[0]
    lanes = [slice(g * LANES, (g + 1) * LANES) for g in range(pairs)]

    def q_group(i, carry):
        blocks = [i * q_unroll + qi for qi in range(q_unroll)]
        tiles = [(qi, n, g) for qi, n in enumerate(blocks) for g in range(pairs)]
        scores = {}
        for qi, n, g in tiles:
            rows = pl.ds(pl.multiple_of(n * BLOCK, BLOCK), BLOCK)
            lhs = _stack_heads(q_ref[0, rows, lanes[g]])
            scores[qi, g] = _sb_band_scores(lhs, _band_rows(k2, n, lanes[g]),
                                            jnp.where(n > 0, 0.0, MASKED), causal)
        weights = {}
        for qi, n, g in tiles:
            weights[qi, g] = _sb_band_weights(*scores[qi, g], ub_ref[...], causal)
        decay = [jnp.float32(jnp.inf)] * q_unroll
        for qi, n, g in tiles:
            w, total = weights[qi, g]
            acc_scr[qi, g] = _sb_weighted_values(w, _band_rows(va_scr, n, lanes[g]),
                                                 _band_rows(vb_scr, n, lanes[g]))
            run_scr[qi, g] = total
            decay[qi] = jnp.minimum(decay[qi], jnp.min(total))

        for qi, n in enumerate(blocks):
            rows = pl.ds(pl.multiple_of(n * BLOCK, BLOCK), BLOCK)

            def more(st):
                j, least = st
                return jnp.logical_and(j >= 0, least < SB_DECAY_DONE)

            def key_block(st, qi=qi, rows=rows):
                j, _ = st
                keys = pl.ds(pl.multiple_of(j * BLOCK, BLOCK), BLOCK)
                zs = [_dot_t(_stack_heads(q_ref[0, rows, lanes[g]]), k_ref[0, keys, lanes[g]])
                      for g in range(pairs)]
                cs = [_dot(_hi_lo(_sb_softplus(z)), u_ref[...]) for z in zs]
                least = jnp.float32(jnp.inf)
                for g in range(pairs):
                    run = run_scr[qi, g]
                    w = jnp.exp2(zs[g] - cs[g][:, :BLOCK] - run).astype(BF16)
                    acc_scr[qi, g] += _sb_weighted_values(w, va_scr[keys, lanes[g]],
                                                          vb_scr[keys, lanes[g]])
                    run = run + cs[g][:, BLOCK:]
                    run_scr[qi, g] = run
                    least = jnp.minimum(least, jnp.min(run))
                return j - 1, least

            lax.while_loop(more, key_block, (n - 2, decay[qi]))
            for g in range(pairs):
                o_ref[0, rows, lanes[g]] = acc_scr[qi, g].astype(BF16)
        return carry

    lax.fori_loop(0, s // BLOCK // q_unroll, q_group, 0)


def _sb_attention(proj, q_col, k_col, v_col, pairs, q_unroll, cast=()):
    b, s, _ = proj.shape
    width = pairs * LANES
    n_steps = B_HEADS // 2 // pairs
    cast_in, cast_out, cast_shapes = _cast_specs(cast, b * n_steps, lambda i, c: i * n_steps + c)
    assert q_col % pairs == 0 and k_col % pairs == 0 and v_col % pairs == 0
    assert (s // BLOCK) % q_unroll == 0

    def suffix(n):
        kk = jnp.arange(n)
        return (kk[:, None] >= kk[None, :]).astype(BF16)

    u_band = jnp.tile(suffix(2 * BLOCK), (2, 1))
    u_block = jnp.tile(jnp.concatenate([suffix(BLOCK), jnp.ones((BLOCK, BLOCK), BF16)], axis=1),
                       (2, 1))
    return pl.pallas_call(
        functools.partial(_sb_body, pairs=pairs, q_unroll=q_unroll, n_cast=len(cast)),
        grid=(b, n_steps),
        in_specs=[
            pl.BlockSpec((1, s, width), lambda i, c: (i, 0, q_col // pairs + c)),
            pl.BlockSpec((1, s, width), lambda i, c: (i, 0, k_col // pairs + c)),
            pl.BlockSpec((1, s, width), lambda i, c: (i, 0, v_col // pairs + c)),
            pl.BlockSpec((4 * BLOCK, 2 * BLOCK), lambda i, c: (0, 0)),
            pl.BlockSpec((2 * BLOCK, 2 * BLOCK), lambda i, c: (0, 0)),
        ] + cast_in,
        out_specs=[pl.BlockSpec((1, s, width), lambda i, c: (i, 0, c))] + cast_out,
        out_shape=[jax.ShapeDtypeStruct((b, s, B_HEADS * HEAD_DIM), BF16)] + cast_shapes,
        scratch_shapes=[pltpu.VMEM((s, width), BF16), pltpu.VMEM((s, width), BF16),
                        pltpu.VMEM((q_unroll, pairs, BLOCK, LANES), F32),
                        pltpu.VMEM((q_unroll, pairs, 2 * BLOCK, LANES), F32)],
        compiler_params=_params("arbitrary", "arbitrary"),
        name="stick_breaking",
    )(proj, proj, proj, u_band, u_block, *[w for w, _ in cast])


RESIDUE_STEP = 4


def _to_residue_major(src, dst, seg):
    part = seg // RESIDUE_STEP
    for base in range(0, src.shape[0], seg):
        for r in range(RESIDUE_STEP):
            dst[base + r * part:base + (r + 1) * part, :] = (
                src[pl.ds(base + r, part, stride=RESIDUE_STEP), :])


def _from_residue_major(src, dst, seg):
    part = seg // RESIDUE_STEP
    for base in range(0, src.shape[0], seg):
        for r in range(RESIDUE_STEP):
            dst[pl.ds(base + r, part, stride=RESIDUE_STEP), :] = (
                src[base + r * part:base + (r + 1) * part, :])


def _dil_body(*refs, n_cast):
    ins, cast_in, o_ref, cast_out, scratch = _split_refs(refs, 4, n_cast)
    q_ref, k_ref, v_ref, bias_ref = ins
    tmp_a, tmp_b, q4, k4, q16, k16, va1, vb1, va4, vb4, va16, vb16, st_a, st_b = scratch
    _cast_rows(cast_in, cast_out)
    s = q_ref.shape[1]
    va1[...], vb1[...] = _split_heads(v_ref[0])
    for src, d4, d16 in ((q_ref, (q4,), (q16,)), (k_ref, (k4,), (k16,)),
                         (v_ref, (va4, vb4), (va16, vb16))):
        tmp_a[...] = src[0].astype(F32)
        _to_residue_major(tmp_a, tmp_b, s)
        _to_residue_major(tmp_b, tmp_a, s // RESIDUE_STEP)
        for dsts, tmp in ((d4, tmp_b), (d16, tmp_a)):
            vals = tmp[...].astype(BF16)
            if len(dsts) == 1:
                dsts[0][...] = vals
            else:
                dsts[0][...], dsts[1][...] = _split_heads(vals)

    layouts = ((q_ref.at[0], k_ref.at[0], va1, vb1), (q4, k4, va4, vb4), (q16, k16, va16, vb16))
    state, spare = st_a, st_b
    for step, branch in enumerate(reversed(range(len(C_PAIRS)))):
        qb, kb, va, vb = layouts[branch]
        class_blocks = s // C_PAIRS[branch][1] // BLOCK
        if step > 0:
            for a in range(3):
                _from_residue_major(state.at[a], spare.at[a],
                                    s // RESIDUE_STEP if step == 1 else s)
            state, spare = spare, state

        def tile(t, carry, step=step, branch=branch, qb=qb, kb=kb, va=va, vb=vb, state=state,
                 class_blocks=class_blocks):
            rows = pl.ds(pl.multiple_of(t * BLOCK, BLOCK), BLOCK)
            lhs = _stack_heads(qb[rows, :])
            first = jnp.where(t % class_blocks == 0, 1, 0)
            num, m, den = _band_softmax(lhs, _band_rows(kb, t), _band_rows(va, t),
                                        _band_rows(vb, t), bias_ref[branch, 0, first])
            if step > 0:
                m_old = state[0, rows, :]
                m_new = jnp.maximum(m_old, m)
                a_old = jnp.exp2(m_old - m_new)
                a_cur = jnp.exp2(m - m_new)
                num = a_old * state[2, rows, :] + a_cur * num
                den = a_old * state[1, rows, :] + a_cur * den
                m = m_new
            if branch == 0:
                o_ref[0, rows, :] = (num / den).astype(BF16)
            else:
                state[0, rows, :] = m
                state[1, rows, :] = den
                state[2, rows, :] = num
            return carry

        lax.fori_loop(0, s // BLOCK, tile, 0, unroll=TILE_UNROLL)


def _dilated_attention(proj, slopes, cast=()):
    b, s, _ = proj.shape
    n_blocks = C_HEADS // 2
    cast_in, cast_out, cast_shapes = _cast_specs(cast, b * n_blocks,
                                                 lambda i, c: i * n_blocks + c)
    assert C_PAIRS[0][1] == 1 and C_PAIRS[1][1] == RESIDUE_STEP
    assert C_PAIRS[2][1] == RESIDUE_STEP ** 2
    bias = jnp.stack([_band_bias(slopes, w // d, d, strict=False)
                      for (w, d) in C_PAIRS])
    scr = ([pltpu.VMEM((s, LANES), F32)] * 2 + [pltpu.VMEM((s, LANES), BF16)] * 10
           + [pltpu.VMEM((3, s, LANES), F32)] * 2)
    return pl.pallas_call(
        functools.partial(_dil_body, n_cast=len(cast)),
        grid=(b, n_blocks),
        in_specs=[
            pl.BlockSpec((1, s, LANES), lambda i, c: (i, 0, c)),
            pl.BlockSpec((1, s, LANES), lambda i, c: (i, 0, n_blocks + c)),
            pl.BlockSpec((1, s, LANES), lambda i, c: (i, 0, 2 * n_blocks + c)),
            pl.BlockSpec((len(C_PAIRS), 1, 2, 2 * BLOCK, 2 * BLOCK),
                         lambda i, c: (0, c, 0, 0, 0)),
        ] + cast_in,
        out_specs=[pl.BlockSpec((1, s, LANES), lambda i, c: (i, 0, c))] + cast_out,
        out_shape=[jax.ShapeDtypeStruct((b, s, n_blocks * LANES), BF16)] + cast_shapes,
        scratch_shapes=scr,
        compiler_params=_params("arbitrary", "arbitrary"),
        name="dilated_mixture",
    )(proj, proj, proj, bias, *[w for w, _ in cast])


def _alibi_slopes(n):
    return jnp.exp2(-8.0 * jnp.arange(1, n + 1, dtype=F32) / n)


def kernel(x, attn_norm, ffn_norm, even_w_in, even_q_norm, even_k_norm, even_sinks, even_w_out,
           odd_w_in, odd_q_norm, odd_k_norm, odd_w_out, ffn_w_gate, ffn_w_up, ffn_w_down):
    b, s, d = x.shape
    depth = attn_norm.shape[0]
    scale = HEAD_DIM ** -0.5
    scale2 = scale * LOG2E
    slopes_a = _alibi_slopes(A_Q_HEADS)
    slopes_c = _alibi_slopes(C_HEADS)
    qa, kva, hb = A_Q_HEADS * HEAD_DIM, A_KV_HEADS * HEAD_DIM, B_HEADS * HEAD_DIM
    hc = C_HEADS * HEAD_DIM
    ones = lambda n: jnp.ones((n,), F32)
    zeros = lambda n: jnp.zeros((n,), F32)

    ffn_cast = lambda i: [(ffn_w_gate, i), (ffn_w_up, i), (ffn_w_down, i)]
    w_in = even_w_in[0].astype(BF16)

    x2 = x.reshape(b * s, d)
    for i in range(depth):
        j = i // 2
        if i % 2 == 0:
            colgain = jnp.concatenate([
                jnp.tile(even_q_norm[j].astype(F32), A_Q_HEADS) * scale2,
                jnp.tile(even_k_norm[j].astype(F32), A_KV_HEADS),
                ones(kva), ones(hb) * scale2, ones(hb), ones(hb)])
            colflag = jnp.concatenate([ones(qa + kva), zeros(kva + 3 * hb)])
            proj = _norm_proj(x2, attn_norm[i], w_in, colgain, colflag,
                              qa + kva, PROJ_ROWS, PROJ_CHUNK_EVEN).reshape(b, s, -1)
            oa, w_gate = _swa_attention(proj, even_sinks[j], slopes_a, cast=ffn_cast(i)[:1])
            qb_col = (qa + 2 * kva) // LANES
            ob, w_up, w_down, w_out, w_in = _sb_attention(
                proj, qb_col, qb_col + hb // LANES, qb_col + 2 * hb // LANES, SB_PAIRS, SB_Q_UNROLL,
                cast=ffn_cast(i)[1:] + [(even_w_out, j), (odd_w_in, j)])
            acts = [oa.reshape(b * s, qa), ob.reshape(b * s, hb)]
        else:
            colgain = jnp.concatenate([
                jnp.tile(odd_q_norm[j].astype(F32), C_HEADS) * scale2,
                jnp.tile(odd_k_norm[j].astype(F32), C_HEADS), ones(hc)])
            colflag = jnp.concatenate([ones(2 * hc), zeros(hc)])
            proj = _norm_proj(x2, attn_norm[i], w_in, colgain, colflag,
                              2 * hc, PROJ_ROWS, PROJ_CHUNK_ODD).reshape(b, s, -1)
            cast = ffn_cast(i) + [(odd_w_out, j)]
            if i + 1 < depth:
                cast.append((even_w_in, j + 1))
            oc, w_gate, w_up, w_down, w_out, *rest = _dilated_attention(proj, slopes_c, cast=cast)
            w_in = rest[0] if rest else None
            acts = [oc.reshape(b * s, hc)]
        x2 = _out_proj(x2, acts, w_out, OUT_ROWS)
        x2 = _ffn(x2, ffn_norm[i], w_gate, w_up, w_down, FFN_ROWS, FFN_COLS)
    return x2.reshape(b, s, d)
```

```python
import functools

import jax
import jax.numpy as jnp
from jax import lax
from jax.experimental import pallas as pl
from jax.experimental.pallas import tpu as pltpu

HEAD_DIM = 64
LANES = 128
BLOCK = 128
A_Q_HEADS = 16
A_KV_HEADS = 2
B_HEADS = 16
C_HEADS = 32
C_PAIRS = ((128, 1), (512, 4), (2048, 16))
RMS_EPS = 1e-6
MASKED = -1e30
LOG2E = 1.4426950408889634
SB_DECAY_DONE = 127.0
VMEM_LIMIT = 62 * 1024 * 1024
MXU_WIDTH = 256
BF16_SUBLANES = 16
TILE_UNROLL = 32
SB_PAIRS = 2
SB_Q_UNROLL = 4
PROJ_ROWS = 512
PROJ_CHUNK_EVEN = 1024
PROJ_CHUNK_ODD = 512
OUT_ROWS = 512
FFN_ROWS = 1024
FFN_COLS = 512

F32 = jnp.float32
BF16 = jnp.bfloat16


def _dot(a, b):
    return jnp.dot(a, b, preferred_element_type=F32)


def _dot_t(a, b):
    return lax.dot_general(a, b, (((1,), (1,)), ((), ())), preferred_element_type=F32)


def _params(*sem):
    return pltpu.CompilerParams(dimension_semantics=sem, vmem_limit_bytes=VMEM_LIMIT)


def _cast_specs(passengers, n_steps, step_of):
    in_specs, out_specs, out_shapes = [], [], []
    for w, layer in passengers:
        _, rows, cols = w.shape
        n_col = next(c for c in (1, 2, 4, 8) if n_steps % c == 0
                     and rows % (BF16_SUBLANES * (n_steps // c)) == 0
                     and cols % (LANES * c) == 0)
        slab = (rows // (n_steps // n_col), cols // n_col)
        in_specs.append(pl.BlockSpec(
            (1,) + slab,
            lambda *g, layer=layer, n_col=n_col: (layer, step_of(*g) // n_col, step_of(*g) % n_col)))
        out_specs.append(pl.BlockSpec(
            slab, lambda *g, n_col=n_col: (step_of(*g) // n_col, step_of(*g) % n_col)))
        out_shapes.append(jax.ShapeDtypeStruct((rows, cols), BF16))
    return in_specs, out_specs, out_shapes


def _cast_rows(src_refs, dst_refs):
    for src, dst in zip(src_refs, dst_refs):
        dst[...] = src[0].astype(BF16)


def _split_refs(refs, n_in, n_cast):
    a, b, c = n_in + n_cast, n_in + n_cast + 1, n_in + 2 * n_cast + 1
    return refs[:n_in], refs[n_in:a], refs[a], refs[b:c], refs[c:]


def _proj_body(x_ref, g_ref, w_ref, cg_ref, cf_ref, s_ref, o_ref, *, n_norm_cols, chunk):
    x = x_ref[...]
    inv = lax.rsqrt(jnp.mean(x * x, axis=-1, keepdims=True) + RMS_EPS)
    h = ((x * inv) * g_ref[...]).astype(BF16)
    n_out = o_ref.shape[1]
    for c0 in range(0, n_out, chunk):
        width = min(chunk, n_out - c0)
        y = _dot(h, w_ref[:, c0:c0 + width])
        for c in range(c0, c0 + width, MXU_WIDTH):
            sl = slice(c, c + MXU_WIDTH)
            yc = y[:, c - c0:c - c0 + MXU_WIDTH]
            cg = cg_ref[:, sl]
            if c < n_norm_cols:
                ss = _dot((yc * yc).astype(BF16), s_ref[...])
                inv_h = lax.rsqrt(ss * (1.0 / HEAD_DIM) + RMS_EPS)
                scale = jnp.where(cf_ref[:, sl] > 0.0, inv_h * cg, cg)
            else:
                scale = cg
            o_ref[:, sl] = (yc * scale).astype(BF16)


def _norm_proj(x2, gain, w, colgain, colflag, n_norm_cols, tm, chunk):
    n, d = x2.shape
    n_out = w.shape[1]
    assert n % tm == 0 and n_out % MXU_WIDTH == 0 and chunk % MXU_WIDTH == 0
    idx = jnp.arange(MXU_WIDTH) // HEAD_DIM
    seg = (idx[:, None] == idx[None, :]).astype(BF16)
    const = lambda shape: pl.BlockSpec(shape, lambda i: (0, 0), pipeline_mode=pl.Buffered(1))
    return pl.pallas_call(
        functools.partial(_proj_body, n_norm_cols=n_norm_cols, chunk=chunk),
        grid=(n // tm,),
        in_specs=[
            pl.BlockSpec((tm, d), lambda i: (i, 0)),
            const((1, d)),
            const((d, n_out)),
            const((1, n_out)),
            const((1, n_out)),
            const((MXU_WIDTH, MXU_WIDTH)),
        ],
        out_specs=pl.BlockSpec((tm, n_out), lambda i: (i, 0)),
        out_shape=jax.ShapeDtypeStruct((n, n_out), BF16),
        compiler_params=_params("parallel"),
        name="norm_proj",
    )(x2, gain.reshape(1, d), w, colgain.reshape(1, n_out), colflag.reshape(1, n_out), seg)


def _out_body(*refs, n_pairs):
    x_ref = refs[0]
    o_ref = refs[1 + 2 * n_pairs]
    acc = x_ref[...]
    for p in range(n_pairs):
        acc = acc + _dot(refs[1 + 2 * p][...], refs[2 + 2 * p][...])
    o_ref[...] = acc


def _out_proj(x2, acts, w, tm):
    n, d = x2.shape
    in_specs = [pl.BlockSpec((tm, d), lambda i: (i, 0))]
    args = [x2]
    offset = 0
    for a in acts:
        kk = a.shape[1]
        assert offset % kk == 0
        in_specs.append(pl.BlockSpec((tm, kk), lambda i: (i, 0)))
        in_specs.append(pl.BlockSpec((kk, d), lambda i, r=offset // kk: (r, 0)))
        args += [a, w]
        offset += kk
    assert offset == w.shape[0] and n % tm == 0
    return pl.pallas_call(
        functools.partial(_out_body, n_pairs=len(acts)),
        grid=(n // tm,),
        in_specs=in_specs,
        out_specs=pl.BlockSpec((tm, d), lambda i: (i, 0)),
        out_shape=jax.ShapeDtypeStruct((n, d), F32),
        compiler_params=_params("parallel"),
        name="out_proj",
    )(*args)


def _ffn_step(x_ref, g_ref, wg_ref, wu_ref, wd_ref, o_ref, *, h_scr, step_scr, n_cols):
    step = step_scr[0]
    step_scr[0] = step + 1
    j = lax.rem(step, n_cols)

    @pl.when(j == 0)
    def _():
        x = x_ref[...]
        inv = lax.rsqrt(jnp.mean(x * x, axis=-1, keepdims=True) + RMS_EPS)
        h_scr[...] = ((x * inv) * g_ref[...]).astype(BF16)
        o_ref[...] = x

    h = h_scr[...]
    gate = _dot(h, wg_ref[...])
    up = _dot(h, wu_ref[...])
    act = (gate * (1.0 / (1.0 + jnp.exp(-gate)))) * up
    o_ref[...] += _dot(act.astype(BF16), wd_ref[...])


def _ffn_body(x_hbm, g_hbm, wg_hbm, wu_hbm, wd_hbm, o_hbm, h_scr, step_scr, *, tm, tf):
    n, d = x_hbm.shape
    dff = wg_hbm.shape[1]
    step_scr[0] = 0
    pltpu.emit_pipeline(
        functools.partial(_ffn_step, h_scr=h_scr, step_scr=step_scr, n_cols=dff // tf),
        grid=(n // tm, dff // tf),
        in_specs=[
            pl.BlockSpec((tm, d), lambda i, j: (i, 0)),
            pl.BlockSpec((1, d), lambda i, j: (0, 0)),
            pl.BlockSpec((d, tf), lambda i, j: (0, j), pipeline_mode=pl.Buffered(3)),
            pl.BlockSpec((d, tf), lambda i, j: (0, j), pipeline_mode=pl.Buffered(3)),
            pl.BlockSpec((tf, d), lambda i, j: (j, 0)),
        ],
        out_specs=[pl.BlockSpec((tm, d), lambda i, j: (i, 0))],
    )(x_hbm, g_hbm, wg_hbm, wu_hbm, wd_hbm, o_hbm)


def _ffn(x2, gain, wg, wu, wd, tm, tf):
    n, d = x2.shape
    dff = wg.shape[1]
    assert n % tm == 0 and dff % tf == 0
    any_space = pl.BlockSpec(memory_space=pl.ANY)
    return pl.pallas_call(
        functools.partial(_ffn_body, tm=tm, tf=tf),
        in_specs=[any_space] * 5,
        out_specs=any_space,
        out_shape=jax.ShapeDtypeStruct((n, d), F32),
        scratch_shapes=[pltpu.VMEM((tm, d), BF16), pltpu.SMEM((1,), jnp.int32)],
        compiler_params=pltpu.CompilerParams(vmem_limit_bytes=VMEM_LIMIT),
        name="ffn",
    )(x2, gain.reshape(1, d), wg, wu, wd)


def _stack_heads(q):
    lo = lax.broadcasted_iota(jnp.int32, q.shape, 1) < HEAD_DIM
    zero = jnp.zeros_like(q)
    return jnp.concatenate([jnp.where(lo, q, zero), jnp.where(lo, zero, q)], axis=0)


def _unstack_heads(t):
    lo = lax.broadcasted_iota(jnp.int32, (BLOCK, LANES), 1) < HEAD_DIM
    return jnp.where(lo, t[:BLOCK], t[BLOCK:])


def _split_heads(v):
    lo = lax.broadcasted_iota(jnp.int32, v.shape, 1) < HEAD_DIM
    zero = jnp.zeros_like(v)
    return jnp.where(lo, v, zero), jnp.where(lo, zero, v)


def _band_rows(ref, n, lanes=slice(None)):
    r0 = pl.multiple_of(n * BLOCK, BLOCK)
    p0 = pl.multiple_of(jnp.maximum(n - 1, 0) * BLOCK, BLOCK)
    return jnp.concatenate([ref[pl.ds(p0, BLOCK), lanes], ref[pl.ds(r0, BLOCK), lanes]], axis=0)


def _band_softmax(lhs, k_band, va_band, vb_band, bias, sink=None):
    s = _dot_t(lhs, k_band) + bias
    m = jnp.broadcast_to(jnp.max(s, axis=1, keepdims=True), (2 * BLOCK, LANES))
    if sink is not None:
        m = jnp.maximum(m, sink)
    p = jnp.exp2(s - jnp.concatenate([m, m], axis=1)).astype(BF16)
    p_cat = jnp.concatenate([p[:BLOCK], p[BLOCK:]], axis=1)
    lo = lax.broadcasted_iota(jnp.int32, (2 * BLOCK, LANES), 1) < HEAD_DIM
    ones_a = jnp.where(lo, 1.0, 0.0).astype(BF16)
    ones_b = jnp.where(lo, 0.0, 1.0).astype(BF16)
    rhs = jnp.concatenate([jnp.concatenate([va_band, vb_band], axis=0),
                           jnp.concatenate([ones_a, ones_b], axis=0)], axis=1)
    ext = _dot(p_cat, rhs)
    return ext[:, :LANES], _unstack_heads(m), ext[:, LANES:]


def _band_bias(slopes, window_max, dist_scale, strict):
    i = jnp.arange(BLOCK)[:, None]
    j = jnp.arange(2 * BLOCK)[None, :]
    dist = i + BLOCK - j
    valid = (dist >= 0) & ((dist < window_max) if strict else (dist <= window_max))
    alibi = (slopes[:, None, None] * (dist * dist_scale).astype(F32)[None]) * LOG2E
    h = slopes.shape[0]
    table = jnp.stack([jnp.where(valid[None], -alibi, MASKED),
                       jnp.where((valid & (j >= BLOCK))[None], -alibi, MASKED)], axis=1)
    table = table.reshape(h // 2, 2, 2, BLOCK, 2 * BLOCK).transpose(0, 2, 1, 3, 4)
    return table.reshape(h // 2, 2, 2 * BLOCK, 2 * BLOCK)


def _pair_rows(t):
    h = t.shape[0]
    return t.reshape((h // 2, 2 * t.shape[1]) + t.shape[2:])


def _swa_body(*refs, n_cast):
    ins, cast_in, o_ref, cast_out, (kd_scr, va_scr, vb_scr) = _split_refs(refs, 5, n_cast)
    q_ref, k_ref, v_ref, bias_ref, sink_ref = ins
    _cast_rows(cast_in, cast_out)
    s = q_ref.shape[1]
    c = pl.program_id(1)
    blocks_per_kv = A_Q_HEADS // A_KV_HEADS // 2
    kv_head = c // blocks_per_kv

    @pl.when(c % blocks_per_kv == 0)
    def _():
        lane_head = lax.broadcasted_iota(jnp.int32, (s, LANES), 1) // HEAD_DIM
        sel = lane_head == kv_head
        kf = jnp.where(sel, k_ref[0].astype(F32), 0.0)
        kd_scr[...] = (kf + pltpu.roll(kf, HEAD_DIM, axis=1)).astype(BF16)
        vf = jnp.where(sel, v_ref[0].astype(F32), 0.0)
        vr = pltpu.roll(vf, HEAD_DIM, axis=1)
        va_scr[...] = jnp.where(kv_head == 0, vf, vr).astype(BF16)
        vb_scr[...] = jnp.where(kv_head == 0, vr, vf).astype(BF16)

    def block(n, carry):
        rows = pl.ds(pl.multiple_of(n * BLOCK, BLOCK), BLOCK)
        lhs = _stack_heads(q_ref[0, rows, :])
        first = jnp.where(n == 0, 1, 0)
        num, m, den = _band_softmax(lhs, _band_rows(kd_scr, n), _band_rows(va_scr, n),
                                    _band_rows(vb_scr, n), bias_ref[0, first],
                                    sink=sink_ref[0, :2 * BLOCK])
        den = den + jnp.exp2(sink_ref[0, 2 * BLOCK:] - m)
        o_ref[0, rows, :] = (num / den).astype(BF16)
        return carry

    lax.fori_loop(0, s // BLOCK, block, 0, unroll=TILE_UNROLL)


def _swa_attention(proj, sinks, slopes, cast=()):
    b, s, _ = proj.shape
    n_blocks = A_Q_HEADS // 2
    cast_in, cast_out, cast_shapes = _cast_specs(cast, b * n_blocks,
                                                 lambda i, c: i * n_blocks + c)
    k_col = A_Q_HEADS * HEAD_DIM // LANES
    bias = _band_bias(slopes, BLOCK, 1, strict=True)
    sink2 = sinks.astype(F32) * LOG2E
    stacked = _pair_rows(jnp.broadcast_to(sink2[:, None, None], (A_Q_HEADS, BLOCK, LANES)))
    by_lane = jnp.broadcast_to(jnp.repeat(sink2, HEAD_DIM).reshape(n_blocks, 1, LANES),
                               (n_blocks, BLOCK, LANES))
    sink = jnp.concatenate([stacked, by_lane], axis=1)
    return pl.pallas_call(
        functools.partial(_swa_body, n_cast=len(cast)),
        grid=(b, n_blocks),
        in_specs=[
            pl.BlockSpec((1, s, LANES), lambda i, c: (i, 0, c)),
            pl.BlockSpec((1, s, LANES), lambda i, c: (i, 0, k_col)),
            pl.BlockSpec((1, s, LANES), lambda i, c: (i, 0, k_col + 1)),
            pl.BlockSpec((1, 2, 2 * BLOCK, 2 * BLOCK), lambda i, c: (c, 0, 0, 0)),
            pl.BlockSpec((1, 3 * BLOCK, LANES), lambda i, c: (c, 0, 0)),
        ] + cast_in,
        out_specs=[pl.BlockSpec((1, s, LANES), lambda i, c: (i, 0, c))] + cast_out,
        out_shape=[jax.ShapeDtypeStruct((b, s, n_blocks * LANES), BF16)] + cast_shapes,
        scratch_shapes=[pltpu.VMEM((s, LANES), BF16)] * 3,
        compiler_params=_params("arbitrary", "arbitrary"),
        name="swa_attention",
    )(proj, proj, proj, bias, sink, *[w for w, _ in cast])


def _sb_softplus(z):
    return jnp.maximum(z, 0.0) + jnp.log2(1.0 + jnp.exp2(-jnp.abs(z)))


def _hi_lo(x):
    hi = x.astype(BF16)
    lo = (x - hi.astype(F32)).astype(BF16)
    return jnp.concatenate([hi, lo], axis=1)


def _sb_weighted_values(w, va, vb):
    return _dot(jnp.concatenate([w[:BLOCK], w[BLOCK:]], axis=1),
                jnp.concatenate([va, vb], axis=0))


def _sb_band_scores(lhs, k_band, first_pen, causal):
    z = _dot_t(lhs, k_band)
    z_prev, z_diag = z[:, :BLOCK] + first_pen, z[:, BLOCK:]
    drop = jnp.concatenate([_sb_softplus(z_prev), jnp.where(causal, _sb_softplus(z_diag), 0.0)],
                           axis=1)
    return z_prev, z_diag, _hi_lo(drop)


def _sb_band_weights(z_prev, z_diag, drop_hi_lo, u_band, causal):
    cs = _dot(drop_hi_lo, u_band)
    w = jnp.concatenate([jnp.exp2(z_prev - cs[:, :BLOCK]),
                         jnp.where(causal, jnp.exp2(z_diag - cs[:, BLOCK:]), 0.0)], axis=1)
    return w.astype(BF16), jnp.broadcast_to(cs[:, :1], (2 * BLOCK, LANES))


def _sb_body(*refs, pairs, q_unroll, n_cast):
    ins, cast_in, o_ref, cast_out, scratch = _split_refs(refs, 5, n_cast)
    q_ref, k_ref, v_ref, ub_ref, u_ref = ins
    va_scr, vb_scr, acc_scr, run_scr = scratch
    _cast_rows(cast_in, cast_out)
    s = q_ref.shape[1]
    lo = lax.broadcasted_iota(jnp.int32, (s, pairs * LANES), 1) % LANES < HEAD_DIM
    v_all = v_ref[0]
    va_scr[...] = jnp.where(lo, v_all, jnp.zeros_like(v_all))
    vb_scr[...] = jnp.where(lo, jnp.zeros_like(v_all), v_all)
    causal = (lax.broadcasted_iota(jnp.int32, (2 * BLOCK, BLOCK), 1)
              < lax.broadcasted_iota(jnp.int32, (2 * BLOCK, BLOCK), 0) % BLOCK)
    k2 = k_ref.at[0]
    lanes = [slice(g * LANES, (g + 1) * LANES) for g in range(pairs)]

    def q_group(i, carry):
        blocks = [i * q_unroll + qi for qi in range(q_unroll)]
        tiles = [(qi, n, g) for qi, n in enumerate(blocks) for g in range(pairs)]
        scores = {}
        for qi, n, g in tiles:
            rows = pl.ds(pl.multiple_of(n * BLOCK, BLOCK), BLOCK)
            lhs = _stack_heads(q_ref[0, rows, lanes[g]])
            scores[qi, g] = _sb_band_scores(lhs, _band_rows(k2, n, lanes[g]),
                                            jnp.where(n > 0, 0.0, MASKED), causal)
        weights = {}
        for qi, n, g in tiles:
            weights[qi, g] = _sb_band_weights(*scores[qi, g], ub_ref[...], causal)
        decay = [jnp.float32(jnp.inf)] * q_unroll
        for qi, n, g in tiles:
            w, total = weights[qi, g]
            acc_scr[qi, g] = _sb_weighted_values(w, _band_rows(va_scr, n, lanes[g]),
                                                 _band_rows(vb_scr, n, lanes[g]))
            run_scr[qi, g] = total
            decay[qi] = jnp.minimum(decay[qi], jnp.min(total))

        for qi, n in enumerate(blocks):
            rows = pl.ds(pl.multiple_of(n * BLOCK, BLOCK), BLOCK)

            def more(st):
                j, least = st
                return jnp.logical_and(j >= 0, least < SB_DECAY_DONE)

            def key_block(st, qi=qi, rows=rows):
                j, _ = st
                keys = pl.ds(pl.multiple_of(j * BLOCK, BLOCK), BLOCK)
                zs = [_dot_t(_stack_heads(q_ref[0, rows, lanes[g]]), k_ref[0, keys, lanes[g]])
                      for g in range(pairs)]
                cs = [_dot(_hi_lo(_sb_softplus(z)), u_ref[...]) for z in zs]
                least = jnp.float32(jnp.inf)
                for g in range(pairs):
                    run = run_scr[qi, g]
                    w = jnp.exp2(zs[g] - cs[g][:, :BLOCK] - run).astype(BF16)
                    acc_scr[qi, g] += _sb_weighted_values(w, va_scr[keys, lanes[g]],
                                                          vb_scr[keys, lanes[g]])
                    run = run + cs[g][:, BLOCK:]
                    run_scr[qi, g] = run
                    least = jnp.minimum(least, jnp.min(run))
                return j - 1, least

            lax.while_loop(more, key_block, (n - 2, decay[qi]))
            for g in range(pairs):
                o_ref[0, rows, lanes[g]] = acc_scr[qi, g].astype(BF16)
        return carry

    lax.fori_loop(0, s // BLOCK // q_unroll, q_group, 0)


def _sb_attention(proj, q_col, k_col, v_col, pairs, q_unroll, cast=()):
    b, s, _ = proj.shape
    width = pairs * LANES
    n_steps = B_HEADS // 2 // pairs
    cast_in, cast_out, cast_shapes = _cast_specs(cast, b * n_steps, lambda i, c: i * n_steps + c)
    assert q_col % pairs == 0 and k_col % pairs == 0 and v_col % pairs == 0
    assert (s // BLOCK) % q_unroll == 0

    def suffix(n):
        kk = jnp.arange(n)
        return (kk[:, None] >= kk[None, :]).astype(BF16)

    u_band = jnp.tile(suffix(2 * BLOCK), (2, 1))
    u_block = jnp.tile(jnp.concatenate([suffix(BLOCK), jnp.ones((BLOCK, BLOCK), BF16)], axis=1),
                       (2, 1))
    return pl.pallas_call(
        functools.partial(_sb_body, pairs=pairs, q_unroll=q_unroll, n_cast=len(cast)),
        grid=(b, n_steps),
        in_specs=[
            pl.BlockSpec((1, s, width), lambda i, c: (i, 0, q_col // pairs + c)),
            pl.BlockSpec((1, s, width), lambda i, c: (i, 0, k_col // pairs + c)),
            pl.BlockSpec((1, s, width), lambda i, c: (i, 0, v_col // pairs + c)),
            pl.BlockSpec((4 * BLOCK, 2 * BLOCK), lambda i, c: (0, 0)),
            pl.BlockSpec((2 * BLOCK, 2 * BLOCK), lambda i, c: (0, 0)),
        ] + cast_in,
        out_specs=[pl.BlockSpec((1, s, width), lambda i, c: (i, 0, c))] + cast_out,
        out_shape=[jax.ShapeDtypeStruct((b, s, B_HEADS * HEAD_DIM), BF16)] + cast_shapes,
        scratch_shapes=[pltpu.VMEM((s, width), BF16), pltpu.VMEM((s, width), BF16),
                        pltpu.VMEM((q_unroll, pairs, BLOCK, LANES), F32),
                        pltpu.VMEM((q_unroll, pairs, 2 * BLOCK, LANES), F32)],
        compiler_params=_params("arbitrary", "arbitrary"),
        name="stick_breaking",
    )(proj, proj, proj, u_band, u_block, *[w for w, _ in cast])


RESIDUE_STEP = 4


def _to_residue_major(src, dst, seg):
    part = seg // RESIDUE_STEP
    for base in range(0, src.shape[0], seg):
        for r in range(RESIDUE_STEP):
            dst[base + r * part:base + (r + 1) * part, :] = (
                src[pl.ds(base + r, part, stride=RESIDUE_STEP), :])


def _from_residue_major(src, dst, seg):
    part = seg // RESIDUE_STEP
    for base in range(0, src.shape[0], seg):
        for r in range(RESIDUE_STEP):
            dst[pl.ds(base + r, part, stride=RESIDUE_STEP), :] = (
                src[base + r * part:base + (r + 1) * part, :])


def _dil_body(*refs, n_cast):
    ins, cast_in, o_ref, cast_out, scratch = _split_refs(refs, 4, n_cast)
    q_ref, k_ref, v_ref, bias_ref = ins
    tmp_a, tmp_b, q4, k4, q16, k16, va1, vb1, va4, vb4, va16, vb16, st_a, st_b = scratch
    _cast_rows(cast_in, cast_out)
    s = q_ref.shape[1]
    va1[...], vb1[...] = _split_heads(v_ref[0])
    for src, d4, d16 in ((q_ref, (q4,), (q16,)), (k_ref, (k4,), (k16,)),
                         (v_ref, (va4, vb4), (va16, vb16))):
        tmp_a[...] = src[0].astype(F32)
        _to_residue_major(tmp_a, tmp_b, s)
        _to_residue_major(tmp_b, tmp_a, s // RESIDUE_STEP)
        for dsts, tmp in ((d4, tmp_b), (d16, tmp_a)):
            vals = tmp[...].astype(BF16)
            if len(dsts) == 1:
                dsts[0][...] = vals
            else:
                dsts[0][...], dsts[1][...] = _split_heads(vals)

    layouts = ((q_ref.at[0], k_ref.at[0], va1, vb1), (q4, k4, va4, vb4), (q16, k16, va16, vb16))
    state, spare = st_a, st_b
    for step, branch in enumerate(reversed(range(len(C_PAIRS)))):
        qb, kb, va, vb = layouts[branch]
        class_blocks = s // C_PAIRS[branch][1] // BLOCK
        if step > 0:
            for a in range(3):
                _from_residue_major(state.at[a], spare.at[a],
                                    s // RESIDUE_STEP if step == 1 else s)
            state, spare = spare, state

        def tile(t, carry, step=step, branch=branch, qb=qb, kb=kb, va=va, vb=vb, state=state,
                 class_blocks=class_blocks):
            rows = pl.ds(pl.multiple_of(t * BLOCK, BLOCK), BLOCK)
            lhs = _stack_heads(qb[rows, :])
            first = jnp.where(t % class_blocks == 0, 1, 0)
            num, m, den = _band_softmax(lhs, _band_rows(kb, t), _band_rows(va, t),
                                        _band_rows(vb, t), bias_ref[branch, 0, first])
            if step > 0:
                m_old = state[0, rows, :]
                m_new = jnp.maximum(m_old, m)
                a_old = jnp.exp2(m_old - m_new)
                a_cur = jnp.exp2(m - m_new)
                num = a_old * state[2, rows, :] + a_cur * num
                den = a_old * state[1, rows, :] + a_cur * den
                m = m_new
            if branch == 0:
                o_ref[0, rows, :] = (num / den).astype(BF16)
            else:
                state[0, rows, :] = m
                state[1, rows, :] = den
                state[2, rows, :] = num
            return carry

        lax.fori_loop(0, s // BLOCK, tile, 0, unroll=TILE_UNROLL)


def _dilated_attention(proj, slopes, cast=()):
    b, s, _ = proj.shape
    n_blocks = C_HEADS // 2
    cast_in, cast_out, cast_shapes = _cast_specs(cast, b * n_blocks,
                                                 lambda i, c: i * n_blocks + c)
    assert C_PAIRS[0][1] == 1 and C_PAIRS[1][1] == RESIDUE_STEP
    assert C_PAIRS[2][1] == RESIDUE_STEP ** 2
    bias = jnp.stack([_band_bias(slopes, w // d, d, strict=False)
                      for (w, d) in C_PAIRS])
    scr = ([pltpu.VMEM((s, LANES), F32)] * 2 + [pltpu.VMEM((s, LANES), BF16)] * 10
           + [pltpu.VMEM((3, s, LANES), F32)] * 2)
    return pl.pallas_call(
        functools.partial(_dil_body, n_cast=len(cast)),
        grid=(b, n_blocks),
        in_specs=[
            pl.BlockSpec((1, s, LANES), lambda i, c: (i, 0, c)),
            pl.BlockSpec((1, s, LANES), lambda i, c: (i, 0, n_blocks + c)),
            pl.BlockSpec((1, s, LANES), lambda i, c: (i, 0, 2 * n_blocks + c)),
            pl.BlockSpec((len(C_PAIRS), 1, 2, 2 * BLOCK, 2 * BLOCK),
                         lambda i, c: (0, c, 0, 0, 0)),
        ] + cast_in,
        out_specs=[pl.BlockSpec((1, s, LANES), lambda i, c: (i, 0, c))] + cast_out,
        out_shape=[jax.ShapeDtypeStruct((b, s, n_blocks * LANES), BF16)] + cast_shapes,
        scratch_shapes=scr,
        compiler_params=_params("arbitrary", "arbitrary"),
        name="dilated_mixture",
    )(proj, proj, proj, bias, *[w for w, _ in cast])


def _alibi_slopes(n):
    return jnp.exp2(-8.0 * jnp.arange(1, n + 1, dtype=F32) / n)


def kernel(x, attn_norm, ffn_norm, even_w_in, even_q_norm, even_k_norm, even_sinks, even_w_out,
           odd_w_in, odd_q_norm, odd_k_norm, odd_w_out, ffn_w_gate, ffn_w_up, ffn_w_down):
    b, s, d = x.shape
    depth = attn_norm.shape[0]
    scale = HEAD_DIM ** -0.5
    scale2 = scale * LOG2E
    slopes_a = _alibi_slopes(A_Q_HEADS)
    slopes_c = _alibi_slopes(C_HEADS)
    qa, kva, hb = A_Q_HEADS * HEAD_DIM, A_KV_HEADS * HEAD_DIM, B_HEADS * HEAD_DIM
    hc = C_HEADS * HEAD_DIM
    ones = lambda n: jnp.ones((n,), F32)
    zeros = lambda n: jnp.zeros((n,), F32)

    ffn_cast = lambda i: [(ffn_w_gate, i), (ffn_w_up, i), (ffn_w_down, i)]
    w_in = even_w_in[0].astype(BF16)

    x2 = x.reshape(b * s, d)
    for i in range(depth):
        j = i // 2
        if i % 2 == 0:
            colgain = jnp.concatenate([
                jnp.tile(even_q_norm[j].astype(F32), A_Q_HEADS) * scale2,
                jnp.tile(even_k_norm[j].astype(F32), A_KV_HEADS),
                ones(kva), ones(hb) * scale2, ones(hb), ones(hb)])
            colflag = jnp.concatenate([ones(qa + kva), zeros(kva + 3 * hb)])
            proj = _norm_proj(x2, attn_norm[i], w_in, colgain, colflag,
                              qa + kva, PROJ_ROWS, PROJ_CHUNK_EVEN).reshape(b, s, -1)
            oa, w_gate = _swa_attention(proj, even_sinks[j], slopes_a, cast=ffn_cast(i)[:1])
            qb_col = (qa + 2 * kva) // LANES
            ob, w_up, w_down, w_out, w_in = _sb_attention(
                proj, qb_col, qb_col + hb // LANES, qb_col + 2 * hb // LANES, SB_PAIRS, SB_Q_UNROLL,
                cast=ffn_cast(i)[1:] + [(even_w_out, j), (odd_w_in, j)])
            acts = [oa.reshape(b * s, qa), ob.reshape(b * s, hb)]
        else:
            colgain = jnp.concatenate([
                jnp.tile(odd_q_norm[j].astype(F32), C_HEADS) * scale2,
                jnp.tile(odd_k_norm[j].astype(F32), C_HEADS), ones(hc)])
            colflag = jnp.concatenate([ones(2 * hc), zeros(hc)])
            proj = _norm_proj(x2, attn_norm[i], w_in, colgain, colflag,
                              2 * hc, PROJ_ROWS, PROJ_CHUNK_ODD).reshape(b, s, -1)
            cast = ffn_cast(i) + [(odd_w_out, j)]
            if i + 1 < depth:
                cast.append((even_w_in, j + 1))
            oc, w_gate, w_up, w_down, w_out, *rest = _dilated_attention(proj, slopes_c, cast=cast)
            w_in = rest[0] if rest else None
            acts = [oc.reshape(b * s, hc)]
        x2 = _out_proj(x2, acts, w_out, OUT_ROWS)
        x2 = _ffn(x2, ffn_norm[i], w_gate, w_up, w_down, FFN_ROWS, FFN_COLS)
    return x2.reshape(b, s, d)
```

```python
import functools

import jax
import jax.numpy as jnp
from jax import lax
from jax.experimental import pallas as pl
from jax.experimental.pallas import tpu as pltpu

HEAD_DIM = 64
LANES = 128
BLOCK = 128
A_Q_HEADS = 16
A_KV_HEADS = 2
B_HEADS = 16
C_HEADS = 32
C_PAIRS = ((128, 1), (512, 4), (2048, 16))
RMS_EPS = 1e-6
MASKED = -1e30
LOG2E = 1.4426950408889634
SB_DECAY_DONE = 127.0
VMEM_LIMIT = 56 * 1024 * 1024
MXU_WIDTH = 256
BF16_SUBLANES = 16
TILE_UNROLL = 32
SB_PAIRS = 2
SB_Q_UNROLL = 4
PROJ_ROWS = 512
PROJ_CHUNK_EVEN = 1024
PROJ_CHUNK_ODD = 512
OUT_ROWS = 512
OUT_CHUNK = 1024
FFN_ROWS = 1024
FFN_COLS = 512

F32 = jnp.float32
BF16 = jnp.bfloat16


def _dot(a, b):
    return jnp.dot(a, b, preferred_element_type=F32)


def _dot_t(a, b):
    return lax.dot_general(a, b, (((1,), (1,)), ((), ())), preferred_element_type=F32)


def _params(*sem):
    return pltpu.CompilerParams(dimension_semantics=sem, vmem_limit_bytes=VMEM_LIMIT)


def _cast_specs(passengers, n_steps, step_of):
    in_specs, out_specs, out_shapes = [], [], []
    for w, layer in passengers:
        _, rows, cols = w.shape
        n_col = next(c for c in (1, 2, 4, 8) if n_steps % c == 0
                     and rows % (BF16_SUBLANES * (n_steps // c)) == 0
                     and cols % (LANES * c) == 0)
        slab = (rows // (n_steps // n_col), cols // n_col)
        in_specs.append(pl.BlockSpec(
            (1,) + slab,
            lambda *g, layer=layer, n_col=n_col: (layer, step_of(*g) // n_col, step_of(*g) % n_col)))
        out_specs.append(pl.BlockSpec(
            slab, lambda *g, n_col=n_col: (step_of(*g) // n_col, step_of(*g) % n_col)))
        out_shapes.append(jax.ShapeDtypeStruct((rows, cols), BF16))
    return in_specs, out_specs, out_shapes


def _cast_rows(src_refs, dst_refs):
    for src, dst in zip(src_refs, dst_refs):
        dst[...] = src[0].astype(BF16)


def _split_refs(refs, n_in, n_cast):
    a, b, c = n_in + n_cast, n_in + n_cast + 1, n_in + 2 * n_cast + 1
    return refs[:n_in], refs[n_in:a], refs[a], refs[b:c], refs[c:]


def _proj_body(x_ref, g_ref, w_ref, cg_ref, cf_ref, s_ref, o_ref, *, n_norm_cols, chunk):
    x = x_ref[...]
    inv = lax.rsqrt(jnp.mean(x * x, axis=-1, keepdims=True) + RMS_EPS)
    h = ((x * inv) * g_ref[...]).astype(BF16)
    n_out = o_ref.shape[1]
    for c0 in range(0, n_out, chunk):
        width = min(chunk, n_out - c0)
        y = _dot(h, w_ref[:, c0:c0 + width])
        for c in range(c0, c0 + width, MXU_WIDTH):
            sl = slice(c, c + MXU_WIDTH)
            yc = y[:, c - c0:c - c0 + MXU_WIDTH]
            cg = cg_ref[:, sl]
            if c < n_norm_cols:
                ss = _dot((yc * yc).astype(BF16), s_ref[...])
                inv_h = lax.rsqrt(ss * (1.0 / HEAD_DIM) + RMS_EPS)
                scale = jnp.where(cf_ref[:, sl] > 0.0, inv_h * cg, cg)
            else:
                scale = cg
            o_ref[:, sl] = (yc * scale).astype(BF16)


def _norm_proj(x2, gain, w, colgain, colflag, n_norm_cols, tm, chunk):
    n, d = x2.shape
    n_out = w.shape[1]
    assert n % tm == 0 and n_out % MXU_WIDTH == 0 and chunk % MXU_WIDTH == 0
    idx = jnp.arange(MXU_WIDTH) // HEAD_DIM
    seg = (idx[:, None] == idx[None, :]).astype(BF16)
    const = lambda shape: pl.BlockSpec(shape, lambda i: (0, 0), pipeline_mode=pl.Buffered(1))
    return pl.pallas_call(
        functools.partial(_proj_body, n_norm_cols=n_norm_cols, chunk=chunk),
        grid=(n // tm,),
        in_specs=[
            pl.BlockSpec((tm, d), lambda i: (i, 0)),
            const((1, d)),
            const((d, n_out)),
            const((1, n_out)),
            const((1, n_out)),
            const((MXU_WIDTH, MXU_WIDTH)),
        ],
        out_specs=pl.BlockSpec((tm, n_out), lambda i: (i, 0)),
        out_shape=jax.ShapeDtypeStruct((n, n_out), BF16),
        compiler_params=_params("parallel"),
        name="norm_proj",
    )(x2, gain.reshape(1, d), w, colgain.reshape(1, n_out), colflag.reshape(1, n_out), seg)


def _out_body(*refs, n_pairs):
    x_ref = refs[0]
    o_ref = refs[1 + 2 * n_pairs]
    for c0 in range(0, o_ref.shape[1], OUT_CHUNK):
        sl = slice(c0, c0 + OUT_CHUNK)
        acc = x_ref[:, sl]
        for p in range(n_pairs):
            acc = acc + _dot(refs[1 + 2 * p][...], refs[2 + 2 * p][:, sl])
        o_ref[:, sl] = acc


def _out_proj(x2, acts, w, tm):
    n, d = x2.shape
    in_specs = [pl.BlockSpec((tm, d), lambda i: (i, 0))]
    args = [x2]
    offset = 0
    for a in acts:
        kk = a.shape[1]
        assert offset % kk == 0
        in_specs.append(pl.BlockSpec((tm, kk), lambda i: (i, 0)))
        in_specs.append(pl.BlockSpec((kk, d), lambda i, r=offset // kk: (r, 0)))
        args += [a, w]
        offset += kk
    assert offset == w.shape[0] and n % tm == 0
    return pl.pallas_call(
        functools.partial(_out_body, n_pairs=len(acts)),
        grid=(n // tm,),
        in_specs=in_specs,
        out_specs=pl.BlockSpec((tm, d), lambda i: (i, 0)),
        out_shape=jax.ShapeDtypeStruct((n, d), F32),
        compiler_params=_params("parallel"),
        name="out_proj",
    )(*args)


def _ffn_body(x_ref, g_ref, wg_ref, wu_ref, wd_ref, o_ref, h_scr):
    j = pl.program_id(1)

    @pl.when(j == 0)
    def _():
        x = x_ref[...]
        inv = lax.rsqrt(jnp.mean(x * x, axis=-1, keepdims=True) + RMS_EPS)
        h_scr[...] = ((x * inv) * g_ref[...]).astype(BF16)
        o_ref[...] = x

    h = h_scr[...]
    gate = _dot(h, wg_ref[...])
    up = _dot(h, wu_ref[...])
    act = (gate * (1.0 / (1.0 + jnp.exp(-gate)))) * up
    o_ref[...] += _dot(act.astype(BF16), wd_ref[...])


def _ffn(x2, gain, wg, wu, wd, tm, tf):
    n, d = x2.shape
    dff = wg.shape[1]
    assert n % tm == 0 and dff % tf == 0
    return pl.pallas_call(
        _ffn_body,
        grid=(n // tm, dff // tf),
        in_specs=[
            pl.BlockSpec((tm, d), lambda i, j: (i, 0)),
            pl.BlockSpec((1, d), lambda i, j: (0, 0)),
            pl.BlockSpec((d, tf), lambda i, j: (0, j)),
            pl.BlockSpec((d, tf), lambda i, j: (0, j)),
            pl.BlockSpec((tf, d), lambda i, j: (j, 0)),
        ],
        out_specs=pl.BlockSpec((tm, d), lambda i, j: (i, 0)),
        out_shape=jax.ShapeDtypeStruct((n, d), F32),
        scratch_shapes=[pltpu.VMEM((tm, d), BF16)],
        compiler_params=_params("parallel", "arbitrary"),
        name="ffn",
    )(x2, gain.reshape(1, d), wg, wu, wd)


def _stack_heads(q):
    lo = lax.broadcasted_iota(jnp.int32, q.shape, 1) < HEAD_DIM
    zero = jnp.zeros_like(q)
    return jnp.concatenate([jnp.where(lo, q, zero), jnp.where(lo, zero, q)], axis=0)


def _unstack_heads(t):
    lo = lax.broadcasted_iota(jnp.int32, (BLOCK, LANES), 1) < HEAD_DIM
    return jnp.where(lo, t[:BLOCK], t[BLOCK:])


def _split_heads(v):
    lo = lax.broadcasted_iota(jnp.int32, v.shape, 1) < HEAD_DIM
    zero = jnp.zeros_like(v)
    return jnp.where(lo, v, zero), jnp.where(lo, zero, v)


def _band_rows(ref, n, lanes=slice(None)):
    r0 = pl.multiple_of(n * BLOCK, BLOCK)
    p0 = pl.multiple_of(jnp.maximum(n - 1, 0) * BLOCK, BLOCK)
    return jnp.concatenate([ref[pl.ds(p0, BLOCK), lanes], ref[pl.ds(r0, BLOCK), lanes]], axis=0)


def _band_softmax(lhs, k_band, va_band, vb_band, bias, sink=None):
    s = _dot_t(lhs, k_band) + bias
    m = jnp.broadcast_to(jnp.max(s, axis=1, keepdims=True), (2 * BLOCK, LANES))
    if sink is not None:
        m = jnp.maximum(m, sink)
    p = jnp.exp2(s - jnp.concatenate([m, m], axis=1)).astype(BF16)
    p_cat = jnp.concatenate([p[:BLOCK], p[BLOCK:]], axis=1)
    lo = lax.broadcasted_iota(jnp.int32, (2 * BLOCK, LANES), 1) < HEAD_DIM
    ones_a = jnp.where(lo, 1.0, 0.0).astype(BF16)
    ones_b = jnp.where(lo, 0.0, 1.0).astype(BF16)
    rhs = jnp.concatenate([jnp.concatenate([va_band, vb_band], axis=0),
                           jnp.concatenate([ones_a, ones_b], axis=0)], axis=1)
    ext = _dot(p_cat, rhs)
    return ext[:, :LANES], _unstack_heads(m), ext[:, LANES:]


def _band_bias(slopes, window_max, dist_scale, strict):
    i = jnp.arange(BLOCK)[:, None]
    j = jnp.arange(2 * BLOCK)[None, :]
    dist = i + BLOCK - j
    valid = (dist >= 0) & ((dist < window_max) if strict else (dist <= window_max))
    alibi = (slopes[:, None, None] * (dist * dist_scale).astype(F32)[None]) * LOG2E
    h = slopes.shape[0]
    table = jnp.stack([jnp.where(valid[None], -alibi, MASKED),
                       jnp.where((valid & (j >= BLOCK))[None], -alibi, MASKED)], axis=1)
    table = table.reshape(h // 2, 2, 2, BLOCK, 2 * BLOCK).transpose(0, 2, 1, 3, 4)
    return table.reshape(h // 2, 2, 2 * BLOCK, 2 * BLOCK)


def _pair_rows(t):
    h = t.shape[0]
    return t.reshape((h // 2, 2 * t.shape[1]) + t.shape[2:])


def _swa_body(*refs, n_cast):
    ins, cast_in, o_ref, cast_out, (kd_scr, va_scr, vb_scr) = _split_refs(refs, 5, n_cast)
    q_ref, k_ref, v_ref, bias_ref, sink_ref = ins
    _cast_rows(cast_in, cast_out)
    s = q_ref.shape[1]
    c = pl.program_id(1)
    blocks_per_kv = A_Q_HEADS // A_KV_HEADS // 2
    kv_head = c // blocks_per_kv

    @pl.when(c % blocks_per_kv == 0)
    def _():
        lane_head = lax.broadcasted_iota(jnp.int32, (s, LANES), 1) // HEAD_DIM
        sel = lane_head == kv_head
        kf = jnp.where(sel, k_ref[0].astype(F32), 0.0)
        kd_scr[...] = (kf + pltpu.roll(kf, HEAD_DIM, axis=1)).astype(BF16)
        vf = jnp.where(sel, v_ref[0].astype(F32), 0.0)
        vr = pltpu.roll(vf, HEAD_DIM, axis=1)
        va_scr[...] = jnp.where(kv_head == 0, vf, vr).astype(BF16)
        vb_scr[...] = jnp.where(kv_head == 0, vr, vf).astype(BF16)

    def block(n, carry):
        rows = pl.ds(pl.multiple_of(n * BLOCK, BLOCK), BLOCK)
        lhs = _stack_heads(q_ref[0, rows, :])
        first = jnp.where(n == 0, 1, 0)
        num, m, den = _band_softmax(lhs, _band_rows(kd_scr, n), _band_rows(va_scr, n),
                                    _band_rows(vb_scr, n), bias_ref[0, first],
                                    sink=sink_ref[0, :2 * BLOCK])
        den = den + jnp.exp2(sink_ref[0, 2 * BLOCK:] - m)
        o_ref[0, rows, :] = (num / den).astype(BF16)
        return carry

    lax.fori_loop(0, s // BLOCK, block, 0, unroll=TILE_UNROLL)


def _swa_attention(proj, sinks, slopes, cast=()):
    b, s, _ = proj.shape
    n_blocks = A_Q_HEADS // 2
    cast_in, cast_out, cast_shapes = _cast_specs(cast, b * n_blocks,
                                                 lambda i, c: i * n_blocks + c)
    k_col = A_Q_HEADS * HEAD_DIM // LANES
    bias = _band_bias(slopes, BLOCK, 1, strict=True)
    sink2 = sinks.astype(F32) * LOG2E
    stacked = _pair_rows(jnp.broadcast_to(sink2[:, None, None], (A_Q_HEADS, BLOCK, LANES)))
    by_lane = jnp.broadcast_to(jnp.repeat(sink2, HEAD_DIM).reshape(n_blocks, 1, LANES),
                               (n_blocks, BLOCK, LANES))
    sink = jnp.concatenate([stacked, by_lane], axis=1)
    return pl.pallas_call(
        functools.partial(_swa_body, n_cast=len(cast)),
        grid=(b, n_blocks),
        in_specs=[
            pl.BlockSpec((1, s, LANES), lambda i, c: (i, 0, c)),
            pl.BlockSpec((1, s, LANES), lambda i, c: (i, 0, k_col)),
            pl.BlockSpec((1, s, LANES), lambda i, c: (i, 0, k_col + 1)),
            pl.BlockSpec((1, 2, 2 * BLOCK, 2 * BLOCK), lambda i, c: (c, 0, 0, 0)),
            pl.BlockSpec((1, 3 * BLOCK, LANES), lambda i, c: (c, 0, 0)),
        ] + cast_in,
        out_specs=[pl.BlockSpec((1, s, LANES), lambda i, c: (i, 0, c))] + cast_out,
        out_shape=[jax.ShapeDtypeStruct((b, s, n_blocks * LANES), BF16)] + cast_shapes,
        scratch_shapes=[pltpu.VMEM((s, LANES), BF16)] * 3,
        compiler_params=_params("arbitrary", "arbitrary"),
        name="swa_attention",
    )(proj, proj, proj, bias, sink, *[w for w, _ in cast])


def _sb_softplus(z):
    return jnp.maximum(z, 0.0) + jnp.log2(1.0 + jnp.exp2(-jnp.abs(z)))


def _hi_lo(x):
    hi = x.astype(BF16)
    lo = (x - hi.astype(F32)).astype(BF16)
    return jnp.concatenate([hi, lo], axis=1)


def _sb_weighted_values(w, va, vb):
    return _dot(jnp.concatenate([w[:BLOCK], w[BLOCK:]], axis=1),
                jnp.concatenate([va, vb], axis=0))


def _sb_band_scores(lhs, k_band, first_pen, causal):
    z = _dot_t(lhs, k_band)
    z_prev, z_diag = z[:, :BLOCK] + first_pen, z[:, BLOCK:]
    drop = jnp.concatenate([_sb_softplus(z_prev), jnp.where(causal, _sb_softplus(z_diag), 0.0)],
                           axis=1)
    return z_prev, z_diag, _hi_lo(drop)


def _sb_band_weights(z_prev, z_diag, drop_hi_lo, u_band, causal):
    cs = _dot(drop_hi_lo, u_band)
    w = jnp.concatenate([jnp.exp2(z_prev - cs[:, :BLOCK]),
                         jnp.where(causal, jnp.exp2(z_diag - cs[:, BLOCK:]), 0.0)], axis=1)
    return w.astype(BF16), jnp.broadcast_to(cs[:, :1], (2 * BLOCK, LANES))


def _sb_body(*refs, pairs, q_unroll, n_cast):
    ins, cast_in, o_ref, cast_out, scratch = _split_refs(refs, 5, n_cast)
    q_ref, k_ref, v_ref, ub_ref, u_ref = ins
    va_scr, vb_scr, acc_scr, run_scr = scratch
    _cast_rows(cast_in, cast_out)
    s = q_ref.shape[1]
    lo = lax.broadcasted_iota(jnp.int32, (s, pairs * LANES), 1) % LANES < HEAD_DIM
    v_all = v_ref[0]
    va_scr[...] = jnp.where(lo, v_all, jnp.zeros_like(v_all))
    vb_scr[...] = jnp.where(lo, jnp.zeros_like(v_all), v_all)
    causal = (lax.broadcasted_iota(jnp.int32, (2 * BLOCK, BLOCK), 1)
              < lax.broadcasted_iota(jnp.int32, (2 * BLOCK, BLOCK), 0) % BLOCK)
    k2 = k_ref.at[0]
    lanes = [slice(g * LANES, (g + 1) * LANES) for g in range(pairs)]

    def q_group(i, carry):
        blocks = [i * q_unroll + qi for qi in range(q_unroll)]
        tiles = [(qi, n, g) for qi, n in enumerate(blocks) for g in range(pairs)]
        scores = {}
        for qi, n, g in tiles:
            rows = pl.ds(pl.multiple_of(n * BLOCK, BLOCK), BLOCK)
            lhs = _stack_heads(q_ref[0, rows, lanes[g]])
            scores[qi, g] = _sb_band_scores(lhs, _band_rows(k2, n, lanes[g]),
                                            jnp.where(n > 0, 0.0, MASKED), causal)
        weights = {}
        for qi, n, g in tiles:
            weights[qi, g] = _sb_band_weights(*scores[qi, g], ub_ref[...], causal)
        decay = [jnp.float32(jnp.inf)] * q_unroll
        for qi, n, g in tiles:
            w, total = weights[qi, g]
            acc_scr[qi, g] = _sb_weighted_values(w, _band_rows(va_scr, n, lanes[g]),
                                                 _band_rows(vb_scr, n, lanes[g]))
            run_scr[qi, g] = total
            decay[qi] = jnp.minimum(decay[qi], jnp.min(total))

        for qi, n in enumerate(blocks):
            rows = pl.ds(pl.multiple_of(n * BLOCK, BLOCK), BLOCK)

            def more(st):
                j, least = st
                return jnp.logical_and(j >= 0, least < SB_DECAY_DONE)

            def key_block(st, qi=qi, rows=rows):
                j, _ = st
                keys = pl.ds(pl.multiple_of(j * BLOCK, BLOCK), BLOCK)
                zs = [_dot_t(_stack_heads(q_ref[0, rows, lanes[g]]), k_ref[0, keys, lanes[g]])
                      for g in range(pairs)]
                cs = [_dot(_hi_lo(_sb_softplus(z)), u_ref[...]) for z in zs]
                least = jnp.float32(jnp.inf)
                for g in range(pairs):
                    run = run_scr[qi, g]
                    w = jnp.exp2(zs[g] - cs[g][:, :BLOCK] - run).astype(BF16)
                    acc_scr[qi, g] += _sb_weighted_values(w, va_scr[keys, lanes[g]],
                                                          vb_scr[keys, lanes[g]])
                    run = run + cs[g][:, BLOCK:]
                    run_scr[qi, g] = run
                    least = jnp.minimum(least, jnp.min(run))
                return j - 1, least

            lax.while_loop(more, key_block, (n - 2, decay[qi]))
            for g in range(pairs):
                o_ref[0, rows, lanes[g]] = acc_scr[qi, g].astype(BF16)
        return carry

    lax.fori_loop(0, s // BLOCK // q_unroll, q_group, 0)


def _sb_attention(proj, q_col, k_col, v_col, pairs, q_unroll, cast=()):
    b, s, _ = proj.shape
    width = pairs * LANES
    n_steps = B_HEADS // 2 // pairs
    cast_in, cast_out, cast_shapes = _cast_specs(cast, b * n_steps, lambda i, c: i * n_steps + c)
    assert q_col % pairs == 0 and k_col % pairs == 0 and v_col % pairs == 0
    assert (s // BLOCK) % q_unroll == 0

    def suffix(n):
        kk = jnp.arange(n)
        return (kk[:, None] >= kk[None, :]).astype(BF16)

    u_band = jnp.tile(suffix(2 * BLOCK), (2, 1))
    u_block = jnp.tile(jnp.concatenate([suffix(BLOCK), jnp.ones((BLOCK, BLOCK), BF16)], axis=1),
                       (2, 1))
    return pl.pallas_call(
        functools.partial(_sb_body, pairs=pairs, q_unroll=q_unroll, n_cast=len(cast)),
        grid=(b, n_steps),
        in_specs=[
            pl.BlockSpec((1, s, width), lambda i, c: (i, 0, q_col // pairs + c)),
            pl.BlockSpec((1, s, width), lambda i, c: (i, 0, k_col // pairs + c)),
            pl.BlockSpec((1, s, width), lambda i, c: (i, 0, v_col // pairs + c)),
            pl.BlockSpec((4 * BLOCK, 2 * BLOCK), lambda i, c: (0, 0)),
            pl.BlockSpec((2 * BLOCK, 2 * BLOCK), lambda i, c: (0, 0)),
        ] + cast_in,
        out_specs=[pl.BlockSpec((1, s, width), lambda i, c: (i, 0, c))] + cast_out,
        out_shape=[jax.ShapeDtypeStruct((b, s, B_HEADS * HEAD_DIM), BF16)] + cast_shapes,
        scratch_shapes=[pltpu.VMEM((s, width), BF16), pltpu.VMEM((s, width), BF16),
                        pltpu.VMEM((q_unroll, pairs, BLOCK, LANES), F32),
                        pltpu.VMEM((q_unroll, pairs, 2 * BLOCK, LANES), F32)],
        compiler_params=_params("arbitrary", "arbitrary"),
        name="stick_breaking",
    )(proj, proj, proj, u_band, u_block, *[w for w, _ in cast])


RESIDUE_STEP = 4


def _to_residue_major(src, dst, seg):
    part = seg // RESIDUE_STEP
    for base in range(0, src.shape[0], seg):
        for r in range(RESIDUE_STEP):
            dst[base + r * part:base + (r + 1) * part, :] = (
                src[pl.ds(base + r, part, stride=RESIDUE_STEP), :])


def _from_residue_major(src, dst, seg):
    part = seg // RESIDUE_STEP
    for base in range(0, src.shape[0], seg):
        for r in range(RESIDUE_STEP):
            dst[pl.ds(base + r, part, stride=RESIDUE_STEP), :] = (
                src[base + r * part:base + (r + 1) * part, :])


def _dil_body(*refs, n_cast):
    ins, cast_in, o_ref, cast_out, scratch = _split_refs(refs, 4, n_cast)
    q_ref, k_ref, v_ref, bias_ref = ins
    tmp_a, tmp_b, q4, k4, q16, k16, va1, vb1, va4, vb4, va16, vb16, st_a, st_b = scratch
    _cast_rows(cast_in, cast_out)
    s = q_ref.shape[1]
    va1[...], vb1[...] = _split_heads(v_ref[0])
    for src, d4, d16 in ((q_ref, (q4,), (q16,)), (k_ref, (k4,), (k16,)),
                         (v_ref, (va4, vb4), (va16, vb16))):
        tmp_a[...] = src[0].astype(F32)
        _to_residue_major(tmp_a, tmp_b, s)
        _to_residue_major(tmp_b, tmp_a, s // RESIDUE_STEP)
        for dsts, tmp in ((d4, tmp_b), (d16, tmp_a)):
            vals = tmp[...].astype(BF16)
            if len(dsts) == 1:
                dsts[0][...] = vals
            else:
                dsts[0][...], dsts[1][...] = _split_heads(vals)

    layouts = ((q_ref.at[0], k_ref.at[0], va1, vb1), (q4, k4, va4, vb4), (q16, k16, va16, vb16))
    state, spare = st_a, st_b
    for step, branch in enumerate(reversed(range(len(C_PAIRS)))):
        qb, kb, va, vb = layouts[branch]
        class_blocks = s // C_PAIRS[branch][1] // BLOCK
        if step > 0:
            for a in range(3):
                _from_residue_major(state.at[a], spare.at[a],
                                    s // RESIDUE_STEP if step == 1 else s)
            state, spare = spare, state

        def tile(t, carry, step=step, branch=branch, qb=qb, kb=kb, va=va, vb=vb, state=state,
                 class_blocks=class_blocks):
            rows = pl.ds(pl.multiple_of(t * BLOCK, BLOCK), BLOCK)
            lhs = _stack_heads(qb[rows, :])
            first = jnp.where(t % class_blocks == 0, 1, 0)
            num, m, den = _band_softmax(lhs, _band_rows(kb, t), _band_rows(va, t),
                                        _band_rows(vb, t), bias_ref[branch, 0, first])
            if step > 0:
                m_old = state[0, rows, :]
                m_new = jnp.maximum(m_old, m)
                a_old = jnp.exp2(m_old - m_new)
                a_cur = jnp.exp2(m - m_new)
                num = a_old * state[2, rows, :] + a_cur * num
                den = a_old * state[1, rows, :] + a_cur * den
                m = m_new
            if branch == 0:
                o_ref[0, rows, :] = (num / den).astype(BF16)
            else:
                state[0, rows, :] = m
                state[1, rows, :] = den
                state[2, rows, :] = num
            return carry

        lax.fori_loop(0, s // BLOCK, tile, 0, unroll=TILE_UNROLL)


def _dilated_attention(proj, slopes, cast=()):
    b, s, _ = proj.shape
    n_blocks = C_HEADS // 2
    cast_in, cast_out, cast_shapes = _cast_specs(cast, b * n_blocks,
                                                 lambda i, c: i * n_blocks + c)
    assert C_PAIRS[0][1] == 1 and C_PAIRS[1][1] == RESIDUE_STEP
    assert C_PAIRS[2][1] == RESIDUE_STEP ** 2
    bias = jnp.stack([_band_bias(slopes, w // d, d, strict=False)
                      for (w, d) in C_PAIRS])
    scr = ([pltpu.VMEM((s, LANES), F32)] * 2 + [pltpu.VMEM((s, LANES), BF16)] * 10
           + [pltpu.VMEM((3, s, LANES), F32)] * 2)
    return pl.pallas_call(
        functools.partial(_dil_body, n_cast=len(cast)),
        grid=(b, n_blocks),
        in_specs=[
            pl.BlockSpec((1, s, LANES), lambda i, c: (i, 0, c)),
            pl.BlockSpec((1, s, LANES), lambda i, c: (i, 0, n_blocks + c)),
            pl.BlockSpec((1, s, LANES), lambda i, c: (i, 0, 2 * n_blocks + c)),
            pl.BlockSpec((len(C_PAIRS), 1, 2, 2 * BLOCK, 2 * BLOCK),
                         lambda i, c: (0, c, 0, 0, 0)),
        ] + cast_in,
        out_specs=[pl.BlockSpec((1, s, LANES), lambda i, c: (i, 0, c))] + cast_out,
        out_shape=[jax.ShapeDtypeStruct((b, s, n_blocks * LANES), BF16)] + cast_shapes,
        scratch_shapes=scr,
        compiler_params=_params("arbitrary", "arbitrary"),
        name="dilated_mixture",
    )(proj, proj, proj, bias, *[w for w, _ in cast])


def _alibi_slopes(n):
    return jnp.exp2(-8.0 * jnp.arange(1, n + 1, dtype=F32) / n)


def kernel(x, attn_norm, ffn_norm, even_w_in, even_q_norm, even_k_norm, even_sinks, even_w_out,
           odd_w_in, odd_q_norm, odd_k_norm, odd_w_out, ffn_w_gate, ffn_w_up, ffn_w_down):
    b, s, d = x.shape
    depth = attn_norm.shape[0]
    scale = HEAD_DIM ** -0.5
    scale2 = scale * LOG2E
    slopes_a = _alibi_slopes(A_Q_HEADS)
    slopes_c = _alibi_slopes(C_HEADS)
    qa, kva, hb = A_Q_HEADS * HEAD_DIM, A_KV_HEADS * HEAD_DIM, B_HEADS * HEAD_DIM
    hc = C_HEADS * HEAD_DIM
    ones = lambda n: jnp.ones((n,), F32)
    zeros = lambda n: jnp.zeros((n,), F32)

    ffn_cast = lambda i: [(ffn_w_gate, i), (ffn_w_up, i), (ffn_w_down, i)]
    w_in = even_w_in[0].astype(BF16)

    x2 = x.reshape(b * s, d)
    for i in range(depth):
        j = i // 2
        if i % 2 == 0:
            colgain = jnp.concatenate([
                jnp.tile(even_q_norm[j].astype(F32), A_Q_HEADS) * scale2,
                jnp.tile(even_k_norm[j].astype(F32), A_KV_HEADS),
                ones(kva), ones(hb) * scale2, ones(hb), ones(hb)])
            colflag = jnp.concatenate([ones(qa + kva), zeros(kva + 3 * hb)])
            proj = _norm_proj(x2, attn_norm[i], w_in, colgain, colflag,
                              qa + kva, PROJ_ROWS, PROJ_CHUNK_EVEN).reshape(b, s, -1)
            oa, w_gate = _swa_attention(proj, even_sinks[j], slopes_a, cast=ffn_cast(i)[:1])
            qb_col = (qa + 2 * kva) // LANES
            ob, w_up, w_down, w_out, w_in = _sb_attention(
                proj, qb_col, qb_col + hb // LANES, qb_col + 2 * hb // LANES, SB_PAIRS, SB_Q_UNROLL,
                cast=ffn_cast(i)[1:] + [(even_w_out, j), (odd_w_in, j)])
            acts = [oa.reshape(b * s, qa), ob.reshape(b * s, hb)]
        else:
            colgain = jnp.concatenate([
                jnp.tile(odd_q_norm[j].astype(F32), C_HEADS) * scale2,
                jnp.tile(odd_k_norm[j].astype(F32), C_HEADS), ones(hc)])
            colflag = jnp.concatenate([ones(2 * hc), zeros(hc)])
            proj = _norm_proj(x2, attn_norm[i], w_in, colgain, colflag,
                              2 * hc, PROJ_ROWS, PROJ_CHUNK_ODD).reshape(b, s, -1)
            cast = ffn_cast(i) + [(odd_w_out, j)]
            if i + 1 < depth:
                cast.append((even_w_in, j + 1))
            oc, w_gate, w_up, w_down, w_out, *rest = _dilated_attention(proj, slopes_c, cast=cast)
            w_in = rest[0] if rest else None
            acts = [oc.reshape(b * s, hc)]
        x2 = _out_proj(x2, acts, w_out, OUT_ROWS)
        x2 = _ffn(x2, ffn_norm[i], w_gate, w_up, w_down, FFN_ROWS, FFN_COLS)
    return x2.reshape(b, s, d)
```

```python
import functools

import jax
import jax.numpy as jnp
from jax import lax
from jax.experimental import pallas as pl
from jax.experimental.pallas import tpu as pltpu

HEAD_DIM = 64
LANES = 128
BLOCK = 128
A_Q_HEADS = 16
A_KV_HEADS = 2
B_HEADS = 16
C_HEADS = 32
C_PAIRS = ((128, 1), (512, 4), (2048, 16))
RMS_EPS = 1e-6
MASKED = -1e30
LOG2E = 1.4426950408889634
SB_DECAY_DONE = 127.0
VMEM_LIMIT = 56 * 1024 * 1024
MXU_WIDTH = 256
BF16_SUBLANES = 16
TILE_UNROLL = 32
SB_PAIRS = 1
SB_Q_UNROLL = 8
PROJ_ROWS = 512
PROJ_CHUNK_EVEN = 1024
PROJ_CHUNK_ODD = 512
OUT_ROWS = 512
FFN_ROWS = 1024
FFN_COLS = 512

F32 = jnp.float32
BF16 = jnp.bfloat16


def _dot(a, b):
    return jnp.dot(a, b, preferred_element_type=F32)


def _dot_t(a, b):
    return lax.dot_general(a, b, (((1,), (1,)), ((), ())), preferred_element_type=F32)


def _params(*sem):
    return pltpu.CompilerParams(dimension_semantics=sem, vmem_limit_bytes=VMEM_LIMIT)


def _cast_specs(passengers, n_steps, step_of):
    in_specs, out_specs, out_shapes = [], [], []
    for w, layer in passengers:
        _, rows, cols = w.shape
        n_col = next(c for c in (1, 2, 4, 8) if n_steps % c == 0
                     and rows % (BF16_SUBLANES * (n_steps // c)) == 0
                     and cols % (LANES * c) == 0)
        slab = (rows // (n_steps // n_col), cols // n_col)
        in_specs.append(pl.BlockSpec(
            (1,) + slab,
            lambda *g, layer=layer, n_col=n_col: (layer, step_of(*g) // n_col, step_of(*g) % n_col)))
        out_specs.append(pl.BlockSpec(
            slab, lambda *g, n_col=n_col: (step_of(*g) // n_col, step_of(*g) % n_col)))
        out_shapes.append(jax.ShapeDtypeStruct((rows, cols), BF16))
    return in_specs, out_specs, out_shapes


def _cast_rows(src_refs, dst_refs):
    for src, dst in zip(src_refs, dst_refs):
        dst[...] = src[0].astype(BF16)


def _split_refs(refs, n_in, n_cast):
    a, b, c = n_in + n_cast, n_in + n_cast + 1, n_in + 2 * n_cast + 1
    return refs[:n_in], refs[n_in:a], refs[a], refs[b:c], refs[c:]


def _proj_body(x_ref, g_ref, w_ref, cg_ref, cf_ref, s_ref, o_ref, *, n_norm_cols, chunk):
    x = x_ref[...]
    inv = lax.rsqrt(jnp.mean(x * x, axis=-1, keepdims=True) + RMS_EPS)
    h = ((x * inv) * g_ref[...]).astype(BF16)
    n_out = o_ref.shape[1]
    for c0 in range(0, n_out, chunk):
        width = min(chunk, n_out - c0)
        y = _dot(h, w_ref[:, c0:c0 + width])
        for c in range(c0, c0 + width, MXU_WIDTH):
            sl = slice(c, c + MXU_WIDTH)
            yc = y[:, c - c0:c - c0 + MXU_WIDTH]
            cg = cg_ref[:, sl]
            if c < n_norm_cols:
                ss = _dot((yc * yc).astype(BF16), s_ref[...])
                inv_h = lax.rsqrt(ss * (1.0 / HEAD_DIM) + RMS_EPS)
                scale = jnp.where(cf_ref[:, sl] > 0.0, inv_h * cg, cg)
            else:
                scale = cg
            o_ref[:, sl] = (yc * scale).astype(BF16)


def _norm_proj(x2, gain, w, colgain, colflag, n_norm_cols, tm, chunk):
    n, d = x2.shape
    n_out = w.shape[1]
    assert n % tm == 0 and n_out % MXU_WIDTH == 0 and chunk % MXU_WIDTH == 0
    idx = jnp.arange(MXU_WIDTH) // HEAD_DIM
    seg = (idx[:, None] == idx[None, :]).astype(BF16)
    const = lambda shape: pl.BlockSpec(shape, lambda i: (0, 0), pipeline_mode=pl.Buffered(1))
    return pl.pallas_call(
        functools.partial(_proj_body, n_norm_cols=n_norm_cols, chunk=chunk),
        grid=(n // tm,),
        in_specs=[
            pl.BlockSpec((tm, d), lambda i: (i, 0)),
            const((1, d)),
            const((d, n_out)),
            const((1, n_out)),
            const((1, n_out)),
            const((MXU_WIDTH, MXU_WIDTH)),
        ],
        out_specs=pl.BlockSpec((tm, n_out), lambda i: (i, 0)),
        out_shape=jax.ShapeDtypeStruct((n, n_out), BF16),
        compiler_params=_params("parallel"),
        name="norm_proj",
    )(x2, gain.reshape(1, d), w, colgain.reshape(1, n_out), colflag.reshape(1, n_out), seg)


def _out_body(*refs, n_pairs):
    x_ref = refs[0]
    o_ref = refs[1 + 2 * n_pairs]
    acc = x_ref[...]
    for p in range(n_pairs):
        acc = acc + _dot(refs[1 + 2 * p][...], refs[2 + 2 * p][...])
    o_ref[...] = acc


def _out_proj(x2, acts, w, tm):
    n, d = x2.shape
    in_specs = [pl.BlockSpec((tm, d), lambda i: (i, 0))]
    args = [x2]
    offset = 0
    for a in acts:
        kk = a.shape[1]
        assert offset % kk == 0
        in_specs.append(pl.BlockSpec((tm, kk), lambda i: (i, 0)))
        in_specs.append(pl.BlockSpec((kk, d), lambda i, r=offset // kk: (r, 0)))
        args += [a, w]
        offset += kk
    assert offset == w.shape[0] and n % tm == 0
    return pl.pallas_call(
        functools.partial(_out_body, n_pairs=len(acts)),
        grid=(n // tm,),
        in_specs=in_specs,
        out_specs=pl.BlockSpec((tm, d), lambda i: (i, 0)),
        out_shape=jax.ShapeDtypeStruct((n, d), F32),
        compiler_params=_params("parallel"),
        name="out_proj",
    )(*args)


def _ffn_body(x_ref, g_ref, wg_ref, wu_ref, wd_ref, o_ref, h_scr):
    j = pl.program_id(1)

    @pl.when(j == 0)
    def _():
        x = x_ref[...]
        inv = lax.rsqrt(jnp.mean(x * x, axis=-1, keepdims=True) + RMS_EPS)
        h_scr[...] = ((x * inv) * g_ref[...]).astype(BF16)
        o_ref[...] = x

    h = h_scr[...]
    gate = _dot(h, wg_ref[...])
    up = _dot(h, wu_ref[...])
    act = (gate * (1.0 / (1.0 + jnp.exp(-gate)))) * up
    o_ref[...] += _dot(act.astype(BF16), wd_ref[...])


def _ffn(x2, gain, wg, wu, wd, tm, tf):
    n, d = x2.shape
    dff = wg.shape[1]
    assert n % tm == 0 and dff % tf == 0
    return pl.pallas_call(
        _ffn_body,
        grid=(n // tm, dff // tf),
        in_specs=[
            pl.BlockSpec((tm, d), lambda i, j: (i, 0)),
            pl.BlockSpec((1, d), lambda i, j: (0, 0)),
            pl.BlockSpec((d, tf), lambda i, j: (0, j)),
            pl.BlockSpec((d, tf), lambda i, j: (0, j)),
            pl.BlockSpec((tf, d), lambda i, j: (j, 0)),
        ],
        out_specs=pl.BlockSpec((tm, d), lambda i, j: (i, 0)),
        out_shape=jax.ShapeDtypeStruct((n, d), F32),
        scratch_shapes=[pltpu.VMEM((tm, d), BF16)],
        compiler_params=_params("parallel", "arbitrary"),
        name="ffn",
    )(x2, gain.reshape(1, d), wg, wu, wd)


def _stack_heads(q):
    lo = lax.broadcasted_iota(jnp.int32, q.shape, 1) < HEAD_DIM
    zero = jnp.zeros_like(q)
    return jnp.concatenate([jnp.where(lo, q, zero), jnp.where(lo, zero, q)], axis=0)


def _unstack_heads(t):
    lo = lax.broadcasted_iota(jnp.int32, (BLOCK, LANES), 1) < HEAD_DIM
    return jnp.where(lo, t[:BLOCK], t[BLOCK:])


def _split_heads(v):
    lo = lax.broadcasted_iota(jnp.int32, v.shape, 1) < HEAD_DIM
    zero = jnp.zeros_like(v)
    return jnp.where(lo, v, zero), jnp.where(lo, zero, v)


def _band_rows(ref, n, lanes=slice(None)):
    r0 = pl.multiple_of(n * BLOCK, BLOCK)
    p0 = pl.multiple_of(jnp.maximum(n - 1, 0) * BLOCK, BLOCK)
    return jnp.concatenate([ref[pl.ds(p0, BLOCK), lanes], ref[pl.ds(r0, BLOCK), lanes]], axis=0)


def _band_softmax(lhs, k_band, va_band, vb_band, bias, sink=None):
    s = _dot_t(lhs, k_band) + bias
    m = jnp.broadcast_to(jnp.max(s, axis=1, keepdims=True), (2 * BLOCK, LANES))
    if sink is not None:
        m = jnp.maximum(m, sink)
    p = jnp.exp2(s - jnp.concatenate([m, m], axis=1)).astype(BF16)
    p_cat = jnp.concatenate([p[:BLOCK], p[BLOCK:]], axis=1)
    lo = lax.broadcasted_iota(jnp.int32, (2 * BLOCK, LANES), 1) < HEAD_DIM
    ones_a = jnp.where(lo, 1.0, 0.0).astype(BF16)
    ones_b = jnp.where(lo, 0.0, 1.0).astype(BF16)
    rhs = jnp.concatenate([jnp.concatenate([va_band, vb_band], axis=0),
                           jnp.concatenate([ones_a, ones_b], axis=0)], axis=1)
    ext = _dot(p_cat, rhs)
    return ext[:, :LANES], _unstack_heads(m), ext[:, LANES:]


def _band_bias(slopes, window_max, dist_scale, strict):
    i = jnp.arange(BLOCK)[:, None]
    j = jnp.arange(2 * BLOCK)[None, :]
    dist = i + BLOCK - j
    valid = (dist >= 0) & ((dist < window_max) if strict else (dist <= window_max))
    alibi = (slopes[:, None, None] * (dist * dist_scale).astype(F32)[None]) * LOG2E
    h = slopes.shape[0]
    table = jnp.stack([jnp.where(valid[None], -alibi, MASKED),
                       jnp.where((valid & (j >= BLOCK))[None], -alibi, MASKED)], axis=1)
    table = table.reshape(h // 2, 2, 2, BLOCK, 2 * BLOCK).transpose(0, 2, 1, 3, 4)
    return table.reshape(h // 2, 2, 2 * BLOCK, 2 * BLOCK)


def _pair_rows(t):
    h = t.shape[0]
    return t.reshape((h // 2, 2 * t.shape[1]) + t.shape[2:])


def _swa_body(*refs, n_cast):
    ins, cast_in, o_ref, cast_out, (kd_scr, va_scr, vb_scr) = _split_refs(refs, 5, n_cast)
    q_ref, k_ref, v_ref, bias_ref, sink_ref = ins
    _cast_rows(cast_in, cast_out)
    s = q_ref.shape[1]
    c = pl.program_id(1)
    blocks_per_kv = A_Q_HEADS // A_KV_HEADS // 2
    kv_head = c // blocks_per_kv

    @pl.when(c % blocks_per_kv == 0)
    def _():
        lane_head = lax.broadcasted_iota(jnp.int32, (s, LANES), 1) // HEAD_DIM
        sel = lane_head == kv_head
        kf = jnp.where(sel, k_ref[0].astype(F32), 0.0)
        kd_scr[...] = (kf + pltpu.roll(kf, HEAD_DIM, axis=1)).astype(BF16)
        vf = jnp.where(sel, v_ref[0].astype(F32), 0.0)
        vr = pltpu.roll(vf, HEAD_DIM, axis=1)
        va_scr[...] = jnp.where(kv_head == 0, vf, vr).astype(BF16)
        vb_scr[...] = jnp.where(kv_head == 0, vr, vf).astype(BF16)

    def block(n, carry):
        rows = pl.ds(pl.multiple_of(n * BLOCK, BLOCK), BLOCK)
        lhs = _stack_heads(q_ref[0, rows, :])
        first = jnp.where(n == 0, 1, 0)
        num, m, den = _band_softmax(lhs, _band_rows(kd_scr, n), _band_rows(va_scr, n),
                                    _band_rows(vb_scr, n), bias_ref[0, first],
                                    sink=sink_ref[0, :2 * BLOCK])
        den = den + jnp.exp2(sink_ref[0, 2 * BLOCK:] - m)
        o_ref[0, rows, :] = (num / den).astype(BF16)
        return carry

    lax.fori_loop(0, s // BLOCK, block, 0, unroll=TILE_UNROLL)


def _swa_attention(proj, sinks, slopes, cast=()):
    b, s, _ = proj.shape
    n_blocks = A_Q_HEADS // 2
    cast_in, cast_out, cast_shapes = _cast_specs(cast, b * n_blocks,
                                                 lambda i, c: i * n_blocks + c)
    k_col = A_Q_HEADS * HEAD_DIM // LANES
    bias = _band_bias(slopes, BLOCK, 1, strict=True)
    sink2 = sinks.astype(F32) * LOG2E
    stacked = _pair_rows(jnp.broadcast_to(sink2[:, None, None], (A_Q_HEADS, BLOCK, LANES)))
    by_lane = jnp.broadcast_to(jnp.repeat(sink2, HEAD_DIM).reshape(n_blocks, 1, LANES),
                               (n_blocks, BLOCK, LANES))
    sink = jnp.concatenate([stacked, by_lane], axis=1)
    return pl.pallas_call(
        functools.partial(_swa_body, n_cast=len(cast)),
        grid=(b, n_blocks),
        in_specs=[
            pl.BlockSpec((1, s, LANES), lambda i, c: (i, 0, c)),
            pl.BlockSpec((1, s, LANES), lambda i, c: (i, 0, k_col)),
            pl.BlockSpec((1, s, LANES), lambda i, c: (i, 0, k_col + 1)),
            pl.BlockSpec((1, 2, 2 * BLOCK, 2 * BLOCK), lambda i, c: (c, 0, 0, 0)),
            pl.BlockSpec((1, 3 * BLOCK, LANES), lambda i, c: (c, 0, 0)),
        ] + cast_in,
        out_specs=[pl.BlockSpec((1, s, LANES), lambda i, c: (i, 0, c))] + cast_out,
        out_shape=[jax.ShapeDtypeStruct((b, s, n_blocks * LANES), BF16)] + cast_shapes,
        scratch_shapes=[pltpu.VMEM((s, LANES), BF16)] * 3,
        compiler_params=_params("arbitrary", "arbitrary"),
        name="swa_attention",
    )(proj, proj, proj, bias, sink, *[w for w, _ in cast])


def _sb_softplus(z):
    return jnp.maximum(z, 0.0) + jnp.log2(1.0 + jnp.exp2(-jnp.abs(z)))


def _hi_lo(x):
    hi = x.astype(BF16)
    lo = (x - hi.astype(F32)).astype(BF16)
    return jnp.concatenate([hi, lo], axis=1)


def _sb_weighted_values(w, va, vb):
    return _dot(jnp.concatenate([w[:BLOCK], w[BLOCK:]], axis=1),
                jnp.concatenate([va, vb], axis=0))


def _sb_band_scores(lhs, k_band, first_pen, causal):
    z = _dot_t(lhs, k_band)
    z_prev, z_diag = z[:, :BLOCK] + first_pen, z[:, BLOCK:]
    drop = jnp.concatenate([_sb_softplus(z_prev), jnp.where(causal, _sb_softplus(z_diag), 0.0)],
                           axis=1)
    return z_prev, z_diag, _hi_lo(drop)


def _sb_band_weights(z_prev, z_diag, drop_hi_lo, u_band, causal):
    cs = _dot(drop_hi_lo, u_band)
    w = jnp.concatenate([jnp.exp2(z_prev - cs[:, :BLOCK]),
                         jnp.where(causal, jnp.exp2(z_diag - cs[:, BLOCK:]), 0.0)], axis=1)
    return w.astype(BF16), jnp.broadcast_to(cs[:, :1], (2 * BLOCK, LANES))


def _sb_body(*refs, pairs, q_unroll, n_cast):
    ins, cast_in, o_ref, cast_out, scratch = _split_refs(refs, 5, n_cast)
    q_ref, k_ref, v_ref, ub_ref, u_ref = ins
    va_scr, vb_scr, acc_scr, run_scr = scratch
    _cast_rows(cast_in, cast_out)
    s = q_ref.shape[1]
    lo = lax.broadcasted_iota(jnp.int32, (s, pairs * LANES), 1) % LANES < HEAD_DIM
    v_all = v_ref[0]
    va_scr[...] = jnp.where(lo, v_all, jnp.zeros_like(v_all))
    vb_scr[...] = jnp.where(lo, jnp.zeros_like(v_all), v_all)
    causal = (lax.broadcasted_iota(jnp.int32, (2 * BLOCK, BLOCK), 1)
              < lax.broadcasted_iota(jnp.int32, (2 * BLOCK, BLOCK), 0) % BLOCK)
    k2 = k_ref.at[0]
    lanes = [slice(g * LANES, (g + 1) * LANES) for g in range(pairs)]

    def q_group(i, carry):
        blocks = [i * q_unroll + qi for qi in range(q_unroll)]
        tiles = [(qi, n, g) for qi, n in enumerate(blocks) for g in range(pairs)]
        scores = {}
        for qi, n, g in tiles:
            rows = pl.ds(pl.multiple_of(n * BLOCK, BLOCK), BLOCK)
            lhs = _stack_heads(q_ref[0, rows, lanes[g]])
            scores[qi, g] = _sb_band_scores(lhs, _band_rows(k2, n, lanes[g]),
                                            jnp.where(n > 0, 0.0, MASKED), causal)
        weights = {}
        for qi, n, g in tiles:
            weights[qi, g] = _sb_band_weights(*scores[qi, g], ub_ref[...], causal)
        decay = [jnp.float32(jnp.inf)] * q_unroll
        for qi, n, g in tiles:
            w, total = weights[qi, g]
            acc_scr[qi, g] = _sb_weighted_values(w, _band_rows(va_scr, n, lanes[g]),
                                                 _band_rows(vb_scr, n, lanes[g]))
            run_scr[qi, g] = total
            decay[qi] = jnp.minimum(decay[qi], jnp.min(total))

        for qi, n in enumerate(blocks):
            rows = pl.ds(pl.multiple_of(n * BLOCK, BLOCK), BLOCK)

            def more(st):
                j, least = st
                return jnp.logical_and(j >= 0, least < SB_DECAY_DONE)

            def key_block(st, qi=qi, rows=rows):
                j, _ = st
                keys = pl.ds(pl.multiple_of(j * BLOCK, BLOCK), BLOCK)
                zs = [_dot_t(_stack_heads(q_ref[0, rows, lanes[g]]), k_ref[0, keys, lanes[g]])
                      for g in range(pairs)]
                cs = [_dot(_hi_lo(_sb_softplus(z)), u_ref[...]) for z in zs]
                least = jnp.float32(jnp.inf)
                for g in range(pairs):
                    run = run_scr[qi, g]
                    w = jnp.exp2(zs[g] - cs[g][:, :BLOCK] - run).astype(BF16)
                    acc_scr[qi, g] += _sb_weighted_values(w, va_scr[keys, lanes[g]],
                                                          vb_scr[keys, lanes[g]])
                    run = run + cs[g][:, BLOCK:]
                    run_scr[qi, g] = run
                    least = jnp.minimum(least, jnp.min(run))
                return j - 1, least

            lax.while_loop(more, key_block, (n - 2, decay[qi]))
            for g in range(pairs):
                o_ref[0, rows, lanes[g]] = acc_scr[qi, g].astype(BF16)
        return carry

    lax.fori_loop(0, s // BLOCK // q_unroll, q_group, 0)


def _sb_attention(proj, q_col, k_col, v_col, pairs, q_unroll, cast=()):
    b, s, _ = proj.shape
    width = pairs * LANES
    n_steps = B_HEADS // 2 // pairs
    cast_in, cast_out, cast_shapes = _cast_specs(cast, b * n_steps, lambda i, c: i * n_steps + c)
    assert q_col % pairs == 0 and k_col % pairs == 0 and v_col % pairs == 0
    assert (s // BLOCK) % q_unroll == 0

    def suffix(n):
        kk = jnp.arange(n)
        return (kk[:, None] >= kk[None, :]).astype(BF16)

    u_band = jnp.tile(suffix(2 * BLOCK), (2, 1))
    u_block = jnp.tile(jnp.concatenate([suffix(BLOCK), jnp.ones((BLOCK, BLOCK), BF16)], axis=1),
                       (2, 1))
    return pl.pallas_call(
        functools.partial(_sb_body, pairs=pairs, q_unroll=q_unroll, n_cast=len(cast)),
        grid=(b, n_steps),
        in_specs=[
            pl.BlockSpec((1, s, width), lambda i, c: (i, 0, q_col // pairs + c)),
            pl.BlockSpec((1, s, width), lambda i, c: (i, 0, k_col // pairs + c)),
            pl.BlockSpec((1, s, width), lambda i, c: (i, 0, v_col // pairs + c)),
            pl.BlockSpec((4 * BLOCK, 2 * BLOCK), lambda i, c: (0, 0)),
            pl.BlockSpec((2 * BLOCK, 2 * BLOCK), lambda i, c: (0, 0)),
        ] + cast_in,
        out_specs=[pl.BlockSpec((1, s, width), lambda i, c: (i, 0, c))] + cast_out,
        out_shape=[jax.ShapeDtypeStruct((b, s, B_HEADS * HEAD_DIM), BF16)] + cast_shapes,
        scratch_shapes=[pltpu.VMEM((s, width), BF16), pltpu.VMEM((s, width), BF16),
                        pltpu.VMEM((q_unroll, pairs, BLOCK, LANES), F32),
                        pltpu.VMEM((q_unroll, pairs, 2 * BLOCK, LANES), F32)],
        compiler_params=_params("arbitrary", "arbitrary"),
        name="stick_breaking",
    )(proj, proj, proj, u_band, u_block, *[w for w, _ in cast])


RESIDUE_STEP = 4


def _to_residue_major(src, dst, seg):
    part = seg // RESIDUE_STEP
    for base in range(0, src.shape[0], seg):
        for r in range(RESIDUE_STEP):
            dst[base + r * part:base + (r + 1) * part, :] = (
                src[pl.ds(base + r, part, stride=RESIDUE_STEP), :])


def _from_residue_major(src, dst, seg):
    part = seg // RESIDUE_STEP
    for base in range(0, src.shape[0], seg):
        for r in range(RESIDUE_STEP):
            dst[pl.ds(base + r, part, stride=RESIDUE_STEP), :] = (
                src[base + r * part:base + (r + 1) * part, :])


def _dil_body(*refs, n_cast):
    ins, cast_in, o_ref, cast_out, scratch = _split_refs(refs, 4, n_cast)
    q_ref, k_ref, v_ref, bias_ref = ins
    tmp_a, tmp_b, q4, k4, q16, k16, va1, vb1, va4, vb4, va16, vb16, st_a, st_b = scratch
    _cast_rows(cast_in, cast_out)
    s = q_ref.shape[1]
    va1[...], vb1[...] = _split_heads(v_ref[0])
    for src, d4, d16 in ((q_ref, (q4,), (q16,)), (k_ref, (k4,), (k16,)),
                         (v_ref, (va4, vb4), (va16, vb16))):
        tmp_a[...] = src[0].astype(F32)
        _to_residue_major(tmp_a, tmp_b, s)
        _to_residue_major(tmp_b, tmp_a, s // RESIDUE_STEP)
        for dsts, tmp in ((d4, tmp_b), (d16, tmp_a)):
            vals = tmp[...].astype(BF16)
            if len(dsts) == 1:
                dsts[0][...] = vals
            else:
                dsts[0][...], dsts[1][...] = _split_heads(vals)

    layouts = ((q_ref.at[0], k_ref.at[0], va1, vb1), (q4, k4, va4, vb4), (q16, k16, va16, vb16))
    state, spare = st_a, st_b
    for step, branch in enumerate(reversed(range(len(C_PAIRS)))):
        qb, kb, va, vb = layouts[branch]
        class_blocks = s // C_PAIRS[branch][1] // BLOCK
        if step > 0:
            for a in range(3):
                _from_residue_major(state.at[a], spare.at[a],
                                    s // RESIDUE_STEP if step == 1 else s)
            state, spare = spare, state

        def tile(t, carry, step=step, branch=branch, qb=qb, kb=kb, va=va, vb=vb, state=state,
                 class_blocks=class_blocks):
            rows = pl.ds(pl.multiple_of(t * BLOCK, BLOCK), BLOCK)
            lhs = _stack_heads(qb[rows, :])
            first = jnp.where(t % class_blocks == 0, 1, 0)
            num, m, den = _band_softmax(lhs, _band_rows(kb, t), _band_rows(va, t),
                                        _band_rows(vb, t), bias_ref[branch, 0, first])
            if step > 0:
                m_old = state[0, rows, :]
                m_new = jnp.maximum(m_old, m)
                a_old = jnp.exp2(m_old - m_new)
                a_cur = jnp.exp2(m - m_new)
                num = a_old * state[2, rows, :] + a_cur * num
                den = a_old * state[1, rows, :] + a_cur * den
                m = m_new
            if branch == 0:
                o_ref[0, rows, :] = (num / den).astype(BF16)
            else:
                state[0, rows, :] = m
                state[1, rows, :] = den
                state[2, rows, :] = num
            return carry

        lax.fori_loop(0, s // BLOCK, tile, 0, unroll=TILE_UNROLL)


def _dilated_attention(proj, slopes, cast=()):
    b, s, _ = proj.shape
    n_blocks = C_HEADS // 2
    cast_in, cast_out, cast_shapes = _cast_specs(cast, b * n_blocks,
                                                 lambda i, c: i * n_blocks + c)
    assert C_PAIRS[0][1] == 1 and C_PAIRS[1][1] == RESIDUE_STEP
    assert C_PAIRS[2][1] == RESIDUE_STEP ** 2
    bias = jnp.stack([_band_bias(slopes, w // d, d, strict=False)
                      for (w, d) in C_PAIRS])
    scr = ([pltpu.VMEM((s, LANES), F32)] * 2 + [pltpu.VMEM((s, LANES), BF16)] * 10
           + [pltpu.VMEM((3, s, LANES), F32)] * 2)
    return pl.pallas_call(
        functools.partial(_dil_body, n_cast=len(cast)),
        grid=(b, n_blocks),
        in_specs=[
            pl.BlockSpec((1, s, LANES), lambda i, c: (i, 0, c)),
            pl.BlockSpec((1, s, LANES), lambda i, c: (i, 0, n_blocks + c)),
            pl.BlockSpec((1, s, LANES), lambda i, c: (i, 0, 2 * n_blocks + c)),
            pl.BlockSpec((len(C_PAIRS), 1, 2, 2 * BLOCK, 2 * BLOCK),
                         lambda i, c: (0, c, 0, 0, 0)),
        ] + cast_in,
        out_specs=[pl.BlockSpec((1, s, LANES), lambda i, c: (i, 0, c))] + cast_out,
        out_shape=[jax.ShapeDtypeStruct((b, s, n_blocks * LANES), BF16)] + cast_shapes,
        scratch_shapes=scr,
        compiler_params=_params("arbitrary", "arbitrary"),
        name="dilated_mixture",
    )(proj, proj, proj, bias, *[w for w, _ in cast])


def _alibi_slopes(n):
    return jnp.exp2(-8.0 * jnp.arange(1, n + 1, dtype=F32) / n)


def kernel(x, attn_norm, ffn_norm, even_w_in, even_q_norm, even_k_norm, even_sinks, even_w_out,
           odd_w_in, odd_q_norm, odd_k_norm, odd_w_out, ffn_w_gate, ffn_w_up, ffn_w_down):
    b, s, d = x.shape
    depth = attn_norm.shape[0]
    scale = HEAD_DIM ** -0.5
    scale2 = scale * LOG2E
    slopes_a = _alibi_slopes(A_Q_HEADS)
    slopes_c = _alibi_slopes(C_HEADS)
    qa, kva, hb = A_Q_HEADS * HEAD_DIM, A_KV_HEADS * HEAD_DIM, B_HEADS * HEAD_DIM
    hc = C_HEADS * HEAD_DIM
    ones = lambda n: jnp.ones((n,), F32)
    zeros = lambda n: jnp.zeros((n,), F32)

    ffn_cast = lambda i: [(ffn_w_gate, i), (ffn_w_up, i), (ffn_w_down, i)]
    w_in = even_w_in[0].astype(BF16)

    x2 = x.reshape(b * s, d)
    for i in range(depth):
        j = i // 2
        if i % 2 == 0:
            colgain = jnp.concatenate([
                jnp.tile(even_q_norm[j].astype(F32), A_Q_HEADS) * scale2,
                jnp.tile(even_k_norm[j].astype(F32), A_KV_HEADS),
                ones(kva), ones(hb) * scale2, ones(hb), ones(hb)])
            colflag = jnp.concatenate([ones(qa + kva), zeros(kva + 3 * hb)])
            proj = _norm_proj(x2, attn_norm[i], w_in, colgain, colflag,
                              qa + kva, PROJ_ROWS, PROJ_CHUNK_EVEN).reshape(b, s, -1)
            oa, w_gate = _swa_attention(proj, even_sinks[j], slopes_a, cast=ffn_cast(i)[:1])
            qb_col = (qa + 2 * kva) // LANES
            ob, w_up, w_down, w_out, w_in = _sb_attention(
                proj, qb_col, qb_col + hb // LANES, qb_col + 2 * hb // LANES, SB_PAIRS, SB_Q_UNROLL,
                cast=ffn_cast(i)[1:] + [(even_w_out, j), (odd_w_in, j)])
            acts = [oa.reshape(b * s, qa), ob.reshape(b * s, hb)]
        else:
            colgain = jnp.concatenate([
                jnp.tile(odd_q_norm[j].astype(F32), C_HEADS) * scale2,
                jnp.tile(odd_k_norm[j].astype(F32), C_HEADS), ones(hc)])
            colflag = jnp.concatenate([ones(2 * hc), zeros(hc)])
            proj = _norm_proj(x2, attn_norm[i], w_in, colgain, colflag,
                              2 * hc, PROJ_ROWS, PROJ_CHUNK_ODD).reshape(b, s, -1)
            cast = ffn_cast(i) + [(odd_w_out, j)]
            if i + 1 < depth:
                cast.append((even_w_in, j + 1))
            oc, w_gate, w_up, w_down, w_out, *rest = _dilated_attention(proj, slopes_c, cast=cast)
            w_in = rest[0] if rest else None
            acts = [oc.reshape(b * s, hc)]
        x2 = _out_proj(x2, acts, w_out, OUT_ROWS)
        x2 = _ffn(x2, ffn_norm[i], w_gate, w_up, w_down, FFN_ROWS, FFN_COLS)
    return x2.reshape(b, s, d)
```

```python
import functools

import jax
import jax.numpy as jnp
from jax import lax
from jax.experimental import pallas as pl
from jax.experimental.pallas import tpu as pltpu

HEAD_DIM = 64
LANES = 128
BLOCK = 128
A_Q_HEADS = 16
A_KV_HEADS = 2
B_HEADS = 16
C_HEADS = 32
C_PAIRS = ((128, 1), (512, 4), (2048, 16))
RMS_EPS = 1e-6
MASKED = -1e30
LOG2E = 1.4426950408889634
SB_DECAY_DONE = 127.0
VMEM_LIMIT = 56 * 1024 * 1024
MXU_WIDTH = 256
BF16_SUBLANES = 16
TILE_UNROLL = 32
SB_PAIRS = 2
SB_Q_UNROLL = 2
PROJ_ROWS = 512
PROJ_CHUNK_EVEN = 1024
PROJ_CHUNK_ODD = 512
OUT_ROWS = 512
FFN_ROWS = 1024
FFN_COLS = 512

F32 = jnp.float32
BF16 = jnp.bfloat16


def _dot(a, b):
    return jnp.dot(a, b, preferred_element_type=F32)


def _dot_t(a, b):
    return lax.dot_general(a, b, (((1,), (1,)), ((), ())), preferred_element_type=F32)


def _params(*sem):
    return pltpu.CompilerParams(dimension_semantics=sem, vmem_limit_bytes=VMEM_LIMIT)


def _cast_specs(passengers, n_steps, step_of):
    in_specs, out_specs, out_shapes = [], [], []
    for w, layer in passengers:
        _, rows, cols = w.shape
        n_col = next(c for c in (1, 2, 4, 8) if n_steps % c == 0
                     and rows % (BF16_SUBLANES * (n_steps // c)) == 0
                     and cols % (LANES * c) == 0)
        slab = (rows // (n_steps // n_col), cols // n_col)
        in_specs.append(pl.BlockSpec(
            (1,) + slab,
            lambda *g, layer=layer, n_col=n_col: (layer, step_of(*g) // n_col, step_of(*g) % n_col)))
        out_specs.append(pl.BlockSpec(
            slab, lambda *g, n_col=n_col: (step_of(*g) // n_col, step_of(*g) % n_col)))
        out_shapes.append(jax.ShapeDtypeStruct((rows, cols), BF16))
    return in_specs, out_specs, out_shapes


def _cast_rows(src_refs, dst_refs):
    for src, dst in zip(src_refs, dst_refs):
        dst[...] = src[0].astype(BF16)


def _split_refs(refs, n_in, n_cast):
    a, b, c = n_in + n_cast, n_in + n_cast + 1, n_in + 2 * n_cast + 1
    return refs[:n_in], refs[n_in:a], refs[a], refs[b:c], refs[c:]


def _proj_body(x_ref, g_ref, w_ref, cg_ref, cf_ref, s_ref, o_ref, *, n_norm_cols, chunk):
    x = x_ref[...]
    inv = lax.rsqrt(jnp.mean(x * x, axis=-1, keepdims=True) + RMS_EPS)
    h = ((x * inv) * g_ref[...]).astype(BF16)
    n_out = o_ref.shape[1]
    for c0 in range(0, n_out, chunk):
        width = min(chunk, n_out - c0)
        y = _dot(h, w_ref[:, c0:c0 + width])
        for c in range(c0, c0 + width, MXU_WIDTH):
            sl = slice(c, c + MXU_WIDTH)
            yc = y[:, c - c0:c - c0 + MXU_WIDTH]
            cg = cg_ref[:, sl]
            if c < n_norm_cols:
                ss = _dot((yc * yc).astype(BF16), s_ref[...])
                inv_h = lax.rsqrt(ss * (1.0 / HEAD_DIM) + RMS_EPS)
                scale = jnp.where(cf_ref[:, sl] > 0.0, inv_h * cg, cg)
            else:
                scale = cg
            o_ref[:, sl] = (yc * scale).astype(BF16)


def _norm_proj(x2, gain, w, colgain, colflag, n_norm_cols, tm, chunk):
    n, d = x2.shape
    n_out = w.shape[1]
    assert n % tm == 0 and n_out % MXU_WIDTH == 0 and chunk % MXU_WIDTH == 0
    idx = jnp.arange(MXU_WIDTH) // HEAD_DIM
    seg = (idx[:, None] == idx[None, :]).astype(BF16)
    const = lambda shape: pl.BlockSpec(shape, lambda i: (0, 0), pipeline_mode=pl.Buffered(1))
    return pl.pallas_call(
        functools.partial(_proj_body, n_norm_cols=n_norm_cols, chunk=chunk),
        grid=(n // tm,),
        in_specs=[
            pl.BlockSpec((tm, d), lambda i: (i, 0)),
            const((1, d)),
            const((d, n_out)),
            const((1, n_out)),
            const((1, n_out)),
            const((MXU_WIDTH, MXU_WIDTH)),
        ],
        out_specs=pl.BlockSpec((tm, n_out), lambda i: (i, 0)),
        out_shape=jax.ShapeDtypeStruct((n, n_out), BF16),
        compiler_params=_params("parallel"),
        name="norm_proj",
    )(x2, gain.reshape(1, d), w, colgain.reshape(1, n_out), colflag.reshape(1, n_out), seg)


def _out_body(*refs, n_pairs):
    x_ref = refs[0]
    o_ref = refs[1 + 2 * n_pairs]
    acc = x_ref[...]
    for p in range(n_pairs):
        acc = acc + _dot(refs[1 + 2 * p][...], refs[2 + 2 * p][...])
    o_ref[...] = acc


def _out_proj(x2, acts, w, tm):
    n, d = x2.shape
    in_specs = [pl.BlockSpec((tm, d), lambda i: (i, 0))]
    args = [x2]
    offset = 0
    for a in acts:
        kk = a.shape[1]
        assert offset % kk == 0
        in_specs.append(pl.BlockSpec((tm, kk), lambda i: (i, 0)))
        in_specs.append(pl.BlockSpec((kk, d), lambda i, r=offset // kk: (r, 0)))
        args += [a, w]
        offset += kk
    assert offset == w.shape[0] and n % tm == 0
    return pl.pallas_call(
        functools.partial(_out_body, n_pairs=len(acts)),
        grid=(n // tm,),
        in_specs=in_specs,
        out_specs=pl.BlockSpec((tm, d), lambda i: (i, 0)),
        out_shape=jax.ShapeDtypeStruct((n, d), F32),
        compiler_params=_params("parallel"),
        name="out_proj",
    )(*args)


def _ffn_body(x_ref, g_ref, wg_ref, wu_ref, wd_ref, o_ref, h_scr):
    j = pl.program_id(1)

    @pl.when(j == 0)
    def _():
        x = x_ref[...]
        inv = lax.rsqrt(jnp.mean(x * x, axis=-1, keepdims=True) + RMS_EPS)
        h_scr[...] = ((x * inv) * g_ref[...]).astype(BF16)
        o_ref[...] = x

    h = h_scr[...]
    gate = _dot(h, wg_ref[...])
    up = _dot(h, wu_ref[...])
    act = (gate * (1.0 / (1.0 + jnp.exp(-gate)))) * up
    o_ref[...] += _dot(act.astype(BF16), wd_ref[...])


def _ffn(x2, gain, wg, wu, wd, tm, tf):
    n, d = x2.shape
    dff = wg.shape[1]
    assert n % tm == 0 and dff % tf == 0
    return pl.pallas_call(
        _ffn_body,
        grid=(n // tm, dff // tf),
        in_specs=[
            pl.BlockSpec((tm, d), lambda i, j: (i, 0)),
            pl.BlockSpec((1, d), lambda i, j: (0, 0)),
            pl.BlockSpec((d, tf), lambda i, j: (0, j)),
            pl.BlockSpec((d, tf), lambda i, j: (0, j)),
            pl.BlockSpec((tf, d), lambda i, j: (j, 0)),
        ],
        out_specs=pl.BlockSpec((tm, d), lambda i, j: (i, 0)),
        out_shape=jax.ShapeDtypeStruct((n, d), F32),
        scratch_shapes=[pltpu.VMEM((tm, d), BF16)],
        compiler_params=_params("parallel", "arbitrary"),
        name="ffn",
    )(x2, gain.reshape(1, d), wg, wu, wd)


def _stack_heads(q):
    lo = lax.broadcasted_iota(jnp.int32, q.shape, 1) < HEAD_DIM
    zero = jnp.zeros_like(q)
    return jnp.concatenate([jnp.where(lo, q, zero), jnp.where(lo, zero, q)], axis=0)


def _unstack_heads(t):
    lo = lax.broadcasted_iota(jnp.int32, (BLOCK, LANES), 1) < HEAD_DIM
    return jnp.where(lo, t[:BLOCK], t[BLOCK:])


def _split_heads(v):
    lo = lax.broadcasted_iota(jnp.int32, v.shape, 1) < HEAD_DIM
    zero = jnp.zeros_like(v)
    return jnp.where(lo, v, zero), jnp.where(lo, zero, v)


def _band_rows(ref, n, lanes=slice(None)):
    r0 = pl.multiple_of(n * BLOCK, BLOCK)
    p0 = pl.multiple_of(jnp.maximum(n - 1, 0) * BLOCK, BLOCK)
    return jnp.concatenate([ref[pl.ds(p0, BLOCK), lanes], ref[pl.ds(r0, BLOCK), lanes]], axis=0)


def _band_softmax(lhs, k_band, va_band, vb_band, bias, sink=None):
    s = _dot_t(lhs, k_band) + bias
    m = jnp.broadcast_to(jnp.max(s, axis=1, keepdims=True), (2 * BLOCK, LANES))
    if sink is not None:
        m = jnp.maximum(m, sink)
    p = jnp.exp2(s - jnp.concatenate([m, m], axis=1)).astype(BF16)
    p_cat = jnp.concatenate([p[:BLOCK], p[BLOCK:]], axis=1)
    lo = lax.broadcasted_iota(jnp.int32, (2 * BLOCK, LANES), 1) < HEAD_DIM
    ones_a = jnp.where(lo, 1.0, 0.0).astype(BF16)
    ones_b = jnp.where(lo, 0.0, 1.0).astype(BF16)
    rhs = jnp.concatenate([jnp.concatenate([va_band, vb_band], axis=0),
                           jnp.concatenate([ones_a, ones_b], axis=0)], axis=1)
    ext = _dot(p_cat, rhs)
    return ext[:, :LANES], _unstack_heads(m), ext[:, LANES:]


def _band_bias(slopes, window_max, dist_scale, strict):
    i = jnp.arange(BLOCK)[:, None]
    j = jnp.arange(2 * BLOCK)[None, :]
    dist = i + BLOCK - j
    valid = (dist >= 0) & ((dist < window_max) if strict else (dist <= window_max))
    alibi = (slopes[:, None, None] * (dist * dist_scale).astype(F32)[None]) * LOG2E
    h = slopes.shape[0]
    table = jnp.stack([jnp.where(valid[None], -alibi, MASKED),
                       jnp.where((valid & (j >= BLOCK))[None], -alibi, MASKED)], axis=1)
    table = table.reshape(h // 2, 2, 2, BLOCK, 2 * BLOCK).transpose(0, 2, 1, 3, 4)
    return table.reshape(h // 2, 2, 2 * BLOCK, 2 * BLOCK)


def _pair_rows(t):
    h = t.shape[0]
    return t.reshape((h // 2, 2 * t.shape[1]) + t.shape[2:])


def _swa_body(*refs, n_cast):
    ins, cast_in, o_ref, cast_out, (kd_scr, va_scr, vb_scr) = _split_refs(refs, 5, n_cast)
    q_ref, k_ref, v_ref, bias_ref, sink_ref = ins
    _cast_rows(cast_in, cast_out)
    s = q_ref.shape[1]
    c = pl.program_id(1)
    blocks_per_kv = A_Q_HEADS // A_KV_HEADS // 2
    kv_head = c // blocks_per_kv

    @pl.when(c % blocks_per_kv == 0)
    def _():
        lane_head = lax.broadcasted_iota(jnp.int32, (s, LANES), 1) // HEAD_DIM
        sel = lane_head == kv_head
        kf = jnp.where(sel, k_ref[0].astype(F32), 0.0)
        kd_scr[...] = (kf + pltpu.roll(kf, HEAD_DIM, axis=1)).astype(BF16)
        vf = jnp.where(sel, v_ref[0].astype(F32), 0.0)
        vr = pltpu.roll(vf, HEAD_DIM, axis=1)
        va_scr[...] = jnp.where(kv_head == 0, vf, vr).astype(BF16)
        vb_scr[...] = jnp.where(kv_head == 0, vr, vf).astype(BF16)

    def block(n, carry):
        rows = pl.ds(pl.multiple_of(n * BLOCK, BLOCK), BLOCK)
        lhs = _stack_heads(q_ref[0, rows, :])
        first = jnp.where(n == 0, 1, 0)
        num, m, den = _band_softmax(lhs, _band_rows(kd_scr, n), _band_rows(va_scr, n),
                                    _band_rows(vb_scr, n), bias_ref[0, first],
                                    sink=sink_ref[0, :2 * BLOCK])
        den = den + jnp.exp2(sink_ref[0, 2 * BLOCK:] - m)
        o_ref[0, rows, :] = (num / den).astype(BF16)
        return carry

    lax.fori_loop(0, s // BLOCK, block, 0, unroll=TILE_UNROLL)


def _swa_attention(proj, sinks, slopes, cast=()):
    b, s, _ = proj.shape
    n_blocks = A_Q_HEADS // 2
    cast_in, cast_out, cast_shapes = _cast_specs(cast, b * n_blocks,
                                                 lambda i, c: i * n_blocks + c)
    k_col = A_Q_HEADS * HEAD_DIM // LANES
    bias = _band_bias(slopes, BLOCK, 1, strict=True)
    sink2 = sinks.astype(F32) * LOG2E
    stacked = _pair_rows(jnp.broadcast_to(sink2[:, None, None], (A_Q_HEADS, BLOCK, LANES)))
    by_lane = jnp.broadcast_to(jnp.repeat(sink2, HEAD_DIM).reshape(n_blocks, 1, LANES),
                               (n_blocks, BLOCK, LANES))
    sink = jnp.concatenate([stacked, by_lane], axis=1)
    return pl.pallas_call(
        functools.partial(_swa_body, n_cast=len(cast)),
        grid=(b, n_blocks),
        in_specs=[
            pl.BlockSpec((1, s, LANES), lambda i, c: (i, 0, c)),
            pl.BlockSpec((1, s, LANES), lambda i, c: (i, 0, k_col)),
            pl.BlockSpec((1, s, LANES), lambda i, c: (i, 0, k_col + 1)),
            pl.BlockSpec((1, 2, 2 * BLOCK, 2 * BLOCK), lambda i, c: (c, 0, 0, 0)),
            pl.BlockSpec((1, 3 * BLOCK, LANES), lambda i, c: (c, 0, 0)),
        ] + cast_in,
        out_specs=[pl.BlockSpec((1, s, LANES), lambda i, c: (i, 0, c))] + cast_out,
        out_shape=[jax.ShapeDtypeStruct((b, s, n_blocks * LANES), BF16)] + cast_shapes,
        scratch_shapes=[pltpu.VMEM((s, LANES), BF16)] * 3,
        compiler_params=_params("arbitrary", "arbitrary"),
        name="swa_attention",
    )(proj, proj, proj, bias, sink, *[w for w, _ in cast])


def _sb_softplus(z):
    return jnp.maximum(z, 0.0) + jnp.log2(1.0 + jnp.exp2(-jnp.abs(z)))


def _hi_lo(x):
    hi = x.astype(BF16)
    lo = (x - hi.astype(F32)).astype(BF16)
    return jnp.concatenate([hi, lo], axis=1)


def _sb_weighted_values(w, va, vb):
    return _dot(jnp.concatenate([w[:BLOCK], w[BLOCK:]], axis=1),
                jnp.concatenate([va, vb], axis=0))


def _sb_band_scores(lhs, k_band, first_pen, causal):
    z = _dot_t(lhs, k_band)
    z_prev, z_diag = z[:, :BLOCK] + first_pen, z[:, BLOCK:]
    drop = jnp.concatenate([_sb_softplus(z_prev), jnp.where(causal, _sb_softplus(z_diag), 0.0)],
                           axis=1)
    return z_prev, z_diag, _hi_lo(drop)


def _sb_band_weights(z_prev, z_diag, drop_hi_lo, u_band, causal):
    cs = _dot(drop_hi_lo, u_band)
    w = jnp.concatenate([jnp.exp2(z_prev - cs[:, :BLOCK]),
                         jnp.where(causal, jnp.exp2(z_diag - cs[:, BLOCK:]), 0.0)], axis=1)
    return w.astype(BF16), jnp.broadcast_to(cs[:, :1], (2 * BLOCK, LANES))


def _sb_body(*refs, pairs, q_unroll, n_cast):
    ins, cast_in, o_ref, cast_out, scratch = _split_refs(refs, 5, n_cast)
    q_ref, k_ref, v_ref, ub_ref, u_ref = ins
    va_scr, vb_scr, acc_scr, run_scr = scratch
    _cast_rows(cast_in, cast_out)
    s = q_ref.shape[1]
    lo = lax.broadcasted_iota(jnp.int32, (s, pairs * LANES), 1) % LANES < HEAD_DIM
    v_all = v_ref[0]
    va_scr[...] = jnp.where(lo, v_all, jnp.zeros_like(v_all))
    vb_scr[...] = jnp.where(lo, jnp.zeros_like(v_all), v_all)
    causal = (lax.broadcasted_iota(jnp.int32, (2 * BLOCK, BLOCK), 1)
              < lax.broadcasted_iota(jnp.int32, (2 * BLOCK, BLOCK), 0) % BLOCK)
    k2 = k_ref.at[0]
    lanes = [slice(g * LANES, (g + 1) * LANES) for g in range(pairs)]

    def q_group(i, carry):
        blocks = [i * q_unroll + qi for qi in range(q_unroll)]
        tiles = [(qi, n, g) for qi, n in enumerate(blocks) for g in range(pairs)]
        scores = {}
        for qi, n, g in tiles:
            rows = pl.ds(pl.multiple_of(n * BLOCK, BLOCK), BLOCK)
            lhs = _stack_heads(q_ref[0, rows, lanes[g]])
            scores[qi, g] = _sb_band_scores(lhs, _band_rows(k2, n, lanes[g]),
                                            jnp.where(n > 0, 0.0, MASKED), causal)
        weights = {}
        for qi, n, g in tiles:
            weights[qi, g] = _sb_band_weights(*scores[qi, g], ub_ref[...], causal)
        decay = [jnp.float32(jnp.inf)] * q_unroll
        for qi, n, g in tiles:
            w, total = weights[qi, g]
            acc_scr[qi, g] = _sb_weighted_values(w, _band_rows(va_scr, n, lanes[g]),
                                                 _band_rows(vb_scr, n, lanes[g]))
            run_scr[qi, g] = total
            decay[qi] = jnp.minimum(decay[qi], jnp.min(total))

        for qi, n in enumerate(blocks):
            rows = pl.ds(pl.multiple_of(n * BLOCK, BLOCK), BLOCK)

            def more(st):
                j, least = st
                return jnp.logical_and(j >= 0, least < SB_DECAY_DONE)

            def key_block(st, qi=qi, rows=rows):
                j, _ = st
                keys = pl.ds(pl.multiple_of(j * BLOCK, BLOCK), BLOCK)
                zs = [_dot_t(_stack_heads(q_ref[0, rows, lanes[g]]), k_ref[0, keys, lanes[g]])
                      for g in range(pairs)]
                cs = [_dot(_hi_lo(_sb_softplus(z)), u_ref[...]) for z in zs]
                least = jnp.float32(jnp.inf)
                for g in range(pairs):
                    run = run_scr[qi, g]
                    w = jnp.exp2(zs[g] - cs[g][:, :BLOCK] - run).astype(BF16)
                    acc_scr[qi, g] += _sb_weighted_values(w, va_scr[keys, lanes[g]],
                                                          vb_scr[keys, lanes[g]])
                    run = run + cs[g][:, BLOCK:]
                    run_scr[qi, g] = run
                    least = jnp.minimum(least, jnp.min(run))
                return j - 1, least

            lax.while_loop(more, key_block, (n - 2, decay[qi]))
            for g in range(pairs):
                o_ref[0, rows, lanes[g]] = acc_scr[qi, g].astype(BF16)
        return carry

    lax.fori_loop(0, s // BLOCK // q_unroll, q_group, 0)


def _sb_attention(proj, q_col, k_col, v_col, pairs, q_unroll, cast=()):
    b, s, _ = proj.shape
    width = pairs * LANES
    n_steps = B_HEADS // 2 // pairs
    cast_in, cast_out, cast_shapes = _cast_specs(cast, b * n_steps, lambda i, c: i * n_steps + c)
    assert q_col % pairs == 0 and k_col % pairs == 0 and v_col % pairs == 0
    assert (s // BLOCK) % q_unroll == 0

    def suffix(n):
        kk = jnp.arange(n)
        return (kk[:, None] >= kk[None, :]).astype(BF16)

    u_band = jnp.tile(suffix(2 * BLOCK), (2, 1))
    u_block = jnp.tile(jnp.concatenate([suffix(BLOCK), jnp.ones((BLOCK, BLOCK), BF16)], axis=1),
                       (2, 1))
    return pl.pallas_call(
        functools.partial(_sb_body, pairs=pairs, q_unroll=q_unroll, n_cast=len(cast)),
        grid=(b, n_steps),
        in_specs=[
            pl.BlockSpec((1, s, width), lambda i, c: (i, 0, q_col // pairs + c)),
            pl.BlockSpec((1, s, width), lambda i, c: (i, 0, k_col // pairs + c)),
            pl.BlockSpec((1, s, width), lambda i, c: (i, 0, v_col // pairs + c)),
            pl.BlockSpec((4 * BLOCK, 2 * BLOCK), lambda i, c: (0, 0)),
            pl.BlockSpec((2 * BLOCK, 2 * BLOCK), lambda i, c: (0, 0)),
        ] + cast_in,
        out_specs=[pl.BlockSpec((1, s, width), lambda i, c: (i, 0, c))] + cast_out,
        out_shape=[jax.ShapeDtypeStruct((b, s, B_HEADS * HEAD_DIM), BF16)] + cast_shapes,
        scratch_shapes=[pltpu.VMEM((s, width), BF16), pltpu.VMEM((s, width), BF16),
                        pltpu.VMEM((q_unroll, pairs, BLOCK, LANES), F32),
                        pltpu.VMEM((q_unroll, pairs, 2 * BLOCK, LANES), F32)],
        compiler_params=_params("arbitrary", "arbitrary"),
        name="stick_breaking",
    )(proj, proj, proj, u_band, u_block, *[w for w, _ in cast])


RESIDUE_STEP = 4


def _to_residue_major(src, dst, seg):
    part = seg // RESIDUE_STEP
    for base in range(0, src.shape[0], seg):
        for r in range(RESIDUE_STEP):
            dst[base + r * part:base + (r + 1) * part, :] = (
                src[pl.ds(base + r, part, stride=RESIDUE_STEP), :])


def _from_residue_major(src, dst, seg):
    part = seg // RESIDUE_STEP
    for base in range(0, src.shape[0], seg):
        for r in range(RESIDUE_STEP):
            dst[pl.ds(base + r, part, stride=RESIDUE_STEP), :] = (
                src[base + r * part:base + (r + 1) * part, :])


def _dil_body(*refs, n_cast):
    ins, cast_in, o_ref, cast_out, scratch = _split_refs(refs, 4, n_cast)
    q_ref, k_ref, v_ref, bias_ref = ins
    tmp_a, tmp_b, q4, k4, q16, k16, va1, vb1, va4, vb4, va16, vb16, st_a, st_b = scratch
    _cast_rows(cast_in, cast_out)
    s = q_ref.shape[1]
    va1[...], vb1[...] = _split_heads(v_ref[0])
    for src, d4, d16 in ((q_ref, (q4,), (q16,)), (k_ref, (k4,), (k16,)),
                         (v_ref, (va4, vb4), (va16, vb16))):
        tmp_a[...] = src[0].astype(F32)
        _to_residue_major(tmp_a, tmp_b, s)
        _to_residue_major(tmp_b, tmp_a, s // RESIDUE_STEP)
        for dsts, tmp in ((d4, tmp_b), (d16, tmp_a)):
            vals = tmp[...].astype(BF16)
            if len(dsts) == 1:
                dsts[0][...] = vals
            else:
                dsts[0][...], dsts[1][...] = _split_heads(vals)

    layouts = ((q_ref.at[0], k_ref.at[0], va1, vb1), (q4, k4, va4, vb4), (q16, k16, va16, vb16))
    state, spare = st_a, st_b
    for step, branch in enumerate(reversed(range(len(C_PAIRS)))):
        qb, kb, va, vb = layouts[branch]
        class_blocks = s // C_PAIRS[branch][1] // BLOCK
        if step > 0:
            for a in range(3):
                _from_residue_major(state.at[a], spare.at[a],
                                    s // RESIDUE_STEP if step == 1 else s)
            state, spare = spare, state

        def tile(t, carry, step=step, branch=branch, qb=qb, kb=kb, va=va, vb=vb, state=state,
                 class_blocks=class_blocks):
            rows = pl.ds(pl.multiple_of(t * BLOCK, BLOCK), BLOCK)
            lhs = _stack_heads(qb[rows, :])
            first = jnp.where(t % class_blocks == 0, 1, 0)
            num, m, den = _band_softmax(lhs, _band_rows(kb, t), _band_rows(va, t),
                                        _band_rows(vb, t), bias_ref[branch, 0, first])
            if step > 0:
                m_old = state[0, rows, :]
                m_new = jnp.maximum(m_old, m)
                a_old = jnp.exp2(m_old - m_new)
                a_cur = jnp.exp2(m - m_new)
                num = a_old * state[2, rows, :] + a_cur * num
                den = a_old * state[1, rows, :] + a_cur * den
                m = m_new
            if branch == 0:
                o_ref[0, rows, :] = (num / den).astype(BF16)
            else:
                state[0, rows, :] = m
                state[1, rows, :] = den
                state[2, rows, :] = num
            return carry

        lax.fori_loop(0, s // BLOCK, tile, 0, unroll=TILE_UNROLL)


def _dilated_attention(proj, slopes, cast=()):
    b, s, _ = proj.shape
    n_blocks = C_HEADS // 2
    cast_in, cast_out, cast_shapes = _cast_specs(cast, b * n_blocks,
                                                 lambda i, c: i * n_blocks + c)
    assert C_PAIRS[0][1] == 1 and C_PAIRS[1][1] == RESIDUE_STEP
    assert C_PAIRS[2][1] == RESIDUE_STEP ** 2
    bias = jnp.stack([_band_bias(slopes, w // d, d, strict=False)
                      for (w, d) in C_PAIRS])
    scr = ([pltpu.VMEM((s, LANES), F32)] * 2 + [pltpu.VMEM((s, LANES), BF16)] * 10
           + [pltpu.VMEM((3, s, LANES), F32)] * 2)
    return pl.pallas_call(
        functools.partial(_dil_body, n_cast=len(cast)),
        grid=(b, n_blocks),
        in_specs=[
            pl.BlockSpec((1, s, LANES), lambda i, c: (i, 0, c)),
            pl.BlockSpec((1, s, LANES), lambda i, c: (i, 0, n_blocks + c)),
            pl.BlockSpec((1, s, LANES), lambda i, c: (i, 0, 2 * n_blocks + c)),
            pl.BlockSpec((len(C_PAIRS), 1, 2, 2 * BLOCK, 2 * BLOCK),
                         lambda i, c: (0, c, 0, 0, 0)),
        ] + cast_in,
        out_specs=[pl.BlockSpec((1, s, LANES), lambda i, c: (i, 0, c))] + cast_out,
        out_shape=[jax.ShapeDtypeStruct((b, s, n_blocks * LANES), BF16)] + cast_shapes,
        scratch_shapes=scr,
        compiler_params=_params("arbitrary", "arbitrary"),
        name="dilated_mixture",
    )(proj, proj, proj, bias, *[w for w, _ in cast])


def _alibi_slopes(n):
    return jnp.exp2(-8.0 * jnp.arange(1, n + 1, dtype=F32) / n)


def kernel(x, attn_norm, ffn_norm, even_w_in, even_q_norm, even_k_norm, even_sinks, even_w_out,
           odd_w_in, odd_q_norm, odd_k_norm, odd_w_out, ffn_w_gate, ffn_w_up, ffn_w_down):
    b, s, d = x.shape
    depth = attn_norm.shape[0]
    scale = HEAD_DIM ** -0.5
    scale2 = scale * LOG2E
    slopes_a = _alibi_slopes(A_Q_HEADS)
    slopes_c = _alibi_slopes(C_HEADS)
    qa, kva, hb = A_Q_HEADS * HEAD_DIM, A_KV_HEADS * HEAD_DIM, B_HEADS * HEAD_DIM
    hc = C_HEADS * HEAD_DIM
    ones = lambda n: jnp.ones((n,), F32)
    zeros = lambda n: jnp.zeros((n,), F32)

    ffn_cast = lambda i: [(ffn_w_gate, i), (ffn_w_up, i), (ffn_w_down, i)]
    w_in = even_w_in[0].astype(BF16)

    x2 = x.reshape(b * s, d)
    for i in range(depth):
        j = i // 2
        if i % 2 == 0:
            colgain = jnp.concatenate([
                jnp.tile(even_q_norm[j].astype(F32), A_Q_HEADS) * scale2,
                jnp.tile(even_k_norm[j].astype(F32), A_KV_HEADS),
                ones(kva), ones(hb) * scale2, ones(hb), ones(hb)])
            colflag = jnp.concatenate([ones(qa + kva), zeros(kva + 3 * hb)])
            proj = _norm_proj(x2, attn_norm[i], w_in, colgain, colflag,
                              qa + kva, PROJ_ROWS, PROJ_CHUNK_EVEN).reshape(b, s, -1)
            oa, w_gate = _swa_attention(proj, even_sinks[j], slopes_a, cast=ffn_cast(i)[:1])
            qb_col = (qa + 2 * kva) // LANES
            ob, w_up, w_down, w_out, w_in = _sb_attention(
                proj, qb_col, qb_col + hb // LANES, qb_col + 2 * hb // LANES, SB_PAIRS, SB_Q_UNROLL,
                cast=ffn_cast(i)[1:] + [(even_w_out, j), (odd_w_in, j)])
            acts = [oa.reshape(b * s, qa), ob.reshape(b * s, hb)]
        else:
            colgain = jnp.concatenate([
                jnp.tile(odd_q_norm[j].astype(F32), C_HEADS) * scale2,
                jnp.tile(odd_k_norm[j].astype(F32), C_HEADS), ones(hc)])
            colflag = jnp.concatenate([ones(2 * hc), zeros(hc)])
            proj = _norm_proj(x2, attn_norm[i], w_in, colgain, colflag,
                              2 * hc, PROJ_ROWS, PROJ_CHUNK_ODD).reshape(b, s, -1)
            cast = ffn_cast(i) + [(odd_w_out, j)]
            if i + 1 < depth:
                cast.append((even_w_in, j + 1))
            oc, w_gate, w_up, w_down, w_out, *rest = _dilated_attention(proj, slopes_c, cast=cast)
            w_in = rest[0] if rest else None
            acts = [oc.reshape(b * s, hc)]
        x2 = _out_proj(x2, acts, w_out, OUT_ROWS)
        x2 = _ffn(x2, ffn_norm[i], w_gate, w_up, w_down, FFN_ROWS, FFN_COLS)
    return x2.reshape(b, s, d)
```
